```python
import math
import jax, jax.numpy as jnp
from jax import lax
import numpy as np

D_MODEL = 1024
BATCH = 8
SEQ = 2048
DEPTH = 2

N_MEM = 256
EPS = 1e-6
CHUNK = 128
N_BRANCH = 4

S5_WIDTH = 512
S5_GROUP = 16
S5_GROUPS = S5_WIDTH // S5_GROUP
S5_STATE = 64
S5_DT_MIN = 1e-3
S5_DT_MAX = 1e-1

SSD_HEADS = 8
SSD_HEAD_DIM = 64
SSD_WIDTH = SSD_HEADS * SSD_HEAD_DIM
SSD_GROUPS = 2
SSD_HEADS_PER_GROUP = SSD_HEADS // SSD_GROUPS
SSD_STATE = 64
SSD_CONV = 4
SSD_CONV_DIM = SSD_WIDTH + 2 * SSD_GROUPS * SSD_STATE

RET_HEADS = 8
RET_HEAD_DIM = 64
RET_WIDTH = RET_HEADS * RET_HEAD_DIM
ROPE_BASE = 10000.0

LRU_WIDTH = 512
LRU_BLOCKS = 8
LRU_BLOCK = LRU_WIDTH // LRU_BLOCKS
LRU_CONV = 4
LRU_C = 8.0

SECTION_WIDTHS = (S5_WIDTH, SSD_WIDTH, SSD_CONV_DIM, SSD_HEADS,
                  RET_WIDTH, RET_WIDTH, RET_WIDTH, RET_WIDTH,
                  LRU_WIDTH, LRU_WIDTH, N_BRANCH * D_MODEL)
IN_TOTAL = sum(SECTION_WIDTHS)
BRANCH_WIDTH = 512

XA_HEADS = 4
XA_HEAD_DIM = D_MODEL // XA_HEADS

D_FF = 2816
N_EXPERTS = 8
TOP_K = 2
D_FF_EXPERT = 3584
N_DENSE = (DEPTH + 1) // 2
N_MOE = DEPTH // 2

kernel_name = 'hybrid_gated_s5_ssd_retention_rglru_moe'


def rmsnorm(x, w):
    xf = x.astype(jnp.float32)
    xf = xf * lax.rsqrt(jnp.mean(xf * xf, axis=-1, keepdims=True) + EPS)
    return (xf * w.astype(jnp.float32)).astype(x.dtype)


def causal_dwconv(x, w, b):
    k = w.shape[0]
    y = lax.conv_general_dilated(x, w[:, None, :].astype(x.dtype), window_strides=(1,),
                                 padding=[(k - 1, 0)], dimension_numbers=('NWC', 'WIO', 'NWC'),
                                 feature_group_count=x.shape[-1])
    return y + b.astype(x.dtype)


def apply_rope(x, cos, sin):
    x1, x2 = jnp.split(x, 2, axis=-1)
    return jnp.concatenate([x1 * cos - x2 * sin, x2 * cos + x1 * sin], axis=-1)


def _combine_complex(e1, e2):
    a1r, a1i, b1r, b1i = e1
    a2r, a2i, b2r, b2i = e2
    return (a2r * a1r - a2i * a1i, a2r * a1i + a2i * a1r,
            a2r * b1r - a2i * b1i + b2r, a2r * b1i + a2i * b1r + b2i)


def _combine_real(e1, e2):
    a1, b1 = e1
    a2, b2 = e2
    return (a1 * a2, a2 * b1 + b2)


def s5_mixer(u, lam_re, lam_im, b_re, b_im, c_re, c_im, d_skip, log_dt, w_glu):
    bsz, seq, _ = u.shape
    f32 = jnp.float32
    ug = u.astype(f32).reshape(bsz, seq, S5_GROUPS, S5_GROUP)
    lr = lam_re.astype(f32)
    li = lam_im.astype(f32)
    step = jnp.exp(log_dt.astype(f32))[:, None]
    mag = jnp.exp(lr * step)
    ar = mag * jnp.cos(li * step)
    ai = mag * jnp.sin(li * step)
    inv = 1.0 / (lr * lr + li * li)
    cr = ((ar - 1.0) * lr + ai * li) * inv
    ci = (ai * lr - (ar - 1.0) * li) * inv
    br = b_re.astype(f32)
    bi = b_im.astype(f32)
    bb_r = cr[..., None] * br - ci[..., None] * bi
    bb_i = cr[..., None] * bi + ci[..., None] * br
    bu_r = jnp.einsum('blgh,gnh->blgn', ug, bb_r)
    bu_i = jnp.einsum('blgh,gnh->blgn', ug, bb_i)
    a_r = jnp.broadcast_to(ar, bu_r.shape)
    a_i = jnp.broadcast_to(ai, bu_r.shape)
    _, _, s_r, s_i = lax.associative_scan(_combine_complex, (a_r, a_i, bu_r, bu_i), axis=1)
    y = (jnp.einsum('blgn,ghn->blgh', s_r, c_re.astype(f32))
         - jnp.einsum('blgn,ghn->blgh', s_i, c_im.astype(f32))
         + d_skip.astype(f32) * ug)
    y = jax.nn.gelu(y).reshape(bsz, seq, S5_WIDTH).astype(u.dtype)
    return y * jax.nn.sigmoid(y @ w_glu)


def ssd_mixer(z, xbc, dt_raw, conv_w, conv_b, dt_bias, a_log, d_skip, norm_w):
    bsz, seq, _ = z.shape
    nc = seq // CHUNK
    f32 = jnp.float32
    xbc = jax.nn.silu(causal_dwconv(xbc, conv_w, conv_b)).astype(f32)
    xs = xbc[..., :SSD_WIDTH]
    bs = xbc[..., SSD_WIDTH:SSD_WIDTH + SSD_GROUPS * SSD_STATE]
    cs = xbc[..., SSD_WIDTH + SSD_GROUPS * SSD_STATE:]
    G, R, P, N = SSD_GROUPS, SSD_HEADS_PER_GROUP, SSD_HEAD_DIM, SSD_STATE
    x = xs.reshape(bsz, nc, CHUNK, G, R, P)
    bm = bs.reshape(bsz, nc, CHUNK, G, N)
    cm = cs.reshape(bsz, nc, CHUNK, G, N)
    dt = jax.nn.softplus(dt_raw.astype(f32) + dt_bias.astype(f32)).reshape(bsz, nc, CHUNK, G, R)
    a = -jnp.exp(a_log.astype(f32)).reshape(G, R)
    acum = jnp.cumsum(dt * a, axis=2)
    xdt = x * dt[..., None]
    seg = acum[:, :, :, None] - acum[:, :, None, :]
    causal = jnp.tril(jnp.ones((CHUNK, CHUNK), dtype=bool))[:, :, None, None]
    lmat = jnp.exp(jnp.where(causal, seg, -jnp.inf))
    cb = jnp.einsum('bclgn,bcsgn->bclsg', cm, bm)
    y_diag = jnp.einsum('bclsgr,bcsgrp->bclgrp', cb[..., None] * lmat, xdt)
    decay_end = jnp.exp(acum[:, :, -1:] - acum)
    chunk_states = jnp.einsum('bcsgn,bcsgr,bcsgrp->bcgrpn', bm, decay_end, xdt)
    chunk_decay = jnp.exp(acum[:, :, -1])

    def step(state, inp):
        cst, dec = inp
        return state * dec[..., None, None] + cst, state

    init = jnp.zeros((bsz, G, R, P, N), f32)
    _, prev = lax.scan(step, init, (jnp.moveaxis(chunk_states, 1, 0), jnp.moveaxis(chunk_decay, 1, 0)))
    prev = jnp.moveaxis(prev, 0, 1)
    y_off = jnp.einsum('bclgn,bcgrpn->bclgrp', cm, prev) * jnp.exp(acum)[..., None]
    y = y_diag + y_off + d_skip.astype(f32).reshape(G, R)[:, :, None] * x
    y = y.reshape(bsz, seq, SSD_WIDTH) * jax.nn.silu(z.astype(f32))
    return rmsnorm(y, norm_w).astype(z.dtype)


def retention_mixer(q, k, v, g, cos, sin, gn_w):
    bsz, seq, _ = q.shape
    nc = seq // CHUNK
    f32 = jnp.float32
    H, Dh = RET_HEADS, RET_HEAD_DIM
    q = apply_rope(q.reshape(bsz, seq, H, Dh), cos.astype(q.dtype), sin.astype(q.dtype))
    k = apply_rope(k.reshape(bsz, seq, H, Dh), cos.astype(k.dtype), sin.astype(k.dtype)) * (Dh ** -0.5)
    q = q.astype(f32).reshape(bsz, nc, CHUNK, H, Dh)
    k = k.astype(f32).reshape(bsz, nc, CHUNK, H, Dh)
    v = v.astype(f32).reshape(bsz, nc, CHUNK, H, Dh)
    log_gamma = jnp.log1p(-jnp.exp2(-5.0 - jnp.arange(H, dtype=f32)))
    idx = jnp.arange(CHUNK, dtype=f32)
    diff = idx[:, None] - idx[None, :]
    causal = diff >= 0
    dmat = jnp.where(causal[None], jnp.exp(jnp.where(causal, diff, 0.0)[None] * log_gamma[:, None, None]), 0.0)
    scores = jnp.einsum('bclhd,bcshd->bchls', q, k) * dmat
    inner = jnp.einsum('bchls,bcshe->bclhe', scores, v)
    k_decay = jnp.exp((CHUNK - 1.0 - idx)[:, None] * log_gamma)
    chunk_kv = jnp.einsum('bcshd,bcshe->bchde', k * k_decay[:, :, None], v)
    chunk_decay = jnp.exp(CHUNK * log_gamma)

    def step(state, kv):
        return state * chunk_decay[:, None, None] + kv, state

    _, prev = lax.scan(step, jnp.zeros((bsz, H, Dh, Dh), f32), jnp.moveaxis(chunk_kv, 1, 0))
    prev = jnp.moveaxis(prev, 0, 1)
    q_decay = jnp.exp((idx + 1.0)[:, None] * log_gamma)
    cross = jnp.einsum('bclhd,bchde->bclhe', q, prev) * q_decay[:, :, None]
    y = (inner + cross).reshape(bsz, seq, H, Dh)
    mu = jnp.mean(y, axis=-1, keepdims=True)
    var = jnp.mean(jnp.square(y - mu), axis=-1, keepdims=True)
    y = (y - mu) * lax.rsqrt(var + EPS) * gn_w.astype(f32).reshape(H, Dh)
    y = y.reshape(bsz, seq, RET_WIDTH)
    return (jax.nn.silu(g.astype(f32)) * y).astype(g.dtype)


def rglru_mixer(x, gate, conv_w, conv_b, wa, ba, wx, bx, lam):
    bsz, seq, _ = x.shape
    f32 = jnp.float32
    xc = causal_dwconv(x, conv_w, conv_b)
    xb = xc.reshape(bsz, seq, LRU_BLOCKS, LRU_BLOCK)
    r = jax.nn.sigmoid((jnp.einsum('blhi,hij->blhj', xb, wa).reshape(bsz, seq, LRU_WIDTH) + ba).astype(f32))
    i = jax.nn.sigmoid((jnp.einsum('blhi,hij->blhj', xb, wx).reshape(bsz, seq, LRU_WIDTH) + bx).astype(f32))
    log_a = -LRU_C * r * jax.nn.softplus(-lam.astype(f32))
    a = jnp.exp(log_a)
    mult = jnp.sqrt(jnp.maximum(-jnp.expm1(2.0 * log_a), 0.0))
    b = mult * i * xc.astype(f32)
    _, hs = lax.associative_scan(_combine_real, (a, b), axis=1)
    return (hs * jax.nn.gelu(gate.astype(f32))).astype(x.dtype)


def mixing_block(h, cos, sin, norm_w, w_in, b_gate,
                 s5_lam_re, s5_lam_im, s5_b_re, s5_b_im, s5_c_re, s5_c_im, s5_d, s5_log_dt, s5_w_glu,
                 ssd_conv_w, ssd_conv_b, ssd_dt_bias, ssd_a_log, ssd_d, ssd_norm,
                 ret_norm,
                 lru_conv_w, lru_conv_b, lru_wa, lru_ba, lru_wx, lru_bx, lru_lam,
                 w_branch, w_out):
    bsz, seq, _ = h.shape
    hn = rmsnorm(h, norm_w)
    proj = hn @ w_in
    split_points = tuple(int(v) for v in np.cumsum(SECTION_WIDTHS)[:-1])
    (u_s5, z_ssd, xbc_ssd, dt_ssd, q_ret, k_ret, v_ret, g_ret,
     x_lru, gate_lru, gate_logits) = jnp.split(proj, split_points, axis=-1)
    branches = (
        s5_mixer(u_s5, s5_lam_re, s5_lam_im, s5_b_re, s5_b_im, s5_c_re, s5_c_im, s5_d, s5_log_dt, s5_w_glu),
        ssd_mixer(z_ssd, xbc_ssd, dt_ssd, ssd_conv_w, ssd_conv_b, ssd_dt_bias, ssd_a_log, ssd_d, ssd_norm),
        retention_mixer(q_ret, k_ret, v_ret, g_ret, cos, sin, ret_norm),
        rglru_mixer(x_lru, gate_lru, lru_conv_w, lru_conv_b, lru_wa, lru_ba, lru_wx, lru_bx, lru_lam),
    )
    gates = jax.nn.sigmoid((gate_logits + b_gate).astype(jnp.float32)).astype(h.dtype)
    gates = gates.reshape(bsz, seq, N_BRANCH, D_MODEL)
    merged = gates[:, :, 0] * (branches[0] @ w_branch[0])
    for bi in range(1, N_BRANCH):
        merged = merged + gates[:, :, bi] * (branches[bi] @ w_branch[bi])
    return merged @ w_out


def cross_attention(h, mem, norm_w, mem_norm_w, wq, wk, wv, wo):
    bsz, seq, _ = h.shape
    hn = rmsnorm(h, norm_w)
    mn = rmsnorm(mem, mem_norm_w)
    q = (hn @ wq).reshape(bsz, seq, XA_HEADS, XA_HEAD_DIM)
    k = (mn @ wk).reshape(bsz, mem.shape[1], XA_HEADS, XA_HEAD_DIM)
    v = (mn @ wv).reshape(bsz, mem.shape[1], XA_HEADS, XA_HEAD_DIM)
    s = jnp.einsum('blhd,bmhd->bhlm', q, k).astype(jnp.float32) * (XA_HEAD_DIM ** -0.5)
    p = jax.nn.softmax(s, axis=-1).astype(v.dtype)
    o = jnp.einsum('bhlm,bmhd->blhd', p, v).reshape(bsz, seq, D_MODEL)
    return o @ wo


def swiglu(x, w1, w3, w2):
    return (jax.nn.silu(x @ w1) * (x @ w3)) @ w2


def moe_swiglu(x, w_router, w1, w3, w2):
    bsz, seq, _ = x.shape
    t = x.reshape(-1, D_MODEL)
    logits = (t @ w_router).astype(jnp.float32)
    top_v, top_i = lax.top_k(logits, TOP_K)
    top_w = jax.nn.softmax(top_v, axis=-1)
    combine = jnp.einsum('tk,tke->te', top_w, jax.nn.one_hot(top_i, N_EXPERTS, dtype=jnp.float32)).astype(x.dtype)
    out = combine[:, 0:1] * swiglu(t, w1[0], w3[0], w2[0])
    for e in range(1, N_EXPERTS):
        out = out + combine[:, e:e + 1] * swiglu(t, w1[e], w3[e], w2[e])
    return out.reshape(bsz, seq, D_MODEL)


def setup_inputs(seed: int = 0) -> dict:
    key = jax.random.key(seed)
    keys = iter(jax.random.split(key, 64))
    f32 = jnp.float32

    def nrm(shape, scale):
        return jax.random.normal(next(keys), shape, f32) * scale

    def gain(shape):
        return 1.0 + nrm(shape, 0.02)

    x = nrm((BATCH, SEQ, D_MODEL), 1.0)
    mem = nrm((BATCH, N_MEM, D_MODEL), 1.0)
    offsets = jax.random.randint(next(keys), (BATCH, 1), 0, 4096, dtype=jnp.int32)
    positions = (jnp.arange(SEQ, dtype=jnp.int32)[None, :] + offsets).astype(jnp.int32)

    s5_lam_re = -0.5 + nrm((DEPTH, S5_GROUPS, S5_STATE), 0.01)
    s5_lam_im = math.pi * jnp.arange(S5_STATE, dtype=f32)[None, None, :] + nrm((DEPTH, S5_GROUPS, S5_STATE), 0.01)
    s5_log_dt = jax.random.uniform(next(keys), (DEPTH, S5_GROUPS), f32, math.log(S5_DT_MIN), math.log(S5_DT_MAX))
    ssd_dt = jnp.exp(jax.random.uniform(next(keys), (DEPTH, SSD_HEADS), f32, math.log(1e-3), math.log(1e-1)))
    ssd_dt_bias = ssd_dt + jnp.log(-jnp.expm1(-ssd_dt))
    ssd_a_log = jnp.log(jax.random.uniform(next(keys), (DEPTH, SSD_HEADS), f32, 1.0, 16.0))
    lru_u = jax.random.uniform(next(keys), (DEPTH, LRU_WIDTH), f32, 0.9, 0.999)
    lru_a0 = lru_u ** (1.0 / LRU_C)
    lru_lam = jnp.log(lru_a0) - jnp.log1p(-lru_a0)

    return {
        'x': x, 'mem': mem, 'positions': positions,
        'norm_mix': gain((DEPTH, D_MODEL)),
        'w_in': nrm((DEPTH, D_MODEL, IN_TOTAL), D_MODEL ** -0.5),
        'b_gate': nrm((DEPTH, N_BRANCH * D_MODEL), 0.02),
        's5_lam_re': s5_lam_re, 's5_lam_im': s5_lam_im,
        's5_b_re': nrm((DEPTH, S5_GROUPS, S5_STATE, S5_GROUP), (2 * S5_GROUP) ** -0.5),
        's5_b_im': nrm((DEPTH, S5_GROUPS, S5_STATE, S5_GROUP), (2 * S5_GROUP) ** -0.5),
        's5_c_re': nrm((DEPTH, S5_GROUPS, S5_GROUP, S5_STATE), S5_STATE ** -0.5),
        's5_c_im': nrm((DEPTH, S5_GROUPS, S5_GROUP, S5_STATE), S5_STATE ** -0.5),
        's5_d': nrm((DEPTH, S5_GROUPS, S5_GROUP), 1.0),
        's5_log_dt': s5_log_dt,
        's5_w_glu': nrm((DEPTH, S5_WIDTH, S5_WIDTH), S5_WIDTH ** -0.5),
        'ssd_conv_w': nrm((DEPTH, SSD_CONV, SSD_CONV_DIM), SSD_CONV ** -0.5),
        'ssd_conv_b': nrm((DEPTH, SSD_CONV_DIM), 0.02),
        'ssd_dt_bias': ssd_dt_bias, 'ssd_a_log': ssd_a_log,
        'ssd_d': 1.0 + nrm((DEPTH, SSD_HEADS), 0.1),
        'ssd_norm': gain((DEPTH, SSD_WIDTH)),
        'ret_norm': gain((DEPTH, RET_WIDTH)),
        'lru_conv_w': nrm((DEPTH, LRU_CONV, LRU_WIDTH), LRU_CONV ** -0.5),
        'lru_conv_b': nrm((DEPTH, LRU_WIDTH), 0.02),
        'lru_wa': nrm((DEPTH, LRU_BLOCKS, LRU_BLOCK, LRU_BLOCK), LRU_BLOCK ** -0.5),
        'lru_ba': nrm((DEPTH, LRU_WIDTH), 0.02),
        'lru_wx': nrm((DEPTH, LRU_BLOCKS, LRU_BLOCK, LRU_BLOCK), LRU_BLOCK ** -0.5),
        'lru_bx': nrm((DEPTH, LRU_WIDTH), 0.02),
        'lru_lam': lru_lam,
        'w_branch': nrm((DEPTH, N_BRANCH, BRANCH_WIDTH, D_MODEL), BRANCH_WIDTH ** -0.5),
        'w_out': nrm((DEPTH, D_MODEL, D_MODEL), D_MODEL ** -0.5),
        'norm_xa': gain((DEPTH, D_MODEL)),
        'norm_mem': gain((DEPTH, D_MODEL)),
        'xa_wq': nrm((DEPTH, D_MODEL, D_MODEL), D_MODEL ** -0.5),
        'xa_wk': nrm((DEPTH, D_MODEL, D_MODEL), D_MODEL ** -0.5),
        'xa_wv': nrm((DEPTH, D_MODEL, D_MODEL), D_MODEL ** -0.5),
        'xa_wo': nrm((DEPTH, D_MODEL, D_MODEL), D_MODEL ** -0.5),
        'norm_ffn': gain((DEPTH, D_MODEL)),
        'ffn_w1': nrm((N_DENSE, D_MODEL, D_FF), D_MODEL ** -0.5),
        'ffn_w3': nrm((N_DENSE, D_MODEL, D_FF), D_MODEL ** -0.5),
        'ffn_w2': nrm((N_DENSE, D_FF, D_MODEL), D_FF ** -0.5),
        'moe_router': nrm((N_MOE, D_MODEL, N_EXPERTS), D_MODEL ** -0.5),
        'moe_w1': nrm((N_MOE, N_EXPERTS, D_MODEL, D_FF_EXPERT), D_MODEL ** -0.5),
        'moe_w3': nrm((N_MOE, N_EXPERTS, D_MODEL, D_FF_EXPERT), D_MODEL ** -0.5),
        'moe_w2': nrm((N_MOE, N_EXPERTS, D_FF_EXPERT, D_MODEL), D_FF_EXPERT ** -0.5),
        'norm_final': gain((D_MODEL,)),
    }


def reference(x, mem, positions, norm_mix, w_in, b_gate,
              s5_lam_re, s5_lam_im, s5_b_re, s5_b_im, s5_c_re, s5_c_im, s5_d, s5_log_dt, s5_w_glu,
              ssd_conv_w, ssd_conv_b, ssd_dt_bias, ssd_a_log, ssd_d, ssd_norm,
              ret_norm,
              lru_conv_w, lru_conv_b, lru_wa, lru_ba, lru_wx, lru_bx, lru_lam,
              w_branch, w_out,
              norm_xa, norm_mem, xa_wq, xa_wk, xa_wv, xa_wo,
              norm_ffn, ffn_w1, ffn_w3, ffn_w2,
              moe_router, moe_w1, moe_w3, moe_w2,
              norm_final):
    half = RET_HEAD_DIM // 2
    inv_freq = ROPE_BASE ** (-jnp.arange(half, dtype=jnp.float32) / half)
    ang = positions.astype(jnp.float32)[..., None] * inv_freq
    cos = jnp.cos(ang)[:, :, None, :]
    sin = jnp.sin(ang)[:, :, None, :]
    h = x
    for i in range(DEPTH):
        h = h + mixing_block(h, cos, sin, norm_mix[i], w_in[i], b_gate[i],
                             s5_lam_re[i], s5_lam_im[i], s5_b_re[i], s5_b_im[i], s5_c_re[i], s5_c_im[i],
                             s5_d[i], s5_log_dt[i], s5_w_glu[i],
                             ssd_conv_w[i], ssd_conv_b[i], ssd_dt_bias[i], ssd_a_log[i], ssd_d[i], ssd_norm[i],
                             ret_norm[i],
                             lru_conv_w[i], lru_conv_b[i], lru_wa[i], lru_ba[i], lru_wx[i], lru_bx[i], lru_lam[i],
                             w_branch[i], w_out[i])
        h = h + cross_attention(h, mem, norm_xa[i], norm_mem[i], xa_wq[i], xa_wk[i], xa_wv[i], xa_wo[i])
        hn = rmsnorm(h, norm_ffn[i])
        if i % 2 == 0:
            h = h + swiglu(hn, ffn_w1[i // 2], ffn_w3[i // 2], ffn_w2[i // 2])
        else:
            h = h + moe_swiglu(hn, moe_router[i // 2], moe_w1[i // 2], moe_w3[i // 2], moe_w2[i // 2])
    return rmsnorm(h, norm_final)
```

```python
import functools
import math

import numpy as np
import jax
import jax.numpy as jnp
from jax import lax
from jax.experimental import pallas as pl
from jax.experimental.pallas import tpu as pltpu

F32 = jnp.float32
BF16 = jnp.bfloat16

D_MODEL = 1024
N_MEM = 256
EPS = 1e-6
CHUNK = 128
N_BRANCH = 4
S5_WIDTH = 512
S5_GROUP = 16
S5_GROUPS = 32
S5_STATE = 64
S5_SUB = 16
SSD_HEADS = 8
SSD_HEAD_DIM = 64
SSD_WIDTH = 512
SSD_GROUPS = 2
SSD_STATE = 64
SSD_CONV = 4
SSD_CONV_DIM = SSD_WIDTH + 2 * SSD_GROUPS * SSD_STATE
RET_HEADS = 8
RET_HEAD_DIM = 64
RET_WIDTH = 512
ROPE_BASE = 10000.0
LRU_WIDTH = 512
LRU_BLOCKS = 8
LRU_BLOCK = 64
LRU_CONV = 4
LRU_C = 8.0
XA_HEADS = 4
XA_HEAD_DIM = 256
D_FF = 2816
N_EXPERTS = 8
D_FF_EXPERT = 3584
SECTION_WIDTHS = (S5_WIDTH, SSD_WIDTH, SSD_CONV_DIM, SSD_HEADS,
                  RET_WIDTH, RET_WIDTH, RET_WIDTH, RET_WIDTH,
                  LRU_WIDTH, LRU_WIDTH, N_BRANCH * D_MODEL)

LANES = 128
SUBLANES = 8
VMEM_LIMIT = 56 * 1024 * 1024
ROW_TILE = 512
LRU_ROWS = 256


def _params(*sem):
    return pltpu.CompilerParams(dimension_semantics=sem, vmem_limit_bytes=VMEM_LIMIT)


def _const_spec(shape):
    nd = len(shape)
    return pl.BlockSpec(shape, lambda *_: (0,) * nd)


def _rms(x, w):
    return x * lax.rsqrt(jnp.mean(x * x, axis=-1, keepdims=True) + EPS) * w


def _sigmoid(x):
    return 1.0 / (1.0 + jnp.exp(-x))


def _silu(x):
    return x * _sigmoid(x)


def _gelu(x):
    return jax.nn.gelu(x)


def _softplus(x):
    return jnp.maximum(x, 0.0) + jnp.log(1.0 + jnp.exp(-jnp.abs(x)))


def _dot(a, b):
    return jnp.dot(a, b, preferred_element_type=F32)


def _dot_nt(a, b):
    return lax.dot_general(a, b, (((1,), (1,)), ((), ())), preferred_element_type=F32)


def _split3(x):
    hi = x.astype(BF16)
    r1 = x - hi.astype(F32)
    mid = r1.astype(BF16)
    lo = (r1 - mid.astype(F32)).astype(BF16)
    return hi, mid, lo


def _split_dot(x, m_bf16):
    hi, mid, lo = _split3(x)
    return _dot(hi, m_bf16) + _dot(mid, m_bf16) + _dot(lo, m_bf16)


def _split_dot_left(m_bf16, x):
    hi, mid, lo = _split3(x)
    return _dot(m_bf16, hi) + _dot(m_bf16, mid) + _dot(m_bf16, lo)


def _shift_rows(x, k, fill=0.0):
    rows = x.shape[0]
    if k % SUBLANES == 0:
        return jnp.concatenate([jnp.full((k, x.shape[1]), fill, x.dtype), x[:rows - k]], axis=0)
    rolled = pltpu.roll(x, k, 0)
    row = lax.broadcasted_iota(jnp.int32, x.shape, 0)
    return jnp.where(row >= k, rolled, fill)


IN_WIDTHS = (S5_WIDTH, SSD_WIDTH, SSD_CONV_DIM, LANES,
             RET_WIDTH, RET_WIDTH, RET_WIDTH, RET_WIDTH,
             LRU_WIDTH, LRU_WIDTH, N_BRANCH * D_MODEL)
IN_TOTAL_PADDED = sum(IN_WIDTHS)
DOT_COLS = 512


def _inproj_kernel(h_ref, nw_ref, w_ref, *out_refs):
    xb = _rms(h_ref[...], nw_ref[...]).astype(BF16)
    off = 0
    for o_ref, width in zip(out_refs, IN_WIDTHS):
        for c in range(0, width, DOT_COLS):
            n = min(DOT_COLS, width - c)
            o_ref[:, c:c + n] = _dot(xb, w_ref[:, off + c:off + c + n]).astype(o_ref.dtype)
        off += width


def _inproj(h, norm_w, w_cat):
    t = h.shape[0]
    tm = min(ROW_TILE, t)
    out_shape = [jax.ShapeDtypeStruct((t, w), BF16) for w in IN_WIDTHS]
    out_specs = [pl.BlockSpec((tm, w), lambda i: (i, 0)) for w in IN_WIDTHS]
    return pl.pallas_call(
        _inproj_kernel,
        grid=(t // tm,),
        in_specs=[pl.BlockSpec((tm, D_MODEL), lambda i: (i, 0)),
                  _const_spec((1, D_MODEL)),
                  _const_spec((D_MODEL, IN_TOTAL_PADDED))],
        out_specs=out_specs,
        out_shape=out_shape,
        compiler_params=_params("parallel"),
        name="inproj",
    )(h, norm_w.reshape(1, D_MODEL), w_cat)


def _pack_w_in(w_in):
    pieces = []
    off = 0
    for width in SECTION_WIDTHS:
        sec = w_in[:, off:off + width]
        if width == SSD_HEADS:
            sec = jnp.pad(sec, ((0, 0), (0, LANES - width)))
        pieces.append(sec)
        off += width
    return jnp.concatenate(pieces, axis=1).astype(BF16)


S5_PAIRS = S5_GROUPS // 2
S5_ROW = S5_SUB * S5_GROUP
S5_SCAN_STEPS = 7


def _s5_kernel(u_ref, t_ref, win_ref, wx_ref, pw_ref, y_ref):
    rows = u_ref.shape[1]
    n_steps = int(math.log2(rows))

    def pair(p, carry):
        u = u_ref[0, p]
        x = _dot(u, win_ref[p])
        sr = x[:, :LANES]
        si = x[:, LANES:]
        pw = pw_ref[p]
        for j in range(n_steps):
            k = 1 << j
            pr = pw[2 * j:2 * j + 1, :]
            pi = pw[2 * j + 1:2 * j + 2, :]
            shr = _shift_rows(sr, k)
            shi = _shift_rows(si, k)
            sr, si = sr + pr * shr - pi * shi, si + pr * shi + pi * shr
        prev = jnp.concatenate([_shift_rows(sr, 1), _shift_rows(si, 1)], axis=1).astype(BF16)
        y_cross = _dot(prev, wx_ref[p])
        y0 = _dot(u[:, :S5_ROW], t_ref[2 * p])
        y1 = _dot(u[:, S5_ROW:], t_ref[2 * p + 1])
        y_ref[0, p, :, :S5_ROW] = (y0 + y_cross[:, :S5_ROW]).astype(y_ref.dtype)
        y_ref[0, p, :, S5_ROW:] = (y1 + y_cross[:, S5_ROW:]).astype(y_ref.dtype)
        return carry

    lax.fori_loop(0, S5_PAIRS, pair, 0)


def _s5_tables(lam_re, lam_im, b_re, b_im, c_re, c_im, log_dt, rows):
    lr = lam_re.astype(F32)
    li = lam_im.astype(F32)
    step = jnp.exp(log_dt.astype(F32))[:, None]
    mag = jnp.exp(lr * step)
    ar = mag * jnp.cos(li * step)
    ai = mag * jnp.sin(li * step)
    inv = 1.0 / (lr * lr + li * li)
    cr = ((ar - 1.0) * lr + ai * li) * inv
    ci = (ai * lr - (ar - 1.0) * li) * inv
    bbr = cr[..., None] * b_re - ci[..., None] * b_im
    bbi = cr[..., None] * b_im + ci[..., None] * b_re

    def apow(e):
        e = jnp.asarray(e, F32)[:, None, None]
        m = jnp.exp(lr * step * e)
        return m * jnp.cos(li * step * e), m * jnp.sin(li * step * e)

    sub = S5_SUB
    pr, pi = apow(np.arange(sub + 1))
    m_r = pr[:sub, :, :, None] * bbr - pi[:sub, :, :, None] * bbi
    m_i = pr[:sub, :, :, None] * bbi + pi[:sub, :, :, None] * bbr
    kern = (jnp.einsum('gon,tgni->tgoi', c_re, m_r)
            - jnp.einsum('gon,tgni->tgoi', c_im, m_i))
    t_in = np.arange(sub)[:, None]
    t_out = np.arange(sub)[None, :]
    lag = t_out - t_in
    toe = kern[np.clip(lag, 0, sub - 1)]
    toe = jnp.where((lag >= 0)[:, :, None, None, None], toe, 0.0)
    toe = jnp.transpose(toe, (2, 0, 4, 1, 3)).reshape(S5_GROUPS, S5_ROW, S5_ROW)
    er, ei = pr[sub - 1 - np.arange(sub)], pi[sub - 1 - np.arange(sub)]
    w_r = er[..., None] * bbr - ei[..., None] * bbi
    w_i = er[..., None] * bbi + ei[..., None] * bbr
    w_r = jnp.transpose(w_r, (1, 0, 3, 2)).reshape(S5_GROUPS, S5_ROW, S5_STATE)
    w_i = jnp.transpose(w_i, (1, 0, 3, 2)).reshape(S5_GROUPS, S5_ROW, S5_STATE)
    z = jnp.zeros_like(w_r[0])
    win = jnp.stack([
        jnp.concatenate([
            jnp.concatenate([w_r[2 * p], z, w_i[2 * p], z], axis=1),
            jnp.concatenate([z, w_r[2 * p + 1], z, w_i[2 * p + 1]], axis=1)], axis=0)
        for p in range(S5_PAIRS)])
    qr, qi = pr[1:], pi[1:]
    x_r = c_re[None] * qr[:, :, None, :] - c_im[None] * qi[:, :, None, :]
    x_i = -(c_re[None] * qi[:, :, None, :] + c_im[None] * qr[:, :, None, :])
    x_r = jnp.transpose(x_r, (1, 3, 0, 2)).reshape(S5_GROUPS, S5_STATE, S5_ROW)
    x_i = jnp.transpose(x_i, (1, 3, 0, 2)).reshape(S5_GROUPS, S5_STATE, S5_ROW)
    zc = jnp.zeros_like(x_r[0])
    wx = jnp.stack([
        jnp.concatenate([
            jnp.concatenate([x_r[2 * p], zc], axis=1),
            jnp.concatenate([zc, x_r[2 * p + 1]], axis=1),
            jnp.concatenate([x_i[2 * p], zc], axis=1),
            jnp.concatenate([zc, x_i[2 * p + 1]], axis=1)], axis=0)
        for p in range(S5_PAIRS)])
    n_steps = int(math.log2(rows))
    sr_, si_ = apow(sub * (2.0 ** np.arange(n_steps)))
    pw = jnp.stack([sr_, si_], axis=1).reshape(2 * n_steps, S5_PAIRS, 2 * S5_STATE)
    pw = jnp.transpose(pw, (1, 0, 2))
    pw = jnp.pad(pw, ((0, 0), (0, 16 - 2 * n_steps), (0, 0)))
    return toe.astype(BF16), win.astype(BF16), wx.astype(BF16), pw


def _s5_scan(u, tables):
    toe, win, wx, pw = tables
    bsz, seq, _ = u.shape
    rows = seq // S5_SUB
    u2 = u.reshape(bsz, rows, S5_SUB, S5_PAIRS, 2, S5_GROUP)
    u2 = jnp.transpose(u2, (0, 3, 1, 4, 2, 5)).reshape(bsz, S5_PAIRS, rows, 2 * S5_ROW)
    y2 = pl.pallas_call(
        _s5_kernel,
        grid=(bsz,),
        in_specs=[pl.BlockSpec((1, S5_PAIRS, rows, 2 * S5_ROW), lambda b: (b, 0, 0, 0)),
                  _const_spec(toe.shape), _const_spec(win.shape),
                  _const_spec(wx.shape), _const_spec(pw.shape)],
        out_specs=pl.BlockSpec((1, S5_PAIRS, rows, 2 * S5_ROW), lambda b: (b, 0, 0, 0)),
        out_shape=jax.ShapeDtypeStruct((bsz, S5_PAIRS, rows, 2 * S5_ROW), BF16),
        compiler_params=_params("parallel"),
        name="s5_scan",
    )(u2, toe, win, wx, pw)
    y = y2.reshape(bsz, S5_PAIRS, rows, 2, S5_SUB, S5_GROUP)
    return jnp.transpose(y, (0, 2, 4, 1, 3, 5)).reshape(bsz, seq, S5_WIDTH)


CONV_PAD = SUBLANES


def _causal_conv(xpad_ref, x, w, b, first):
    rows = x.shape[0]

    @pl.when(first)
    def _():
        xpad_ref[0:CONV_PAD, :] = jnp.zeros((CONV_PAD, x.shape[1]), F32)

    xpad_ref[CONV_PAD:CONV_PAD + rows, :] = x
    k = w.shape[0]
    acc = b
    for j in range(k):
        s = CONV_PAD - (k - 1) + j
        acc = acc + w[j:j + 1, :] * xpad_ref[s:s + rows, :]
    xpad_ref[0:CONV_PAD, :] = xpad_ref[rows:rows + CONV_PAD, :]
    return acc


def _ssd_kernel(z_ref, xbc_ref, dt_ref, cw_ref, cb_ref, dtb_ref, a_ref, d_ref, nw_ref, tri_ref,
                o_ref, xpad_ref, state_ref, y_ref):
    c = pl.program_id(1)
    first = c == 0

    @pl.when(first)
    def _():
        state_ref[...] = jnp.zeros(state_ref.shape, F32)

    xbc = _causal_conv(xpad_ref, xbc_ref[0].astype(F32), cw_ref[...], cb_ref[...], first)
    xbc = _silu(xbc)
    xs = xbc[:, :SSD_WIDTH]
    bs = xbc[:, SSD_WIDTH:SSD_WIDTH + LANES]
    cs = xbc[:, SSD_WIDTH + LANES:]
    bs_t = bs.T.astype(BF16)
    bs_b = bs.astype(BF16)
    cs_b = cs.astype(BF16)
    dt = _softplus(dt_ref[0].astype(F32) + dtb_ref[...])
    da = dt * a_ref[...]
    acum = _split_dot_left(tri_ref[...], da)
    acum_t = acum.T
    causal = (lax.broadcasted_iota(jnp.int32, (CHUNK, CHUNK), 0)
              >= lax.broadcasted_iota(jnp.int32, (CHUNK, CHUNK), 1))
    for g in range(SSD_GROUPS):
        b_g = bs_b[:, g * SSD_STATE:(g + 1) * SSD_STATE]
        c_g = cs_b[:, g * SSD_STATE:(g + 1) * SSD_STATE]
        b_gt = bs_t[g * SSD_STATE:(g + 1) * SSD_STATE, :]
        cb = _dot_nt(c_g, b_g)
        for r in range(SSD_HEADS // SSD_GROUPS):
            hd = g * (SSD_HEADS // SSD_GROUPS) + r
            col = acum[:, hd:hd + 1]
            row = acum_t[hd:hd + 1, :]
            lmat = jnp.exp(jnp.where(causal, col - row, -jnp.inf))
            x_h = xs[:, hd * SSD_HEAD_DIM:(hd + 1) * SSD_HEAD_DIM]
            xdt = x_h * dt[:, hd:hd + 1]
            y = _dot((cb * lmat).astype(BF16), xdt.astype(BF16))
            last = acum[CHUNK - 1:CHUNK, hd:hd + 1]
            decay_end = jnp.exp(last - col)
            new_t = _dot(b_gt, (xdt * decay_end).astype(BF16))
            prev_t = state_ref[hd]
            y = y + _dot(c_g, prev_t.astype(BF16)) * jnp.exp(col)
            state_ref[hd] = prev_t * jnp.exp(last) + new_t
            y = y + d_ref[:, hd:hd + 1] * x_h
            y_ref[:, hd * SSD_HEAD_DIM:(hd + 1) * SSD_HEAD_DIM] = y
    y = y_ref[...] * _silu(z_ref[0].astype(F32))
    o_ref[0] = _rms(y, nw_ref[...]).astype(o_ref.dtype)


def _ssd(z, xbc, dt, conv_w, conv_b, dt_bias, a_log, d_skip, norm_w):
    bsz, seq, _ = z.shape
    nc = seq // CHUNK

    def lane_pad(v):
        return jnp.pad(v.astype(F32).reshape(1, -1), ((0, 0), (0, LANES - v.shape[-1])))

    tri = jnp.asarray(np.tril(np.ones((CHUNK, CHUNK), np.float32)), BF16)
    blk = lambda w: pl.BlockSpec((1, CHUNK, w), lambda b, c: (b, c, 0))
    return pl.pallas_call(
        _ssd_kernel,
        grid=(bsz, nc),
        in_specs=[blk(SSD_WIDTH), blk(SSD_CONV_DIM), blk(LANES),
                  _const_spec((SSD_CONV, SSD_CONV_DIM)), _const_spec((1, SSD_CONV_DIM)),
                  _const_spec((1, LANES)), _const_spec((1, LANES)), _const_spec((1, LANES)),
                  _const_spec((1, SSD_WIDTH)), _const_spec((CHUNK, CHUNK))],
        out_specs=blk(SSD_WIDTH),
        out_shape=jax.ShapeDtypeStruct((bsz, seq, SSD_WIDTH), BF16),
        scratch_shapes=[pltpu.VMEM((CHUNK + CONV_PAD, SSD_CONV_DIM), F32),
                        pltpu.VMEM((SSD_HEADS, SSD_STATE, SSD_HEAD_DIM), F32),
                        pltpu.VMEM((CHUNK, SSD_WIDTH), F32)],
        compiler_params=_params("parallel", "arbitrary"),
        name="ssd",
    )(z, xbc, dt, conv_w.astype(F32), conv_b.astype(F32).reshape(1, -1),
      lane_pad(dt_bias), lane_pad(-jnp.exp(a_log.astype(F32))), lane_pad(d_skip),
      norm_w.astype(F32).reshape(1, -1), tri)


def _retention_tables():
    h = np.arange(RET_HEADS, dtype=np.float64)
    log_gamma = np.log1p(-np.exp2(-5.0 - h))
    idx = np.arange(CHUNK, dtype=np.float64)
    diff = idx[:, None] - idx[None, :]
    dmat = np.where(diff >= 0, np.exp(np.maximum(diff, 0.0)[None] * log_gamma[:, None, None]), 0.0)
    k_decay = np.exp((CHUNK - 1.0 - idx)[:, None] * log_gamma)
    q_decay = np.exp((idx + 1.0)[:, None] * log_gamma)
    c_decay = np.exp(CHUNK * log_gamma)[None, :]
    rep = lambda a: np.repeat(a, RET_HEAD_DIM, axis=1).astype(np.float32)
    avg = np.kron(np.eye(RET_HEADS), np.full((RET_HEAD_DIM, RET_HEAD_DIM), 1.0 / RET_HEAD_DIM))
    return (dmat.astype(np.float32), rep(k_decay * RET_HEAD_DIM ** -0.5), rep(q_decay),
            rep(c_decay), avg.astype(np.float32))


def _rope(x, cos, sin_signed):
    half = RET_HEAD_DIM // 2
    width = x.shape[1]
    fwd = pltpu.roll(x, half, 1)
    bwd = pltpu.roll(x, width - half, 1)
    lane = lax.broadcasted_iota(jnp.int32, x.shape, 1)
    swapped = jnp.where((lane % RET_HEAD_DIM) < half, bwd, fwd)
    return x * cos + swapped * sin_signed


def _retention_kernel(q_ref, k_ref, v_ref, g_ref, cos_ref, sin_ref, dmat_ref, kdec_ref, qdec_ref,
                      cdec_ref, avg_ref, gn_ref, o_ref, state_ref, y_ref):
    c = pl.program_id(1)

    @pl.when(c == 0)
    def _():
        state_ref[...] = jnp.zeros(state_ref.shape, F32)

    reps = RET_WIDTH // LANES
    cos = jnp.tile(cos_ref[0], (1, reps))
    sin = jnp.tile(sin_ref[0], (1, reps))
    q = _rope(q_ref[0].astype(F32), cos, sin)
    k = _rope(k_ref[0].astype(F32), cos, sin)
    v = v_ref[0]
    qb = q.astype(BF16)
    k_t = k.T
    kd_t = (k * kdec_ref[...]).T.astype(BF16)
    k_tb = (k_t * (RET_HEAD_DIM ** -0.5)).astype(BF16)
    qd = qdec_ref[...]
    cd = cdec_ref[...]
    for hd in range(RET_HEADS):
        sl = slice(hd * RET_HEAD_DIM, (hd + 1) * RET_HEAD_DIM)
        q_h = qb[:, sl]
        v_h = v[:, sl]
        scores = _dot(q_h, k_tb[sl, :]) * dmat_ref[hd]
        y = _dot(scores.astype(BF16), v_h)
        prev = state_ref[hd]
        y = y + _dot(q_h, prev.astype(BF16)) * qd[:, sl]
        state_ref[hd] = prev * cd[:, sl] + _dot(kd_t[sl, :], v_h)
        y_ref[:, sl] = y
    y = y_ref[...]
    avg = avg_ref[...]
    mu = _split_dot(y, avg)
    yc = y - mu
    var = _split_dot(yc * yc, avg)
    yn = yc * lax.rsqrt(var + EPS) * gn_ref[...]
    o_ref[0] = (_silu(g_ref[0].astype(F32)) * yn).astype(o_ref.dtype)


def _retention(q, k, v, g, cos2, sin2, gn_w):
    bsz, seq, _ = q.shape
    nc = seq // CHUNK
    dmat, kdec, qdec, cdec, avg = (jnp.asarray(a) for a in _retention_tables())
    avg = avg.astype(BF16)
    blk = lambda w: pl.BlockSpec((1, CHUNK, w), lambda b, c: (b, c, 0))
    return pl.pallas_call(
        _retention_kernel,
        grid=(bsz, nc),
        in_specs=[blk(RET_WIDTH)] * 4 + [blk(LANES)] * 2 + [
            _const_spec(dmat.shape), _const_spec(kdec.shape), _const_spec(qdec.shape),
            _const_spec(cdec.shape), _const_spec(avg.shape), _const_spec((1, RET_WIDTH))],
        out_specs=blk(RET_WIDTH),
        out_shape=jax.ShapeDtypeStruct((bsz, seq, RET_WIDTH), BF16),
        scratch_shapes=[pltpu.VMEM((RET_HEADS, RET_HEAD_DIM, RET_HEAD_DIM), F32),
                        pltpu.VMEM((CHUNK, RET_WIDTH), F32)],
        compiler_params=_params("parallel", "arbitrary"),
        name="retention",
    )(q, k, v, g, cos2, sin2, dmat, kdec, qdec, cdec, avg, gn_w.astype(F32).reshape(1, -1))


def _rope_tables(positions):
    half = RET_HEAD_DIM // 2
    inv_freq = ROPE_BASE ** (-jnp.arange(half, dtype=F32) / half)
    ang = positions.astype(F32)[..., None] * inv_freq
    cos = jnp.cos(ang)
    sin = jnp.sin(ang)
    cos2 = jnp.concatenate([cos, cos, cos, cos], axis=-1)
    sin2 = jnp.concatenate([-sin, sin, -sin, sin], axis=-1)
    return cos2, sin2


def _lru_kernel(x_ref, gate_ref, cw_ref, cb_ref, w_ref, bias_ref, lamc_ref, o_ref, xpad_ref, h_ref):
    c = pl.program_id(1)
    first = c == 0

    @pl.when(first)
    def _():
        h_ref[...] = jnp.zeros(h_ref.shape, F32)

    rows = x_ref.shape[1]
    xc = _causal_conv(xpad_ref, x_ref[0].astype(F32), cw_ref[...], cb_ref[...], first)
    ri = _sigmoid(_dot(xc.astype(BF16), w_ref[...]) + bias_ref[...])
    log_a = lamc_ref[...] * ri[:, :LRU_WIDTH]
    a_all = jnp.exp(log_a)
    mult = jnp.sqrt(jnp.maximum(1.0 - jnp.exp(2.0 * log_a), 0.0))
    b_all = mult * ri[:, LRU_WIDTH:] * xc
    gate = _gelu(gate_ref[0].astype(F32))
    n_steps = int(math.log2(rows))
    row0 = lax.broadcasted_iota(jnp.int32, (rows, LANES), 0) == 0
    for j in range(LRU_WIDTH // LANES):
        sl = slice(j * LANES, (j + 1) * LANES)
        a = a_all[:, sl]
        b = b_all[:, sl]
        b = b + jnp.where(row0, a * h_ref[0:1, sl], 0.0)
        for s in range(n_steps):
            k = 1 << s
            b = b + a * _shift_rows(b, k)
            if s + 1 < n_steps:
                a = a * _shift_rows(a, k, 1.0)
        h_ref[0:1, sl] = b[rows - 1:rows, :]
        o_ref[0, :, sl] = (b * gate[:, sl]).astype(o_ref.dtype)


def _block_diag(w):
    nb, n, _ = w.shape
    eye = jnp.eye(nb, dtype=w.dtype)
    return jnp.einsum('bij,bc->bicj', w, eye).reshape(nb * n, nb * n)


def _lru(x, gate, conv_w, conv_b, wa, ba, wx, bx, lam):
    bsz, seq, _ = x.shape
    rows = min(LRU_ROWS, seq)
    w = jnp.concatenate([_block_diag(wa), _block_diag(wx)], axis=1).astype(BF16)
    bias = jnp.concatenate([ba, bx]).astype(F32).reshape(1, -1)
    lamc = (-LRU_C * jax.nn.softplus(-lam.astype(F32))).reshape(1, -1)
    blk = lambda wd: pl.BlockSpec((1, rows, wd), lambda b, c: (b, c, 0))
    return pl.pallas_call(
        _lru_kernel,
        grid=(bsz, seq // rows),
        in_specs=[blk(LRU_WIDTH), blk(LRU_WIDTH),
                  _const_spec((LRU_CONV, LRU_WIDTH)), _const_spec((1, LRU_WIDTH)),
                  _const_spec((LRU_WIDTH, 2 * LRU_WIDTH)), _const_spec((1, 2 * LRU_WIDTH)),
                  _const_spec((1, LRU_WIDTH))],
        out_specs=blk(LRU_WIDTH),
        out_shape=jax.ShapeDtypeStruct((bsz, seq, LRU_WIDTH), BF16),
        scratch_shapes=[pltpu.VMEM((rows + CONV_PAD, LRU_WIDTH), F32),
                        pltpu.VMEM((SUBLANES, LRU_WIDTH), F32)],
        compiler_params=_params("parallel", "arbitrary"),
        name="rglru",
    )(x, gate, conv_w.astype(F32), conv_b.astype(F32).reshape(1, -1), w, bias, lamc)


def _merge_kernel(h_ref, u_ref, ys5_ref, b1_ref, b2_ref, b3_ref, gl_ref, bg_ref, d_ref, wglu_ref,
                  wb_ref, wo_ref, o_ref):
    y = ys5_ref[...].astype(F32) + d_ref[...] * u_ref[...].astype(F32)
    y = _gelu(y)
    b0 = (y * _sigmoid(_dot(y.astype(BF16), wglu_ref[...]))).astype(BF16)
    branches = (b0, b1_ref[...], b2_ref[...], b3_ref[...])
    merged = None
    for i, br in enumerate(branches):
        sl = slice(i * D_MODEL, (i + 1) * D_MODEL)
        gate = _sigmoid(gl_ref[:, sl].astype(F32) + bg_ref[:, sl])
        term = gate * _dot(br, wb_ref[i])
        merged = term if merged is None else merged + term
    o_ref[...] = h_ref[...] + _dot(merged.astype(BF16), wo_ref[...])


def _merge(h, u, ys5, b1, b2, b3, gate_logits, b_gate, s5_d, w_glu, w_branch, w_out):
    t = h.shape[0]
    tm = min(ROW_TILE, t)
    row = lambda w: pl.BlockSpec((tm, w), lambda i: (i, 0))
    return pl.pallas_call(
        _merge_kernel,
        grid=(t // tm,),
        in_specs=[row(D_MODEL)] + [row(512)] * 5 + [row(N_BRANCH * D_MODEL),
                  _const_spec((1, N_BRANCH * D_MODEL)), _const_spec((1, S5_WIDTH)),
                  _const_spec((S5_WIDTH, S5_WIDTH)),
                  _const_spec((N_BRANCH, 512, D_MODEL)), _const_spec((D_MODEL, D_MODEL))],
        out_specs=row(D_MODEL),
        out_shape=jax.ShapeDtypeStruct((t, D_MODEL), F32),
        compiler_params=_params("parallel"),
        name="merge",
    )(h, u, ys5, b1, b2, b3, gate_logits, b_gate.astype(F32).reshape(1, -1),
      s5_d.astype(F32).reshape(1, -1), w_glu.astype(BF16), w_branch.astype(BF16),
      w_out.astype(BF16))


def _kv_kernel(mem_ref, nw_ref, w_ref, k_ref, v_ref):
    mb = _rms(mem_ref[0], nw_ref[...]).astype(BF16)
    for c in range(0, D_MODEL, DOT_COLS):
        k_ref[0, :, c:c + DOT_COLS] = _dot(mb, w_ref[:, c:c + DOT_COLS]).astype(BF16)
        v_ref[0, :, c:c + DOT_COLS] = _dot(
            mb, w_ref[:, D_MODEL + c:D_MODEL + c + DOT_COLS]).astype(BF16)


def _kv(mem, norm_w, wk, wv):
    bsz, n_mem, _ = mem.shape
    w = jnp.concatenate([wk, wv], axis=1).astype(BF16)
    blk = pl.BlockSpec((1, n_mem, D_MODEL), lambda b: (b, 0, 0))
    return pl.pallas_call(
        _kv_kernel,
        grid=(bsz,),
        in_specs=[blk, _const_spec((1, D_MODEL)), _const_spec((D_MODEL, 2 * D_MODEL))],
        out_specs=[blk, blk],
        out_shape=[jax.ShapeDtypeStruct((bsz, n_mem, D_MODEL), BF16)] * 2,
        compiler_params=_params("parallel"),
        name="xattn_kv",
    )(mem, norm_w.astype(F32).reshape(1, -1), w)


def _xattn_kernel(h_ref, nw_ref, wq_ref, k_ref, v_ref, wo_ref, o_ref, att_ref):
    h = h_ref[0]
    xb = _rms(h, nw_ref[...]).astype(BF16)
    scale = XA_HEAD_DIM ** -0.5
    for hd in range(XA_HEADS):
        sl = slice(hd * XA_HEAD_DIM, (hd + 1) * XA_HEAD_DIM)
        q = _dot(xb, wq_ref[:, sl]).astype(BF16)
        s = _dot_nt(q, k_ref[0, :, sl]) * scale
        p = jnp.exp(s - jnp.max(s, axis=-1, keepdims=True))
        denom = jnp.sum(p, axis=-1, keepdims=True)
        att_ref[:, sl] = (_dot(p.astype(BF16), v_ref[0, :, sl]) / denom).astype(BF16)
    o_ref[0] = h + _dot(att_ref[...], wo_ref[...])


def _xattn(h3, norm_w, wq, k, v, wo):
    bsz, seq, _ = h3.shape
    tq = min(ROW_TILE, seq)
    n_mem = k.shape[1]
    row = pl.BlockSpec((1, tq, D_MODEL), lambda b, i: (b, i, 0))
    kv = pl.BlockSpec((1, n_mem, D_MODEL), lambda b, i: (b, 0, 0))
    return pl.pallas_call(
        _xattn_kernel,
        grid=(bsz, seq // tq),
        in_specs=[row, _const_spec((1, D_MODEL)), _const_spec((D_MODEL, D_MODEL)), kv, kv,
                  _const_spec((D_MODEL, D_MODEL))],
        out_specs=row,
        out_shape=jax.ShapeDtypeStruct((bsz, seq, D_MODEL), F32),
        scratch_shapes=[pltpu.VMEM((tq, D_MODEL), BF16)],
        compiler_params=_params("parallel", "parallel"),
        name="xattn",
    )(h3, norm_w.astype(F32).reshape(1, -1), wq.astype(BF16), k, v, wo.astype(BF16))


FF_COLS = 256


def _swiglu_acc(xb, w1_ref, w3_ref, w2_ref, width, lead=()):
    acc = None
    for c in range(0, width, FF_COLS):
        a = _dot(xb, w1_ref[lead + (slice(None), slice(c, c + FF_COLS))])
        b = _dot(xb, w3_ref[lead + (slice(None), slice(c, c + FF_COLS))])
        g = (_silu(a) * b).astype(BF16)
        term = _dot(g, w2_ref[lead + (slice(c, c + FF_COLS), slice(None))])
        acc = term if acc is None else acc + term
    return acc


def _ffn_kernel(h_ref, nw_ref, w1_ref, w3_ref, w2_ref, o_ref):
    h = h_ref[...]
    xb = _rms(h, nw_ref[...]).astype(BF16)
    o_ref[...] = h + _swiglu_acc(xb, w1_ref, w3_ref, w2_ref, D_FF)


def _ffn(h, norm_w, w1, w3, w2):
    t = h.shape[0]
    tm = min(ROW_TILE, t)
    row = pl.BlockSpec((tm, D_MODEL), lambda i: (i, 0))
    return pl.pallas_call(
        _ffn_kernel,
        grid=(t // tm,),
        in_specs=[row, _const_spec((1, D_MODEL)), _const_spec((D_MODEL, D_FF)),
                  _const_spec((D_MODEL, D_FF)), _const_spec((D_FF, D_MODEL))],
        out_specs=row,
        out_shape=jax.ShapeDtypeStruct((t, D_MODEL), F32),
        compiler_params=_params("parallel"),
        name="ffn",
    )(h, norm_w.astype(F32).reshape(1, -1), w1.astype(BF16), w3.astype(BF16), w2.astype(BF16))


MOE_FF_BLOCK = D_FF_EXPERT // 2


def _route(xn, wr):
    logits = lax.dot_general(xn, wr, (((1,), (0,)), ((), ())), precision=lax.Precision.HIGHEST,
                             preferred_element_type=F32)
    lane = lax.broadcasted_iota(jnp.int32, logits.shape, 1)
    logits = jnp.where(lane < N_EXPERTS, logits, -jnp.inf)
    m1 = jnp.max(logits, axis=-1, keepdims=True)
    i1 = jnp.min(jnp.where(logits == m1, lane, LANES), axis=-1, keepdims=True)
    rest = jnp.where(lane == i1, -jnp.inf, logits)
    m2 = jnp.max(rest, axis=-1, keepdims=True)
    i2 = jnp.min(jnp.where(rest == m2, lane, LANES), axis=-1, keepdims=True)
    e2 = jnp.exp(m2 - m1)
    w1 = 1.0 / (1.0 + e2)
    w2 = e2 / (1.0 + e2)
    return jnp.where(lane == i1, w1, 0.0) + jnp.where(lane == i2, w2, 0.0)


def _moe_dense_kernel(h_ref, nw_ref, wr_ref, w1_ref, w3_ref, w2_ref, fw_ref, o_ref,
                      xb_ref, comb_ref, acc_ref, *, final_norm):
    e = pl.program_id(1)
    f = pl.program_id(2)

    @pl.when((e == 0) & (f == 0))
    def _():
        xn = _rms(h_ref[...], nw_ref[...])
        xb_ref[...] = xn.astype(BF16)
        comb_ref[...] = _route(xn, wr_ref[...])
        acc_ref[...] = jnp.zeros(acc_ref.shape, F32)

    lane = lax.broadcasted_iota(jnp.int32, comb_ref.shape, 1)
    w_e = jnp.sum(jnp.where(lane == e, comb_ref[...], 0.0), axis=-1, keepdims=True)
    acc_ref[...] += w_e * _swiglu_acc(xb_ref[...], w1_ref, w3_ref, w2_ref, MOE_FF_BLOCK, lead=(0,))

    @pl.when((e == pl.num_programs(1) - 1) & (f == pl.num_programs(2) - 1))
    def _():
        out = h_ref[...] + acc_ref[...]
        if final_norm:
            out = _rms(out, fw_ref[...])
        o_ref[...] = out


def _moe_dense(h, norm_w, w_router, w1, w3, w2, final_w, final_norm):
    t = h.shape[0]
    tm = min(ROW_TILE, t)
    nf = D_FF_EXPERT // MOE_FF_BLOCK
    row = pl.BlockSpec((tm, D_MODEL), lambda i, e, f: (i, 0))
    wr = jnp.pad(w_router.astype(F32), ((0, 0), (0, LANES - N_EXPERTS)))
    const = lambda shape: pl.BlockSpec(shape, lambda i, e, f: (0,) * len(shape))
    return pl.pallas_call(
        functools.partial(_moe_dense_kernel, final_norm=final_norm),
        grid=(t // tm, N_EXPERTS, nf),
        in_specs=[row, const((1, D_MODEL)), const((D_MODEL, LANES)),
                  pl.BlockSpec((1, D_MODEL, MOE_FF_BLOCK), lambda i, e, f: (e, 0, f)),
                  pl.BlockSpec((1, D_MODEL, MOE_FF_BLOCK), lambda i, e, f: (e, 0, f)),
                  pl.BlockSpec((1, MOE_FF_BLOCK, D_MODEL), lambda i, e, f: (e, f, 0)),
                  const((1, D_MODEL))],
        out_specs=row,
        out_shape=jax.ShapeDtypeStruct((t, D_MODEL), F32),
        scratch_shapes=[pltpu.VMEM((tm, D_MODEL), BF16), pltpu.VMEM((tm, LANES), F32),
                        pltpu.VMEM((tm, D_MODEL), F32)],
        compiler_params=_params("parallel", "arbitrary", "arbitrary"),
        name="moe_dense",
    )(h, norm_w.astype(F32).reshape(1, -1), wr, w1.astype(BF16), w3.astype(BF16),
      w2.astype(BF16), final_w.astype(F32).reshape(1, -1))


def _final_norm_kernel(h_ref, w_ref, o_ref):
    o_ref[...] = _rms(h_ref[...], w_ref[...])


def _final_norm(h, w):
    t = h.shape[0]
    tm = min(ROW_TILE, t)
    row = pl.BlockSpec((tm, D_MODEL), lambda i: (i, 0))
    return pl.pallas_call(
        _final_norm_kernel, grid=(t // tm,), in_specs=[row, _const_spec((1, D_MODEL))],
        out_specs=row, out_shape=jax.ShapeDtypeStruct((t, D_MODEL), F32),
        compiler_params=_params("parallel"), name="final_norm",
    )(h, w.astype(F32).reshape(1, -1))


def _mixing_block(h, bsz, seq, cos2, sin2, norm_w, w_in, b_gate,
                  s5_lam_re, s5_lam_im, s5_b_re, s5_b_im, s5_c_re, s5_c_im, s5_d, s5_log_dt, s5_w_glu,
                  ssd_conv_w, ssd_conv_b, ssd_dt_bias, ssd_a_log, ssd_d, ssd_norm,
                  ret_norm,
                  lru_conv_w, lru_conv_b, lru_wa, lru_ba, lru_wx, lru_bx, lru_lam,
                  w_branch, w_out):
    (u_s5, z_ssd, xbc_ssd, dt_ssd, q_ret, k_ret, v_ret, g_ret, x_lru, gate_lru,
     gate_logits) = _inproj(h, norm_w, _pack_w_in(w_in))
    seq3 = lambda a: a.reshape(bsz, seq, a.shape[-1])
    tables = _s5_tables(s5_lam_re, s5_lam_im, s5_b_re, s5_b_im, s5_c_re, s5_c_im, s5_log_dt,
                        seq // S5_SUB)
    y_s5 = _s5_scan(seq3(u_s5), tables)
    y_ssd = _ssd(seq3(z_ssd), seq3(xbc_ssd), seq3(dt_ssd), ssd_conv_w, ssd_conv_b, ssd_dt_bias,
                 ssd_a_log, ssd_d, ssd_norm)
    y_ret = _retention(seq3(q_ret), seq3(k_ret), seq3(v_ret), seq3(g_ret), cos2, sin2, ret_norm)
    y_lru = _lru(seq3(x_lru), seq3(gate_lru), lru_conv_w, lru_conv_b, lru_wa, lru_ba, lru_wx,
                 lru_bx, lru_lam)
    flat = lambda a: a.reshape(bsz * seq, a.shape[-1])
    return _merge(h, u_s5, flat(y_s5), flat(y_ssd), flat(y_ret), flat(y_lru), gate_logits,
                  b_gate, s5_d, s5_w_glu, w_branch, w_out)


def kernel(x, mem, positions, norm_mix, w_in, b_gate, s5_lam_re, s5_lam_im, s5_b_re, s5_b_im, s5_c_re, s5_c_im, s5_d, s5_log_dt, s5_w_glu, ssd_conv_w, ssd_conv_b, ssd_dt_bias, ssd_a_log, ssd_d, ssd_norm, ret_norm, lru_conv_w, lru_conv_b, lru_wa, lru_ba, lru_wx, lru_bx, lru_lam, w_branch, w_out, norm_xa, norm_mem, xa_wq, xa_wk, xa_wv, xa_wo, norm_ffn, ffn_w1, ffn_w3, ffn_w2, moe_router, moe_w1, moe_w3, moe_w2, norm_final):
    bsz, seq, _ = x.shape
    depth = norm_mix.shape[0]
    cos2, sin2 = _rope_tables(positions)
    h = x.reshape(bsz * seq, D_MODEL)
    for i in range(depth):
        h = _mixing_block(h, bsz, seq, cos2, sin2, norm_mix[i], w_in[i], b_gate[i],
                          s5_lam_re[i], s5_lam_im[i], s5_b_re[i], s5_b_im[i], s5_c_re[i], s5_c_im[i],
                          s5_d[i], s5_log_dt[i], s5_w_glu[i],
                          ssd_conv_w[i], ssd_conv_b[i], ssd_dt_bias[i], ssd_a_log[i], ssd_d[i],
                          ssd_norm[i], ret_norm[i],
                          lru_conv_w[i], lru_conv_b[i], lru_wa[i], lru_ba[i], lru_wx[i], lru_bx[i],
                          lru_lam[i], w_branch[i], w_out[i])
        k, v = _kv(mem, norm_mem[i], xa_wk[i], xa_wv[i])
        h = _xattn(h.reshape(bsz, seq, D_MODEL), norm_xa[i], xa_wq[i], k, v, xa_wo[i])
        h = h.reshape(bsz * seq, D_MODEL)
        last = i == depth - 1
        if i % 2 == 0:
            h = _ffn(h, norm_ffn[i], ffn_w1[i // 2], ffn_w3[i // 2], ffn_w2[i // 2])
            if last:
                h = _final_norm(h, norm_final)
        else:
            h = _moe_dense(h, norm_ffn[i], moe_router[i // 2], moe_w1[i // 2], moe_w3[i // 2],
                           moe_w2[i // 2], norm_final, last)
    return h.reshape(bsz, seq, D_MODEL)
```

```python
import functools
import math

import numpy as np
import jax
import jax.numpy as jnp
from jax import lax
from jax.experimental import pallas as pl
from jax.experimental.pallas import tpu as pltpu

F32 = jnp.float32
BF16 = jnp.bfloat16

D_MODEL = 1024
N_MEM = 256
EPS = 1e-6
CHUNK = 128
N_BRANCH = 4
S5_WIDTH = 512
S5_GROUP = 16
S5_GROUPS = 32
S5_STATE = 64
S5_SUB = 16
SSD_HEADS = 8
SSD_HEAD_DIM = 64
SSD_WIDTH = 512
SSD_GROUPS = 2
SSD_STATE = 64
SSD_CONV = 4
SSD_CONV_DIM = SSD_WIDTH + 2 * SSD_GROUPS * SSD_STATE
RET_HEADS = 8
RET_HEAD_DIM = 64
RET_WIDTH = 512
ROPE_BASE = 10000.0
LRU_WIDTH = 512
LRU_BLOCKS = 8
LRU_BLOCK = 64
LRU_CONV = 4
LRU_C = 8.0
XA_HEADS = 4
XA_HEAD_DIM = 256
D_FF = 2816
N_EXPERTS = 8
D_FF_EXPERT = 3584
SECTION_WIDTHS = (S5_WIDTH, SSD_WIDTH, SSD_CONV_DIM, SSD_HEADS,
                  RET_WIDTH, RET_WIDTH, RET_WIDTH, RET_WIDTH,
                  LRU_WIDTH, LRU_WIDTH, N_BRANCH * D_MODEL)

LANES = 128
SUBLANES = 8
VMEM_LIMIT = 56 * 1024 * 1024
ROW_TILE = 512
LRU_ROWS = 256


def _params(*sem):
    return pltpu.CompilerParams(dimension_semantics=sem, vmem_limit_bytes=VMEM_LIMIT)


def _const_spec(shape):
    nd = len(shape)
    return pl.BlockSpec(shape, lambda *_: (0,) * nd)


def _rms(x, w):
    return x * lax.rsqrt(jnp.mean(x * x, axis=-1, keepdims=True) + EPS) * w


def _sigmoid(x):
    return 1.0 / (1.0 + jnp.exp(-x))


def _silu(x):
    return x * _sigmoid(x)


def _gelu(x):
    return jax.nn.gelu(x)


def _softplus(x):
    return jnp.maximum(x, 0.0) + jnp.log(1.0 + jnp.exp(-jnp.abs(x)))


def _dot(a, b):
    return jnp.dot(a, b, preferred_element_type=F32)


def _dot_nt(a, b):
    return lax.dot_general(a, b, (((1,), (1,)), ((), ())), preferred_element_type=F32)


def _split3(x):
    hi = x.astype(BF16)
    r1 = x - hi.astype(F32)
    mid = r1.astype(BF16)
    lo = (r1 - mid.astype(F32)).astype(BF16)
    return hi, mid, lo


def _split_dot(x, m_bf16):
    hi, mid, lo = _split3(x)
    return _dot(hi, m_bf16) + _dot(mid, m_bf16) + _dot(lo, m_bf16)


def _split_dot_left(m_bf16, x):
    hi, mid, lo = _split3(x)
    return _dot(m_bf16, hi) + _dot(m_bf16, mid) + _dot(m_bf16, lo)


def _shift_rows(x, k, fill=0.0):
    rows = x.shape[0]
    if k % SUBLANES == 0:
        return jnp.concatenate([jnp.full((k, x.shape[1]), fill, x.dtype), x[:rows - k]], axis=0)
    rolled = pltpu.roll(x, k, 0)
    row = lax.broadcasted_iota(jnp.int32, x.shape, 0)
    return jnp.where(row >= k, rolled, fill)


IN_WIDTHS = (S5_WIDTH, SSD_WIDTH, SSD_CONV_DIM, LANES,
             RET_WIDTH, RET_WIDTH, RET_WIDTH, RET_WIDTH,
             LRU_WIDTH, LRU_WIDTH, N_BRANCH * D_MODEL)
IN_TOTAL_PADDED = sum(IN_WIDTHS)
DOT_COLS = 512


def _inproj_kernel(h_ref, nw_ref, w_ref, u_ref, *out_refs):
    xb = _rms(h_ref[...], nw_ref[...]).astype(BF16)
    u = _dot(xb, w_ref[:, :S5_WIDTH]).astype(u_ref.dtype)
    for q in range(S5_Q):
        u_ref[q] = u[:, q * LANES:(q + 1) * LANES]
    off = S5_WIDTH
    for o_ref, width in zip(out_refs, IN_WIDTHS[1:]):
        for c in range(0, width, DOT_COLS):
            n = min(DOT_COLS, width - c)
            o_ref[:, c:c + n] = _dot(xb, w_ref[:, off + c:off + c + n]).astype(o_ref.dtype)
        off += width


def _inproj(h, norm_w, w_cat):
    t = h.shape[0]
    tm = min(ROW_TILE, t)
    out_shape = ([jax.ShapeDtypeStruct((S5_Q, t, LANES), BF16)]
                 + [jax.ShapeDtypeStruct((t, w), BF16) for w in IN_WIDTHS[1:]])
    out_specs = ([pl.BlockSpec((S5_Q, tm, LANES), lambda i: (0, i, 0))]
                 + [pl.BlockSpec((tm, w), lambda i: (i, 0)) for w in IN_WIDTHS[1:]])
    return pl.pallas_call(
        _inproj_kernel,
        grid=(t // tm,),
        in_specs=[pl.BlockSpec((tm, D_MODEL), lambda i: (i, 0)),
                  _const_spec((1, D_MODEL)),
                  _const_spec((D_MODEL, IN_TOTAL_PADDED))],
        out_specs=out_specs,
        out_shape=out_shape,
        compiler_params=_params("parallel"),
        name="inproj",
    )(h, norm_w.reshape(1, D_MODEL), w_cat)


def _pack_w_in(w_in):
    pieces = []
    off = 0
    for width in SECTION_WIDTHS:
        sec = w_in[:, off:off + width]
        if width == SSD_HEADS:
            sec = jnp.pad(sec, ((0, 0), (0, LANES - width)))
        pieces.append(sec)
        off += width
    return jnp.concatenate(pieces, axis=1).astype(BF16)


S5_Q = S5_WIDTH // LANES
S5_QG = S5_GROUPS // S5_Q
S5_QS = S5_QG * S5_STATE
S5_ROW = S5_SUB * LANES
S5_TILE = 256
S5_SEQS = 4


def _s5_kernel(u_ref, toe_ref, win_ref, wx_ref, pw_ref, y_ref, x_ref, sp_ref, *, rows):
    n_steps = int(math.log2(rows))
    for c in range(0, 2 * S5_QS, DOT_COLS):
        x_ref[:, c:c + DOT_COLS] = _dot(u_ref[0], win_ref[0, :, c:c + DOT_COLS])

    def seq(b, carry):
        r0 = pl.multiple_of(b * rows, rows)
        for lc in range(S5_QS // LANES):
            re = slice(lc * LANES, (lc + 1) * LANES)
            im = slice(S5_QS + lc * LANES, S5_QS + (lc + 1) * LANES)
            sr = x_ref[pl.ds(r0, rows), re]
            si = x_ref[pl.ds(r0, rows), im]
            for j in range(n_steps):
                k = 1 << j
                pr = pw_ref[0, 2 * j:2 * j + 1, re]
                pi = pw_ref[0, 2 * j + 1:2 * j + 2, re]
                shr = _shift_rows(sr, k)
                shi = _shift_rows(si, k)
                sr, si = sr + pr * shr - pi * shi, si + pr * shi + pi * shr
            sp_ref[pl.ds(r0, rows), re] = _shift_rows(sr, 1).astype(BF16)
            sp_ref[pl.ds(r0, rows), im] = _shift_rows(si, 1).astype(BF16)
        return carry

    lax.fori_loop(0, u_ref.shape[1] // rows, seq, 0)
    for nn in range(S5_ROW // S5_TILE):
        cols = slice(nn * S5_TILE, (nn + 1) * S5_TILE)
        acc = _dot(sp_ref[...], wx_ref[0, :, cols])
        for kk in range(nn + 1):
            acc = acc + _dot(u_ref[0, :, kk * S5_TILE:(kk + 1) * S5_TILE], toe_ref[0, nn - kk])
        y_ref[0, :, cols] = acc.astype(y_ref.dtype)


def _s5_tables(lam_re, lam_im, b_re, b_im, c_re, c_im, log_dt, rows):
    lr = lam_re.astype(F32)
    li = lam_im.astype(F32)
    step = jnp.exp(log_dt.astype(F32))[:, None]
    mag = jnp.exp(lr * step)
    ar = mag * jnp.cos(li * step)
    ai = mag * jnp.sin(li * step)
    inv = 1.0 / (lr * lr + li * li)
    cr = ((ar - 1.0) * lr + ai * li) * inv
    ci = (ai * lr - (ar - 1.0) * li) * inv
    bbr = cr[..., None] * b_re - ci[..., None] * b_im
    bbi = cr[..., None] * b_im + ci[..., None] * b_re

    def apow(e):
        e = jnp.asarray(e, F32)[:, None, None]
        m = jnp.exp(lr * step * e)
        return m * jnp.cos(li * step * e), m * jnp.sin(li * step * e)

    sub = S5_SUB
    pr, pi = apow(np.arange(sub + 1))
    m_r = pr[:sub, :, :, None] * bbr - pi[:sub, :, :, None] * bbi
    m_i = pr[:sub, :, :, None] * bbi + pi[:sub, :, :, None] * bbr
    kern = (jnp.einsum('gon,tgni->tgoi', c_re, m_r)
            - jnp.einsum('gon,tgni->tgoi', c_im, m_i))
    eye = jnp.eye(S5_QG, dtype=F32)
    blk = lambda a: a.reshape(a.shape[0], S5_Q, S5_QG, *a.shape[2:])
    kern_bd = jnp.einsum('tqgoi,gh->tqgiho', blk(kern), eye).reshape(sub, S5_Q, LANES, LANES)
    kern_bd = jnp.concatenate([jnp.zeros_like(kern_bd[:1]), kern_bd], axis=0)
    steps_per_tile = S5_TILE // LANES
    d = np.arange(sub // steps_per_tile)[:, None, None]
    tau = steps_per_tile * d + np.arange(steps_per_tile)[None, None, :] - np.arange(steps_per_tile)[None, :, None]
    toe = kern_bd[tau + 1]
    toe = jnp.transpose(toe, (3, 0, 1, 4, 2, 5)).reshape(S5_Q, sub // steps_per_tile, S5_TILE, S5_TILE)
    er, ei = pr[sub - 1 - np.arange(sub)], pi[sub - 1 - np.arange(sub)]
    w_r = er[..., None] * bbr - ei[..., None] * bbi
    w_i = er[..., None] * bbi + ei[..., None] * bbr
    fold_in = lambda w: jnp.einsum('tqgni,gh->qtgihn', blk(w), eye).reshape(S5_Q, S5_ROW, S5_QS)
    win = jnp.concatenate([fold_in(w_r), fold_in(w_i)], axis=2)
    qr, qi = pr[1:], pi[1:]
    x_r = c_re[None] * qr[:, :, None, :] - c_im[None] * qi[:, :, None, :]
    x_i = -(c_re[None] * qi[:, :, None, :] + c_im[None] * qr[:, :, None, :])
    fold_out = lambda w: jnp.einsum('tqgon,gh->qgntho', blk(w), eye).reshape(S5_Q, S5_QS, S5_ROW)
    wx = jnp.concatenate([fold_out(x_r), fold_out(x_i)], axis=1)
    n_steps = int(math.log2(rows))
    sr_, si_ = apow(sub * (2.0 ** np.arange(n_steps)))
    pw = jnp.stack([sr_, si_], axis=1).reshape(2 * n_steps, S5_Q, S5_QS)
    pw = jnp.transpose(pw, (1, 0, 2))
    pw = jnp.pad(pw, ((0, 0), (0, 2 * SUBLANES - 2 * n_steps), (0, 0)))
    return toe.astype(BF16), win.astype(BF16), wx.astype(BF16), pw


def _s5_scan(u4, tables, bsz, seq):
    toe, win, wx, pw = tables
    rows = seq // S5_SUB
    nb = math.gcd(S5_SEQS, bsz)
    u5 = u4.reshape(S5_Q, bsz * rows, S5_ROW)
    row_blk = pl.BlockSpec((1, nb * rows, S5_ROW), lambda q, j: (q, j, 0))
    per_q = lambda a: pl.BlockSpec((1,) + a.shape[1:], lambda q, j: (q,) + (0,) * (a.ndim - 1))
    y5 = pl.pallas_call(
        functools.partial(_s5_kernel, rows=rows),
        grid=(S5_Q, bsz // nb),
        in_specs=[row_blk, per_q(toe), per_q(win), per_q(wx), per_q(pw)],
        out_specs=row_blk,
        out_shape=jax.ShapeDtypeStruct(u5.shape, BF16),
        scratch_shapes=[pltpu.VMEM((nb * rows, 2 * S5_QS), F32),
                        pltpu.VMEM((nb * rows, 2 * S5_QS), BF16)],
        compiler_params=_params("parallel", "parallel"),
        name="s5_scan",
    )(u5, toe, win, wx, pw)
    return y5.reshape(u4.shape)


CONV_PAD = SUBLANES


def _causal_conv(xpad_ref, x, w, b, first):
    rows = x.shape[0]

    @pl.when(first)
    def _():
        xpad_ref[0:CONV_PAD, :] = jnp.zeros((CONV_PAD, x.shape[1]), F32)

    xpad_ref[CONV_PAD:CONV_PAD + rows, :] = x
    k = w.shape[0]
    acc = b
    for j in range(k):
        s = CONV_PAD - (k - 1) + j
        acc = acc + w[j:j + 1, :] * xpad_ref[s:s + rows, :]
    xpad_ref[0:CONV_PAD, :] = xpad_ref[rows:rows + CONV_PAD, :]
    return acc


def _ssd_kernel(z_ref, xbc_ref, dt_ref, cw_ref, cb_ref, dtb_ref, a_ref, d_ref, nw_ref, tri_ref,
                o_ref, xpad_ref, state_ref, y_ref):
    c = pl.program_id(1)
    first = c == 0

    @pl.when(first)
    def _():
        state_ref[...] = jnp.zeros(state_ref.shape, F32)

    xbc = _causal_conv(xpad_ref, xbc_ref[0].astype(F32), cw_ref[...], cb_ref[...], first)
    xbc = _silu(xbc)
    xs = xbc[:, :SSD_WIDTH]
    bs = xbc[:, SSD_WIDTH:SSD_WIDTH + LANES]
    cs = xbc[:, SSD_WIDTH + LANES:]
    bs_t = bs.T.astype(BF16)
    bs_b = bs.astype(BF16)
    cs_b = cs.astype(BF16)
    dt = _softplus(dt_ref[0].astype(F32) + dtb_ref[...])
    da = dt * a_ref[...]
    acum = _split_dot_left(tri_ref[...], da)
    acum_t = acum.T
    causal = (lax.broadcasted_iota(jnp.int32, (CHUNK, CHUNK), 0)
              >= lax.broadcasted_iota(jnp.int32, (CHUNK, CHUNK), 1))
    for g in range(SSD_GROUPS):
        b_g = bs_b[:, g * SSD_STATE:(g + 1) * SSD_STATE]
        c_g = cs_b[:, g * SSD_STATE:(g + 1) * SSD_STATE]
        b_gt = bs_t[g * SSD_STATE:(g + 1) * SSD_STATE, :]
        cb = _dot_nt(c_g, b_g)
        for r in range(SSD_HEADS // SSD_GROUPS):
            hd = g * (SSD_HEADS // SSD_GROUPS) + r
            col = acum[:, hd:hd + 1]
            row = acum_t[hd:hd + 1, :]
            lmat = jnp.exp(jnp.where(causal, col - row, -jnp.inf))
            x_h = xs[:, hd * SSD_HEAD_DIM:(hd + 1) * SSD_HEAD_DIM]
            xdt = x_h * dt[:, hd:hd + 1]
            y = _dot((cb * lmat).astype(BF16), xdt.astype(BF16))
            last = acum[CHUNK - 1:CHUNK, hd:hd + 1]
            decay_end = jnp.exp(last - col)
            new_t = _dot(b_gt, (xdt * decay_end).astype(BF16))
            prev_t = state_ref[hd]
            y = y + _dot(c_g, prev_t.astype(BF16)) * jnp.exp(col)
            state_ref[hd] = prev_t * jnp.exp(last) + new_t
            y = y + d_ref[:, hd:hd + 1] * x_h
            y_ref[:, hd * SSD_HEAD_DIM:(hd + 1) * SSD_HEAD_DIM] = y
    y = y_ref[...] * _silu(z_ref[0].astype(F32))
    o_ref[0] = _rms(y, nw_ref[...]).astype(o_ref.dtype)


def _ssd(z, xbc, dt, conv_w, conv_b, dt_bias, a_log, d_skip, norm_w):
    bsz, seq, _ = z.shape
    nc = seq // CHUNK

    def lane_pad(v):
        return jnp.pad(v.astype(F32).reshape(1, -1), ((0, 0), (0, LANES - v.shape[-1])))

    tri = jnp.asarray(np.tril(np.ones((CHUNK, CHUNK), np.float32)), BF16)
    blk = lambda w: pl.BlockSpec((1, CHUNK, w), lambda b, c: (b, c, 0))
    return pl.pallas_call(
        _ssd_kernel,
        grid=(bsz, nc),
        in_specs=[blk(SSD_WIDTH), blk(SSD_CONV_DIM), blk(LANES),
                  _const_spec((SSD_CONV, SSD_CONV_DIM)), _const_spec((1, SSD_CONV_DIM)),
                  _const_spec((1, LANES)), _const_spec((1, LANES)), _const_spec((1, LANES)),
                  _const_spec((1, SSD_WIDTH)), _const_spec((CHUNK, CHUNK))],
        out_specs=blk(SSD_WIDTH),
        out_shape=jax.ShapeDtypeStruct((bsz, seq, SSD_WIDTH), BF16),
        scratch_shapes=[pltpu.VMEM((CHUNK + CONV_PAD, SSD_CONV_DIM), F32),
                        pltpu.VMEM((SSD_HEADS, SSD_STATE, SSD_HEAD_DIM), F32),
                        pltpu.VMEM((CHUNK, SSD_WIDTH), F32)],
        compiler_params=_params("parallel", "arbitrary"),
        name="ssd",
    )(z, xbc, dt, conv_w.astype(F32), conv_b.astype(F32).reshape(1, -1),
      lane_pad(dt_bias), lane_pad(-jnp.exp(a_log.astype(F32))), lane_pad(d_skip),
      norm_w.astype(F32).reshape(1, -1), tri)


def _retention_tables():
    h = np.arange(RET_HEADS, dtype=np.float64)
    log_gamma = np.log1p(-np.exp2(-5.0 - h))
    idx = np.arange(CHUNK, dtype=np.float64)
    diff = idx[:, None] - idx[None, :]
    dmat = np.where(diff >= 0, np.exp(np.maximum(diff, 0.0)[None] * log_gamma[:, None, None]), 0.0)
    k_decay = np.exp((CHUNK - 1.0 - idx)[:, None] * log_gamma)
    q_decay = np.exp((idx + 1.0)[:, None] * log_gamma)
    c_decay = np.exp(CHUNK * log_gamma)[None, :]
    rep = lambda a: np.repeat(a, RET_HEAD_DIM, axis=1).astype(np.float32)
    avg = np.kron(np.eye(RET_HEADS), np.full((RET_HEAD_DIM, RET_HEAD_DIM), 1.0 / RET_HEAD_DIM))
    return (dmat.astype(np.float32), rep(k_decay * RET_HEAD_DIM ** -0.5), rep(q_decay),
            rep(c_decay), avg.astype(np.float32))


def _rope(x, cos, sin_signed):
    half = RET_HEAD_DIM // 2
    width = x.shape[1]
    fwd = pltpu.roll(x, half, 1)
    bwd = pltpu.roll(x, width - half, 1)
    lane = lax.broadcasted_iota(jnp.int32, x.shape, 1)
    swapped = jnp.where((lane % RET_HEAD_DIM) < half, bwd, fwd)
    return x * cos + swapped * sin_signed


def _retention_kernel(q_ref, k_ref, v_ref, g_ref, cos_ref, sin_ref, dmat_ref, kdec_ref, qdec_ref,
                      cdec_ref, avg_ref, gn_ref, o_ref, state_ref, y_ref):
    c = pl.program_id(1)

    @pl.when(c == 0)
    def _():
        state_ref[...] = jnp.zeros(state_ref.shape, F32)

    reps = RET_WIDTH // LANES
    cos = jnp.tile(cos_ref[0], (1, reps))
    sin = jnp.tile(sin_ref[0], (1, reps))
    q = _rope(q_ref[0].astype(F32), cos, sin)
    k = _rope(k_ref[0].astype(F32), cos, sin)
    v = v_ref[0]
    qb = q.astype(BF16)
    k_t = k.T
    kd_t = (k * kdec_ref[...]).T.astype(BF16)
    k_tb = (k_t * (RET_HEAD_DIM ** -0.5)).astype(BF16)
    qd = qdec_ref[...]
    cd = cdec_ref[...]
    for hd in range(RET_HEADS):
        sl = slice(hd * RET_HEAD_DIM, (hd + 1) * RET_HEAD_DIM)
        q_h = qb[:, sl]
        v_h = v[:, sl]
        scores = _dot(q_h, k_tb[sl, :]) * dmat_ref[hd]
        y = _dot(scores.astype(BF16), v_h)
        prev = state_ref[hd]
        y = y + _dot(q_h, prev.astype(BF16)) * qd[:, sl]
        state_ref[hd] = prev * cd[:, sl] + _dot(kd_t[sl, :], v_h)
        y_ref[:, sl] = y
    y = y_ref[...]
    avg = avg_ref[...]
    mu = _split_dot(y, avg)
    yc = y - mu
    var = _split_dot(yc * yc, avg)
    yn = yc * lax.rsqrt(var + EPS) * gn_ref[...]
    o_ref[0] = (_silu(g_ref[0].astype(F32)) * yn).astype(o_ref.dtype)


def _retention(q, k, v, g, cos2, sin2, gn_w):
    bsz, seq, _ = q.shape
    nc = seq // CHUNK
    dmat, kdec, qdec, cdec, avg = (jnp.asarray(a) for a in _retention_tables())
    avg = avg.astype(BF16)
    blk = lambda w: pl.BlockSpec((1, CHUNK, w), lambda b, c: (b, c, 0))
    return pl.pallas_call(
        _retention_kernel,
        grid=(bsz, nc),
        in_specs=[blk(RET_WIDTH)] * 4 + [blk(LANES)] * 2 + [
            _const_spec(dmat.shape), _const_spec(kdec.shape), _const_spec(qdec.shape),
            _const_spec(cdec.shape), _const_spec(avg.shape), _const_spec((1, RET_WIDTH))],
        out_specs=blk(RET_WIDTH),
        out_shape=jax.ShapeDtypeStruct((bsz, seq, RET_WIDTH), BF16),
        scratch_shapes=[pltpu.VMEM((RET_HEADS, RET_HEAD_DIM, RET_HEAD_DIM), F32),
                        pltpu.VMEM((CHUNK, RET_WIDTH), F32)],
        compiler_params=_params("parallel", "arbitrary"),
        name="retention",
    )(q, k, v, g, cos2, sin2, dmat, kdec, qdec, cdec, avg, gn_w.astype(F32).reshape(1, -1))


def _rope_tables(positions):
    half = RET_HEAD_DIM // 2
    inv_freq = ROPE_BASE ** (-jnp.arange(half, dtype=F32) / half)
    ang = positions.astype(F32)[..., None] * inv_freq
    cos = jnp.cos(ang)
    sin = jnp.sin(ang)
    cos2 = jnp.concatenate([cos, cos, cos, cos], axis=-1)
    sin2 = jnp.concatenate([-sin, sin, -sin, sin], axis=-1)
    return cos2, sin2


def _lru_kernel(x_ref, gate_ref, cw_ref, cb_ref, w_ref, bias_ref, lamc_ref, o_ref, xpad_ref, h_ref):
    c = pl.program_id(1)
    first = c == 0

    @pl.when(first)
    def _():
        h_ref[...] = jnp.zeros(h_ref.shape, F32)

    rows = x_ref.shape[1]
    xc = _causal_conv(xpad_ref, x_ref[0].astype(F32), cw_ref[...], cb_ref[...], first)
    ri = _sigmoid(_dot(xc.astype(BF16), w_ref[...]) + bias_ref[...])
    log_a = lamc_ref[...] * ri[:, :LRU_WIDTH]
    a_all = jnp.exp(log_a)
    mult = jnp.sqrt(jnp.maximum(1.0 - jnp.exp(2.0 * log_a), 0.0))
    b_all = mult * ri[:, LRU_WIDTH:] * xc
    gate = _gelu(gate_ref[0].astype(F32))
    n_steps = int(math.log2(rows))
    row0 = lax.broadcasted_iota(jnp.int32, (rows, LANES), 0) == 0
    for j in range(LRU_WIDTH // LANES):
        sl = slice(j * LANES, (j + 1) * LANES)
        a = a_all[:, sl]
        b = b_all[:, sl]
        b = b + jnp.where(row0, a * h_ref[0:1, sl], 0.0)
        for s in range(n_steps):
            k = 1 << s
            b = b + a * _shift_rows(b, k)
            if s + 1 < n_steps:
                a = a * _shift_rows(a, k, 1.0)
        h_ref[0:1, sl] = b[rows - 1:rows, :]
        o_ref[0, :, sl] = (b * gate[:, sl]).astype(o_ref.dtype)


def _block_diag(w):
    nb, n, _ = w.shape
    eye = jnp.eye(nb, dtype=w.dtype)
    return jnp.einsum('bij,bc->bicj', w, eye).reshape(nb * n, nb * n)


def _lru(x, gate, conv_w, conv_b, wa, ba, wx, bx, lam):
    bsz, seq, _ = x.shape
    rows = min(LRU_ROWS, seq)
    w = jnp.concatenate([_block_diag(wa), _block_diag(wx)], axis=1).astype(BF16)
    bias = jnp.concatenate([ba, bx]).astype(F32).reshape(1, -1)
    lamc = (-LRU_C * jax.nn.softplus(-lam.astype(F32))).reshape(1, -1)
    blk = lambda wd: pl.BlockSpec((1, rows, wd), lambda b, c: (b, c, 0))
    return pl.pallas_call(
        _lru_kernel,
        grid=(bsz, seq // rows),
        in_specs=[blk(LRU_WIDTH), blk(LRU_WIDTH),
                  _const_spec((LRU_CONV, LRU_WIDTH)), _const_spec((1, LRU_WIDTH)),
                  _const_spec((LRU_WIDTH, 2 * LRU_WIDTH)), _const_spec((1, 2 * LRU_WIDTH)),
                  _const_spec((1, LRU_WIDTH))],
        out_specs=blk(LRU_WIDTH),
        out_shape=jax.ShapeDtypeStruct((bsz, seq, LRU_WIDTH), BF16),
        scratch_shapes=[pltpu.VMEM((rows + CONV_PAD, LRU_WIDTH), F32),
                        pltpu.VMEM((SUBLANES, LRU_WIDTH), F32)],
        compiler_params=_params("parallel", "arbitrary"),
        name="rglru",
    )(x, gate, conv_w.astype(F32), conv_b.astype(F32).reshape(1, -1), w, bias, lamc)


def _merge_kernel(h_ref, u_ref, ys5_ref, b1_ref, b2_ref, b3_ref, gl_ref, bg_ref, d_ref, wglu_ref,
                  wb_ref, wo_ref, o_ref):
    y = jnp.concatenate([ys5_ref[q].astype(F32) + d_ref[:, q * LANES:(q + 1) * LANES] * u_ref[q].astype(F32)
                         for q in range(S5_Q)], axis=1)
    y = _gelu(y)
    b0 = (y * _sigmoid(_dot(y.astype(BF16), wglu_ref[...]))).astype(BF16)
    branches = (b0, b1_ref[...], b2_ref[...], b3_ref[...])
    merged = None
    for i, br in enumerate(branches):
        sl = slice(i * D_MODEL, (i + 1) * D_MODEL)
        gate = _sigmoid(gl_ref[:, sl].astype(F32) + bg_ref[:, sl])
        term = gate * _dot(br, wb_ref[i])
        merged = term if merged is None else merged + term
    o_ref[...] = h_ref[...] + _dot(merged.astype(BF16), wo_ref[...])


def _merge(h, u, ys5, b1, b2, b3, gate_logits, b_gate, s5_d, w_glu, w_branch, w_out):
    t = h.shape[0]
    tm = min(ROW_TILE, t)
    row = lambda w: pl.BlockSpec((tm, w), lambda i: (i, 0))
    s5_blk = pl.BlockSpec((S5_Q, tm, LANES), lambda i: (0, i, 0))
    return pl.pallas_call(
        _merge_kernel,
        grid=(t // tm,),
        in_specs=[row(D_MODEL), s5_blk, s5_blk] + [row(512)] * 3 + [row(N_BRANCH * D_MODEL),
                  _const_spec((1, N_BRANCH * D_MODEL)), _const_spec((1, S5_WIDTH)),
                  _const_spec((S5_WIDTH, S5_WIDTH)),
                  _const_spec((N_BRANCH, 512, D_MODEL)), _const_spec((D_MODEL, D_MODEL))],
        out_specs=row(D_MODEL),
        out_shape=jax.ShapeDtypeStruct((t, D_MODEL), F32),
        compiler_params=_params("parallel"),
        name="merge",
    )(h, u, ys5, b1, b2, b3, gate_logits, b_gate.astype(F32).reshape(1, -1),
      s5_d.astype(F32).reshape(1, -1), w_glu.astype(BF16), w_branch.astype(BF16),
      w_out.astype(BF16))


def _kv_kernel(mem_ref, nw_ref, w_ref, k_ref, v_ref):
    mb = _rms(mem_ref[0], nw_ref[...]).astype(BF16)
    for c in range(0, D_MODEL, DOT_COLS):
        k_ref[0, :, c:c + DOT_COLS] = _dot(mb, w_ref[:, c:c + DOT_COLS]).astype(BF16)
        v_ref[0, :, c:c + DOT_COLS] = _dot(
            mb, w_ref[:, D_MODEL + c:D_MODEL + c + DOT_COLS]).astype(BF16)


def _kv(mem, norm_w, wk, wv):
    bsz, n_mem, _ = mem.shape
    w = jnp.concatenate([wk, wv], axis=1).astype(BF16)
    blk = pl.BlockSpec((1, n_mem, D_MODEL), lambda b: (b, 0, 0))
    return pl.pallas_call(
        _kv_kernel,
        grid=(bsz,),
        in_specs=[blk, _const_spec((1, D_MODEL)), _const_spec((D_MODEL, 2 * D_MODEL))],
        out_specs=[blk, blk],
        out_shape=[jax.ShapeDtypeStruct((bsz, n_mem, D_MODEL), BF16)] * 2,
        compiler_params=_params("parallel"),
        name="xattn_kv",
    )(mem, norm_w.astype(F32).reshape(1, -1), w)


def _xattn_kernel(h_ref, nw_ref, wq_ref, k_ref, v_ref, wo_ref, o_ref, att_ref):
    h = h_ref[0]
    xb = _rms(h, nw_ref[...]).astype(BF16)
    scale = XA_HEAD_DIM ** -0.5
    for hd in range(XA_HEADS):
        sl = slice(hd * XA_HEAD_DIM, (hd + 1) * XA_HEAD_DIM)
        q = _dot(xb, wq_ref[:, sl]).astype(BF16)
        s = _dot_nt(q, k_ref[0, :, sl]) * scale
        p = jnp.exp(s - jnp.max(s, axis=-1, keepdims=True))
        denom = jnp.sum(p, axis=-1, keepdims=True)
        att_ref[:, sl] = (_dot(p.astype(BF16), v_ref[0, :, sl]) / denom).astype(BF16)
    o_ref[0] = h + _dot(att_ref[...], wo_ref[...])


def _xattn(h3, norm_w, wq, k, v, wo):
    bsz, seq, _ = h3.shape
    tq = min(ROW_TILE, seq)
    n_mem = k.shape[1]
    row = pl.BlockSpec((1, tq, D_MODEL), lambda b, i: (b, i, 0))
    kv = pl.BlockSpec((1, n_mem, D_MODEL), lambda b, i: (b, 0, 0))
    return pl.pallas_call(
        _xattn_kernel,
        grid=(bsz, seq // tq),
        in_specs=[row, _const_spec((1, D_MODEL)), _const_spec((D_MODEL, D_MODEL)), kv, kv,
                  _const_spec((D_MODEL, D_MODEL))],
        out_specs=row,
        out_shape=jax.ShapeDtypeStruct((bsz, seq, D_MODEL), F32),
        scratch_shapes=[pltpu.VMEM((tq, D_MODEL), BF16)],
        compiler_params=_params("parallel", "parallel"),
        name="xattn",
    )(h3, norm_w.astype(F32).reshape(1, -1), wq.astype(BF16), k, v, wo.astype(BF16))


FF_COLS = 256


def _swiglu_acc(xb, w1_ref, w3_ref, w2_ref, width, lead=()):
    acc = None
    for c in range(0, width, FF_COLS):
        a = _dot(xb, w1_ref[lead + (slice(None), slice(c, c + FF_COLS))])
        b = _dot(xb, w3_ref[lead + (slice(None), slice(c, c + FF_COLS))])
        g = (_silu(a) * b).astype(BF16)
        term = _dot(g, w2_ref[lead + (slice(c, c + FF_COLS), slice(None))])
        acc = term if acc is None else acc + term
    return acc


def _ffn_kernel(h_ref, nw_ref, w1_ref, w3_ref, w2_ref, o_ref):
    h = h_ref[...]
    xb = _rms(h, nw_ref[...]).astype(BF16)
    o_ref[...] = h + _swiglu_acc(xb, w1_ref, w3_ref, w2_ref, D_FF)


def _ffn(h, norm_w, w1, w3, w2):
    t = h.shape[0]
    tm = min(ROW_TILE, t)
    row = pl.BlockSpec((tm, D_MODEL), lambda i: (i, 0))
    return pl.pallas_call(
        _ffn_kernel,
        grid=(t // tm,),
        in_specs=[row, _const_spec((1, D_MODEL)), _const_spec((D_MODEL, D_FF)),
                  _const_spec((D_MODEL, D_FF)), _const_spec((D_FF, D_MODEL))],
        out_specs=row,
        out_shape=jax.ShapeDtypeStruct((t, D_MODEL), F32),
        compiler_params=_params("parallel"),
        name="ffn",
    )(h, norm_w.astype(F32).reshape(1, -1), w1.astype(BF16), w3.astype(BF16), w2.astype(BF16))


MOE_TILE = 512
TOP_K = 2
DISPATCH_ROWS = 512
COMBINE_ROWS = 256
DMA_UNROLL = 8


def _sorted_tiles(t):
    return pl.cdiv(TOP_K * t, MOE_TILE) + N_EXPERTS


def _router_kernel(h_ref, nw_ref, wr_ref, xn_ref, member_ref, wsel_ref, esel_ref):
    xn = _rms(h_ref[...], nw_ref[...])
    xn_ref[...] = xn
    logits = lax.dot_general(xn, wr_ref[...], (((1,), (0,)), ((), ())),
                             precision=lax.Precision.HIGHEST, preferred_element_type=F32)
    lane = lax.broadcasted_iota(jnp.int32, logits.shape, 1)
    logits = jnp.where(lane < N_EXPERTS, logits, -jnp.inf)
    m1 = jnp.max(logits, axis=-1, keepdims=True)
    i1 = jnp.min(jnp.where(logits == m1, lane, LANES), axis=-1, keepdims=True)
    rest = jnp.where(lane == i1, -jnp.inf, logits)
    m2 = jnp.max(rest, axis=-1, keepdims=True)
    i2 = jnp.min(jnp.where(rest == m2, lane, LANES), axis=-1, keepdims=True)
    e2 = jnp.exp(m2 - m1)
    w1 = 1.0 / (1.0 + e2)
    w2 = e2 / (1.0 + e2)
    member_ref[...] = jnp.where(lane == i1, 1.0, jnp.where(lane == i2, 1.0, 0.0)).astype(BF16)
    wsel_ref[...] = jnp.where(lane == 0, w1, jnp.where(lane == 1, w2, 0.0))
    esel_ref[...] = jnp.where(lane == 0, i1, jnp.where(lane == 1, i2, 0))


def _router(h, norm_w, w_router):
    t = h.shape[0]
    tm = min(ROW_TILE, t)
    wr = jnp.pad(w_router.astype(F32), ((0, 0), (0, LANES - N_EXPERTS)))
    row = lambda w: pl.BlockSpec((tm, w), lambda i: (i, 0))
    return pl.pallas_call(
        _router_kernel,
        grid=(t // tm,),
        in_specs=[row(D_MODEL), _const_spec((1, D_MODEL)), _const_spec((D_MODEL, LANES))],
        out_specs=[row(D_MODEL), row(LANES), row(LANES), row(LANES)],
        out_shape=[jax.ShapeDtypeStruct((t, D_MODEL), F32), jax.ShapeDtypeStruct((t, LANES), BF16),
                   jax.ShapeDtypeStruct((t, LANES), F32), jax.ShapeDtypeStruct((t, LANES), jnp.int32)],
        compiler_params=_params("parallel"),
        name="moe_router",
    )(h, norm_w.astype(F32).reshape(1, -1), wr)


def _positions_kernel(member_ref, esel_ref, pos_ref, meta_ref, cnt_ref, carry_ref, off_ref):
    phase = pl.program_id(0)
    i = pl.program_id(1)
    m = member_ref[...]
    tp = m.shape[0]
    col_sum = _dot(jnp.ones((SUBLANES, tp), BF16), m)

    @pl.when((phase == 0) & (i == 0))
    def _():
        cnt_ref[...] = jnp.zeros(cnt_ref.shape, F32)

    @pl.when(phase == 0)
    def _():
        cnt_ref[...] += col_sum

    @pl.when((phase == 1) & (i == 0))
    def _():
        tiles = jnp.floor((cnt_ref[...] + (MOE_TILE - 1.0)) * (1.0 / MOE_TILE))
        r = lax.broadcasted_iota(jnp.int32, (LANES, LANES), 0)
        c = lax.broadcasted_iota(jnp.int32, (LANES, LANES), 1)
        before = jnp.where(r < c, 1.0, 0.0).astype(BF16)
        first_tile = _dot(tiles.astype(BF16), before)
        off_ref[...] = first_tile * MOE_TILE
        carry_ref[...] = jnp.zeros(carry_ref.shape, F32)
        row = lax.broadcasted_iota(jnp.int32, (SUBLANES, LANES), 0)
        meta_ref[...] = jnp.where(row == 0, first_tile, tiles).astype(jnp.int32)

    @pl.when(phase == 1)
    def _():
        r = lax.broadcasted_iota(jnp.int32, (tp, tp), 0)
        c = lax.broadcasted_iota(jnp.int32, (tp, tp), 1)
        earlier = jnp.where(r > c, 1.0, 0.0).astype(BF16)
        posm = off_ref[0:1, :] + carry_ref[0:1, :] + _dot(earlier, m)
        lane = lax.broadcasted_iota(jnp.int32, posm.shape, 1)
        e = esel_ref[...]
        p0 = jnp.sum(jnp.where(lane == e[:, 0:1], posm, 0.0), axis=-1, keepdims=True)
        p1 = jnp.sum(jnp.where(lane == e[:, 1:2], posm, 0.0), axis=-1, keepdims=True)
        pos_ref[...] = jnp.where(lane == 0, p0, jnp.where(lane == 1, p1, 0.0)).astype(jnp.int32)
        carry_ref[...] += col_sum


def _positions(member, esel):
    t = member.shape[0]
    tp = min(ROW_TILE, t)
    return pl.pallas_call(
        _positions_kernel,
        grid=(2, t // tp),
        in_specs=[pl.BlockSpec((tp, LANES), lambda p, i: (i, 0)),
                  pl.BlockSpec((tp, LANES), lambda p, i: (i * p, 0))],
        out_specs=[pl.BlockSpec((tp, LANES), lambda p, i: (i * p, 0)),
                   pl.BlockSpec((SUBLANES, LANES), lambda p, i: (0, 0))],
        out_shape=[jax.ShapeDtypeStruct((t, LANES), jnp.int32),
                   jax.ShapeDtypeStruct((SUBLANES, LANES), jnp.int32)],
        scratch_shapes=[pltpu.VMEM((SUBLANES, LANES), F32)] * 3,
        compiler_params=_params("arbitrary", "arbitrary"),
        name="moe_positions",
    )(member, esel)


def _row_copy(src_ref, src_row, dst_ref, dst_row, sem):
    return pltpu.make_async_copy(src_ref.at[pl.ds(src_row, 1)], dst_ref.at[pl.ds(dst_row, 1)], sem)


def _dispatch_kernel(pos_ref, xn_ref, xs_in_ref, xs_ref, sem):
    del xs_in_ref
    rows = xn_ref.shape[0]

    def copies(r):
        return [_row_copy(xn_ref, r, xs_ref, pos_ref[0, k, r], sem) for k in range(TOP_K)]

    def issue(r, carry):
        for cp in copies(r):
            cp.start()
        return carry

    def drain(r, carry):
        for cp in copies(r):
            cp.wait()
        return carry

    lax.fori_loop(0, rows, issue, 0, unroll=DMA_UNROLL)
    lax.fori_loop(0, rows, drain, 0, unroll=DMA_UNROLL)


def _slot_major(pos, rows):
    t = pos.shape[0]
    return jnp.transpose(pos[:, :TOP_K].reshape(t // rows, rows, TOP_K), (0, 2, 1))


def _dispatch(xn, pos, n_sorted):
    t = xn.shape[0]
    rows = min(DISPATCH_ROWS, t)
    xs0 = jnp.zeros((n_sorted, D_MODEL), F32)
    return pl.pallas_call(
        _dispatch_kernel,
        grid=(t // rows,),
        in_specs=[pl.BlockSpec((1, TOP_K, rows), lambda i: (i, 0, 0), memory_space=pltpu.SMEM),
                  pl.BlockSpec((rows, D_MODEL), lambda i: (i, 0)),
                  pl.BlockSpec(memory_space=pl.ANY)],
        out_specs=pl.BlockSpec(memory_space=pl.ANY),
        out_shape=jax.ShapeDtypeStruct((n_sorted, D_MODEL), F32),
        scratch_shapes=[pltpu.SemaphoreType.DMA(())],
        input_output_aliases={2: 0},
        compiler_params=_params("arbitrary"),
        name="moe_dispatch",
    )(_slot_major(pos, rows), xn, xs0)


def _experts_kernel(tile_expert_ref, n_used_ref, xs_ref, w1_ref, w3_ref, w2_ref, y_ref):
    del tile_expert_ref
    used = pl.program_id(0) < n_used_ref[0]

    @pl.when(used)
    def _():
        xb = xs_ref[...].astype(BF16)
        y_ref[...] = _swiglu_acc(xb, w1_ref, w3_ref, w2_ref, D_FF_EXPERT, lead=(0,))

    @pl.when(jnp.logical_not(used))
    def _():
        y_ref[...] = jnp.zeros(y_ref.shape, F32)


def _experts(xs, tile_expert, n_used, w1, w3, w2):
    n_tiles = xs.shape[0] // MOE_TILE
    row = pl.BlockSpec((MOE_TILE, D_MODEL), lambda i, te, nu: (i, 0))
    wspec = lambda shape: pl.BlockSpec((1,) + shape, lambda i, te, nu: (te[i], 0, 0),
                                       pipeline_mode=pl.Buffered(1))
    return pl.pallas_call(
        _experts_kernel,
        grid_spec=pltpu.PrefetchScalarGridSpec(
            num_scalar_prefetch=2,
            grid=(n_tiles,),
            in_specs=[row, wspec((D_MODEL, D_FF_EXPERT)), wspec((D_MODEL, D_FF_EXPERT)),
                      wspec((D_FF_EXPERT, D_MODEL))],
            out_specs=row),
        out_shape=jax.ShapeDtypeStruct(xs.shape, F32),
        compiler_params=_params("arbitrary"),
        name="moe_experts",
    )(tile_expert, n_used, xs, w1.astype(BF16), w3.astype(BF16), w2.astype(BF16))


def _combine_kernel(pos_ref, h_ref, wsel_ref, fw_ref, y_ref, o_ref, ybuf_ref, sem, *, final_norm):
    rows = h_ref.shape[0]

    def copies(r):
        return [_row_copy(y_ref, pos_ref[0, k, r], ybuf_ref.at[k], r, sem) for k in range(TOP_K)]

    def issue(r, carry):
        for cp in copies(r):
            cp.start()
        return carry

    def drain(r, carry):
        for cp in copies(r):
            cp.wait()
        return carry

    lax.fori_loop(0, rows, issue, 0, unroll=DMA_UNROLL)
    lax.fori_loop(0, rows, drain, 0, unroll=DMA_UNROLL)
    w = wsel_ref[...]
    out = h_ref[...] + w[:, 0:1] * ybuf_ref[0] + w[:, 1:2] * ybuf_ref[1]
    if final_norm:
        out = _rms(out, fw_ref[...])
    o_ref[...] = out


def _combine(h, wsel, pos, y_sorted, final_w, final_norm):
    t = h.shape[0]
    rows = min(COMBINE_ROWS, t)
    return pl.pallas_call(
        functools.partial(_combine_kernel, final_norm=final_norm),
        grid=(t // rows,),
        in_specs=[pl.BlockSpec((1, TOP_K, rows), lambda i: (i, 0, 0), memory_space=pltpu.SMEM),
                  pl.BlockSpec((rows, D_MODEL), lambda i: (i, 0)),
                  pl.BlockSpec((rows, LANES), lambda i: (i, 0)),
                  _const_spec((1, D_MODEL)),
                  pl.BlockSpec(memory_space=pl.ANY)],
        out_specs=pl.BlockSpec((rows, D_MODEL), lambda i: (i, 0)),
        out_shape=jax.ShapeDtypeStruct((t, D_MODEL), F32),
        scratch_shapes=[pltpu.VMEM((TOP_K, rows, D_MODEL), F32), pltpu.SemaphoreType.DMA(())],
        compiler_params=_params("arbitrary"),
        name="moe_combine",
    )(_slot_major(pos, rows), h, wsel, final_w.astype(F32).reshape(1, -1), y_sorted)


def _moe(h, norm_w, w_router, w1, w3, w2, final_w, final_norm):
    t = h.shape[0]
    n_tiles = _sorted_tiles(t)
    xn, member, wsel, esel = _router(h, norm_w, w_router)
    pos, meta = _positions(member, esel)
    first_tile = meta[0, :N_EXPERTS]
    last_tile = first_tile + meta[1, :N_EXPERTS]
    tile = jnp.arange(n_tiles, dtype=jnp.int32)
    tile_expert = jnp.minimum(jnp.sum(tile[:, None] >= last_tile[None, :], axis=1), N_EXPERTS - 1)
    xs = _dispatch(xn, pos, n_tiles * MOE_TILE)
    ys = _experts(xs, tile_expert.astype(jnp.int32), last_tile[N_EXPERTS - 1:], w1, w3, w2)
    return _combine(h, wsel, pos, ys, final_w, final_norm)


def _final_norm_kernel(h_ref, w_ref, o_ref):
    o_ref[...] = _rms(h_ref[...], w_ref[...])


def _final_norm(h, w):
    t = h.shape[0]
    tm = min(ROW_TILE, t)
    row = pl.BlockSpec((tm, D_MODEL), lambda i: (i, 0))
    return pl.pallas_call(
        _final_norm_kernel, grid=(t // tm,), in_specs=[row, _const_spec((1, D_MODEL))],
        out_specs=row, out_shape=jax.ShapeDtypeStruct((t, D_MODEL), F32),
        compiler_params=_params("parallel"), name="final_norm",
    )(h, w.astype(F32).reshape(1, -1))


def _mixing_block(h, bsz, seq, cos2, sin2, norm_w, w_in, b_gate,
                  s5_lam_re, s5_lam_im, s5_b_re, s5_b_im, s5_c_re, s5_c_im, s5_d, s5_log_dt, s5_w_glu,
                  ssd_conv_w, ssd_conv_b, ssd_dt_bias, ssd_a_log, ssd_d, ssd_norm,
                  ret_norm,
                  lru_conv_w, lru_conv_b, lru_wa, lru_ba, lru_wx, lru_bx, lru_lam,
                  w_branch, w_out):
    (u_s5, z_ssd, xbc_ssd, dt_ssd, q_ret, k_ret, v_ret, g_ret, x_lru, gate_lru,
     gate_logits) = _inproj(h, norm_w, _pack_w_in(w_in))
    seq3 = lambda a: a.reshape(bsz, seq, a.shape[-1])
    tables = _s5_tables(s5_lam_re, s5_lam_im, s5_b_re, s5_b_im, s5_c_re, s5_c_im, s5_log_dt,
                        seq // S5_SUB)
    y_s5 = _s5_scan(u_s5, tables, bsz, seq)
    y_ssd = _ssd(seq3(z_ssd), seq3(xbc_ssd), seq3(dt_ssd), ssd_conv_w, ssd_conv_b, ssd_dt_bias,
                 ssd_a_log, ssd_d, ssd_norm)
    y_ret = _retention(seq3(q_ret), seq3(k_ret), seq3(v_ret), seq3(g_ret), cos2, sin2, ret_norm)
    y_lru = _lru(seq3(x_lru), seq3(gate_lru), lru_conv_w, lru_conv_b, lru_wa, lru_ba, lru_wx,
                 lru_bx, lru_lam)
    flat = lambda a: a.reshape(bsz * seq, a.shape[-1])
    return _merge(h, u_s5, y_s5, flat(y_ssd), flat(y_ret), flat(y_lru), gate_logits,
                  b_gate, s5_d, s5_w_glu, w_branch, w_out)


def kernel(x, mem, positions, norm_mix, w_in, b_gate, s5_lam_re, s5_lam_im, s5_b_re, s5_b_im, s5_c_re, s5_c_im, s5_d, s5_log_dt, s5_w_glu, ssd_conv_w, ssd_conv_b, ssd_dt_bias, ssd_a_log, ssd_d, ssd_norm, ret_norm, lru_conv_w, lru_conv_b, lru_wa, lru_ba, lru_wx, lru_bx, lru_lam, w_branch, w_out, norm_xa, norm_mem, xa_wq, xa_wk, xa_wv, xa_wo, norm_ffn, ffn_w1, ffn_w3, ffn_w2, moe_router, moe_w1, moe_w3, moe_w2, norm_final):
    bsz, seq, _ = x.shape
    depth = norm_mix.shape[0]
    cos2, sin2 = _rope_tables(positions)
    h = x.reshape(bsz * seq, D_MODEL)
    for i in range(depth):
        h = _mixing_block(h, bsz, seq, cos2, sin2, norm_mix[i], w_in[i], b_gate[i],
                          s5_lam_re[i], s5_lam_im[i], s5_b_re[i], s5_b_im[i], s5_c_re[i], s5_c_im[i],
                          s5_d[i], s5_log_dt[i], s5_w_glu[i],
                          ssd_conv_w[i], ssd_conv_b[i], ssd_dt_bias[i], ssd_a_log[i], ssd_d[i],
                          ssd_norm[i], ret_norm[i],
                          lru_conv_w[i], lru_conv_b[i], lru_wa[i], lru_ba[i], lru_wx[i], lru_bx[i],
                          lru_lam[i], w_branch[i], w_out[i])
        k, v = _kv(mem, norm_mem[i], xa_wk[i], xa_wv[i])
        h = _xattn(h.reshape(bsz, seq, D_MODEL), norm_xa[i], xa_wq[i], k, v, xa_wo[i])
        h = h.reshape(bsz * seq, D_MODEL)
        last = i == depth - 1
        if i % 2 == 0:
            h = _ffn(h, norm_ffn[i], ffn_w1[i // 2], ffn_w3[i // 2], ffn_w2[i // 2])
            if last:
                h = _final_norm(h, norm_final)
        else:
            h = _moe(h, norm_ffn[i], moe_router[i // 2], moe_w1[i // 2], moe_w3[i // 2],
                     moe_w2[i // 2], norm_final, last)
    return h.reshape(bsz, seq, D_MODEL)
```

```python
import functools
import math

import numpy as np
import jax
import jax.numpy as jnp
from jax import lax
from jax.experimental import pallas as pl
from jax.experimental.pallas import tpu as pltpu

F32 = jnp.float32
BF16 = jnp.bfloat16

D_MODEL = 1024
N_MEM = 256
EPS = 1e-6
CHUNK = 128
N_BRANCH = 4
S5_WIDTH = 512
S5_GROUP = 16
S5_GROUPS = 32
S5_STATE = 64
S5_SUB = 16
SSD_HEADS = 8
SSD_HEAD_DIM = 64
SSD_WIDTH = 512
SSD_GROUPS = 2
SSD_STATE = 64
SSD_CONV = 4
SSD_CONV_DIM = SSD_WIDTH + 2 * SSD_GROUPS * SSD_STATE
RET_HEADS = 8
RET_HEAD_DIM = 64
RET_WIDTH = 512
ROPE_BASE = 10000.0
LRU_WIDTH = 512
LRU_BLOCKS = 8
LRU_BLOCK = 64
LRU_CONV = 4
LRU_C = 8.0
XA_HEADS = 4
XA_HEAD_DIM = 256
D_FF = 2816
N_EXPERTS = 8
D_FF_EXPERT = 3584
SECTION_WIDTHS = (S5_WIDTH, SSD_WIDTH, SSD_CONV_DIM, SSD_HEADS,
                  RET_WIDTH, RET_WIDTH, RET_WIDTH, RET_WIDTH,
                  LRU_WIDTH, LRU_WIDTH, N_BRANCH * D_MODEL)

LANES = 128
SUBLANES = 8
VMEM_LIMIT = 56 * 1024 * 1024
ROW_TILE = 512
LRU_ROWS = 256


def _params(*sem):
    return pltpu.CompilerParams(dimension_semantics=sem, vmem_limit_bytes=VMEM_LIMIT)


def _const_spec(shape):
    nd = len(shape)
    return pl.BlockSpec(shape, lambda *_: (0,) * nd)


def _rms(x, w):
    return x * lax.rsqrt(jnp.mean(x * x, axis=-1, keepdims=True) + EPS) * w


def _sigmoid(x):
    return 0.5 + 0.5 * jnp.tanh(0.5 * x)


def _silu(x):
    return x * _sigmoid(x)


def _gelu(x):
    return jax.nn.gelu(x)


def _softplus(x):
    return jnp.maximum(x, 0.0) + jnp.log(1.0 + jnp.exp(-jnp.abs(x)))


def _dot(a, b):
    return jnp.dot(a, b, preferred_element_type=F32)


def _dot_nt(a, b):
    return lax.dot_general(a, b, (((1,), (1,)), ((), ())), preferred_element_type=F32)


def _split3(x):
    hi = x.astype(BF16)
    r1 = x - hi.astype(F32)
    mid = r1.astype(BF16)
    lo = (r1 - mid.astype(F32)).astype(BF16)
    return hi, mid, lo


def _split_dot(x, m_bf16):
    hi, mid, lo = _split3(x)
    return _dot(hi, m_bf16) + _dot(mid, m_bf16) + _dot(lo, m_bf16)


def _split_dot_left(m_bf16, x):
    hi, mid, lo = _split3(x)
    return _dot(m_bf16, hi) + _dot(m_bf16, mid) + _dot(m_bf16, lo)


def _shift_rows(x, k, fill=0.0):
    rows = x.shape[0]
    if k % SUBLANES == 0:
        return jnp.concatenate([jnp.full((k, x.shape[1]), fill, x.dtype), x[:rows - k]], axis=0)
    rolled = pltpu.roll(x, k, 0)
    row = lax.broadcasted_iota(jnp.int32, x.shape, 0)
    return jnp.where(row >= k, rolled, fill)


IN_WIDTHS = (S5_WIDTH, SSD_WIDTH, SSD_CONV_DIM, LANES,
             RET_WIDTH, RET_WIDTH, RET_WIDTH, RET_WIDTH,
             LRU_WIDTH, LRU_WIDTH, N_BRANCH * D_MODEL)
IN_TOTAL_PADDED = sum(IN_WIDTHS)
DOT_COLS = 512


def _inproj_kernel(h_ref, nw_ref, w_ref, u_ref, *refs):
    out_refs, uscr_ref = refs[:-1], refs[-1]
    xb = _rms(h_ref[...], nw_ref[...]).astype(BF16)
    u = _dot(xb, w_ref[:, :S5_WIDTH])
    folded = u.shape[0] // S5_SUB
    for q in range(S5_Q):
        uscr_ref[q] = u[:, q * LANES:(q + 1) * LANES]
        for t in range(S5_SUB):
            u_ref[q, :, t * LANES:(t + 1) * LANES] = uscr_ref[
                q, pl.ds(t, folded, stride=S5_SUB), :].astype(u_ref.dtype)
    off = S5_WIDTH
    for o_ref, width in zip(out_refs, IN_WIDTHS[1:]):
        for c in range(0, width, DOT_COLS):
            n = min(DOT_COLS, width - c)
            o_ref[:, c:c + n] = _dot(xb, w_ref[:, off + c:off + c + n]).astype(o_ref.dtype)
        off += width


def _inproj(h, norm_w, w_cat):
    t = h.shape[0]
    tm = min(ROW_TILE, t)
    out_shape = ([jax.ShapeDtypeStruct((S5_Q, t // S5_SUB, S5_ROW), BF16)]
                 + [jax.ShapeDtypeStruct((t, w), BF16) for w in IN_WIDTHS[1:]])
    out_specs = ([pl.BlockSpec((S5_Q, tm // S5_SUB, S5_ROW), lambda i: (0, i, 0))]
                 + [pl.BlockSpec((tm, w), lambda i: (i, 0)) for w in IN_WIDTHS[1:]])
    return pl.pallas_call(
        _inproj_kernel,
        grid=(t // tm,),
        in_specs=[pl.BlockSpec((tm, D_MODEL), lambda i: (i, 0)),
                  _const_spec((1, D_MODEL)),
                  _const_spec((D_MODEL, IN_TOTAL_PADDED))],
        out_specs=out_specs,
        out_shape=out_shape,
        scratch_shapes=[pltpu.VMEM((S5_Q, tm, LANES), F32)],
        compiler_params=_params("parallel"),
        name="inproj",
    )(h, norm_w.reshape(1, D_MODEL), w_cat)


def _pack_w_in(w_in):
    pieces = []
    off = 0
    for width in SECTION_WIDTHS:
        sec = w_in[:, off:off + width]
        if width == SSD_HEADS:
            sec = jnp.pad(sec, ((0, 0), (0, LANES - width)))
        pieces.append(sec)
        off += width
    return jnp.concatenate(pieces, axis=1).astype(BF16)


S5_Q = S5_WIDTH // LANES
S5_QG = S5_GROUPS // S5_Q
S5_QS = S5_QG * S5_STATE
S5_ROW = S5_SUB * LANES
S5_TILE = 256
S5_SEQS = 4


def _s5_expand_tables(wc_ref, xc_ref, rin_ref, rout_ref, win_ref, wx_ref):
    group_shift = int(math.log2(S5_GROUP))
    state_shift = int(math.log2(S5_STATE))
    for c in range(0, 2 * S5_QS, DOT_COLS):
        r = lax.broadcasted_iota(jnp.int32, (S5_ROW, DOT_COLS), 0)
        col = lax.broadcasted_iota(jnp.int32, (S5_ROW, DOT_COLS), 1) + c
        own = ((r >> group_shift) & (S5_QG - 1)) == ((col & (S5_QS - 1)) >> state_shift)
        rep = _dot(wc_ref[0], rin_ref[:, c:c + DOT_COLS])
        win_ref[:, c:c + DOT_COLS] = jnp.where(own, rep, 0.0).astype(BF16)
    for c in range(0, S5_ROW, DOT_COLS):
        r = lax.broadcasted_iota(jnp.int32, (2 * S5_QS, DOT_COLS), 0)
        col = lax.broadcasted_iota(jnp.int32, (2 * S5_QS, DOT_COLS), 1) + c
        own = ((r & (S5_QS - 1)) >> state_shift) == ((col >> group_shift) & (S5_QG - 1))
        rep = _dot(xc_ref[0], rout_ref[:, c:c + DOT_COLS])
        wx_ref[:, c:c + DOT_COLS] = jnp.where(own, rep, 0.0).astype(BF16)


def _s5_kernel(u_ref, toe_ref, wc_ref, xc_ref, rin_ref, rout_ref, pw_ref, y_ref,
               win_ref, wx_ref, x_ref, sp_ref, *, rows):
    @pl.when(pl.program_id(1) == 0)
    def _():
        _s5_expand_tables(wc_ref, xc_ref, rin_ref, rout_ref, win_ref, wx_ref)

    n_steps = int(math.log2(rows))
    for c in range(0, 2 * S5_QS, DOT_COLS):
        x_ref[:, c:c + DOT_COLS] = _dot(u_ref[0], win_ref[:, c:c + DOT_COLS])

    def seq(b, carry):
        r0 = pl.multiple_of(b * rows, rows)
        for lc in range(S5_QS // LANES):
            re = slice(lc * LANES, (lc + 1) * LANES)
            im = slice(S5_QS + lc * LANES, S5_QS + (lc + 1) * LANES)
            sr = x_ref[pl.ds(r0, rows), re]
            si = x_ref[pl.ds(r0, rows), im]
            for j in range(n_steps):
                k = 1 << j
                pr = pw_ref[0, 2 * j:2 * j + 1, re]
                pi = pw_ref[0, 2 * j + 1:2 * j + 2, re]
                shr = _shift_rows(sr, k)
                shi = _shift_rows(si, k)
                sr, si = sr + pr * shr - pi * shi, si + pr * shi + pi * shr
            sp_ref[pl.ds(r0, rows), re] = _shift_rows(sr, 1).astype(BF16)
            sp_ref[pl.ds(r0, rows), im] = _shift_rows(si, 1).astype(BF16)
        return carry

    lax.fori_loop(0, u_ref.shape[1] // rows, seq, 0)
    for nn in range(S5_ROW // S5_TILE):
        cols = slice(nn * S5_TILE, (nn + 1) * S5_TILE)
        acc = _dot(sp_ref[...], wx_ref[:, cols])
        for kk in range(nn + 1):
            acc = acc + _dot(u_ref[0, :, kk * S5_TILE:(kk + 1) * S5_TILE], toe_ref[0, nn - kk])
        y_ref[0, :, cols] = acc.astype(y_ref.dtype)


def _s5_tables(lam_re, lam_im, b_re, b_im, c_re, c_im, log_dt, rows):
    lr = lam_re.astype(F32)
    li = lam_im.astype(F32)
    step = jnp.exp(log_dt.astype(F32))[:, None]
    mag = jnp.exp(lr * step)
    ar = mag * jnp.cos(li * step)
    ai = mag * jnp.sin(li * step)
    inv = 1.0 / (lr * lr + li * li)
    cr = ((ar - 1.0) * lr + ai * li) * inv
    ci = (ai * lr - (ar - 1.0) * li) * inv
    bbr = cr[..., None] * b_re - ci[..., None] * b_im
    bbi = cr[..., None] * b_im + ci[..., None] * b_re

    def apow(e):
        e = jnp.asarray(e, F32)[:, None, None]
        m = jnp.exp(lr * step * e)
        return m * jnp.cos(li * step * e), m * jnp.sin(li * step * e)

    sub = S5_SUB
    pr, pi = apow(np.arange(sub + 1))
    m_r = pr[:sub, :, :, None] * bbr - pi[:sub, :, :, None] * bbi
    m_i = pr[:sub, :, :, None] * bbi + pi[:sub, :, :, None] * bbr
    kern = (jnp.einsum('gon,tgni->tgoi', c_re, m_r)
            - jnp.einsum('gon,tgni->tgoi', c_im, m_i))
    eye = jnp.eye(S5_QG, dtype=F32)
    blk = lambda a: a.reshape(a.shape[0], S5_Q, S5_QG, *a.shape[2:])
    kern_bd = jnp.einsum('tqgoi,gh->tqgiho', blk(kern), eye).reshape(sub, S5_Q, LANES, LANES)
    kern_bd = jnp.concatenate([jnp.zeros_like(kern_bd[:1]), kern_bd], axis=0)
    steps_per_tile = S5_TILE // LANES
    d = np.arange(sub // steps_per_tile)[:, None, None]
    tau = steps_per_tile * d + np.arange(steps_per_tile)[None, None, :] - np.arange(steps_per_tile)[None, :, None]
    toe = kern_bd[tau + 1]
    toe = jnp.transpose(toe, (3, 0, 1, 4, 2, 5)).reshape(S5_Q, sub // steps_per_tile, S5_TILE, S5_TILE)
    er, ei = pr[sub - 1 - np.arange(sub)], pi[sub - 1 - np.arange(sub)]
    w_r = er[..., None] * bbr - ei[..., None] * bbi
    w_i = er[..., None] * bbi + ei[..., None] * bbr
    fold_in = lambda w: jnp.transpose(blk(w), (1, 0, 2, 4, 3)).reshape(S5_Q, S5_ROW, S5_STATE)
    wc = jnp.concatenate([fold_in(w_r), fold_in(w_i)], axis=2)
    qr, qi = pr[1:], pi[1:]
    x_r = c_re[None] * qr[:, :, None, :] - c_im[None] * qi[:, :, None, :]
    x_i = -(c_re[None] * qi[:, :, None, :] + c_im[None] * qr[:, :, None, :])
    fold_out = lambda w: jnp.transpose(blk(w), (1, 2, 4, 0, 3)).reshape(S5_Q, S5_QS, sub * S5_GROUP)
    xc = jnp.concatenate([fold_out(x_r), fold_out(x_i)], axis=1)
    n_steps = int(math.log2(rows))
    sr_, si_ = apow(sub * (2.0 ** np.arange(n_steps)))
    pw = jnp.stack([sr_, si_], axis=1).reshape(2 * n_steps, S5_Q, S5_QS)
    pw = jnp.transpose(pw, (1, 0, 2))
    pw = jnp.pad(pw, ((0, 0), (0, 2 * SUBLANES - 2 * n_steps), (0, 0)))
    return toe.astype(BF16), wc.astype(BF16), xc.astype(BF16), pw


def _s5_replicators():
    k = np.arange(2 * S5_STATE)[:, None]
    c = np.arange(2 * S5_QS)[None, :]
    rin = ((k // S5_STATE) == (c // S5_QS)) & ((k % S5_STATE) == (c % S5_STATE))
    r = np.arange(S5_SUB * S5_GROUP)[:, None]
    c = np.arange(S5_ROW)[None, :]
    rout = ((r // S5_GROUP) == (c // LANES)) & ((r % S5_GROUP) == (c % S5_GROUP))
    return jnp.asarray(rin, BF16), jnp.asarray(rout, BF16)


def _s5_scan(u5, tables, bsz, seq):
    toe, wc, xc, pw = tables
    rin, rout = _s5_replicators()
    rows = seq // S5_SUB
    nb = math.gcd(S5_SEQS, bsz)
    row_blk = pl.BlockSpec((1, nb * rows, S5_ROW), lambda q, j: (q, j, 0))
    per_q = lambda a: pl.BlockSpec((1,) + a.shape[1:], lambda q, j: (q,) + (0,) * (a.ndim - 1))
    const = lambda a: pl.BlockSpec(a.shape, lambda q, j: (0,) * a.ndim)
    return pl.pallas_call(
        functools.partial(_s5_kernel, rows=rows),
        grid=(S5_Q, bsz // nb),
        in_specs=[row_blk, per_q(toe), per_q(wc), per_q(xc), const(rin), const(rout), per_q(pw)],
        out_specs=row_blk,
        out_shape=jax.ShapeDtypeStruct(u5.shape, BF16),
        scratch_shapes=[pltpu.VMEM((S5_ROW, 2 * S5_QS), BF16),
                        pltpu.VMEM((2 * S5_QS, S5_ROW), BF16),
                        pltpu.VMEM((nb * rows, 2 * S5_QS), F32),
                        pltpu.VMEM((nb * rows, 2 * S5_QS), BF16)],
        compiler_params=_params("parallel", "arbitrary"),
        name="s5_scan",
    )(u5, toe, wc, xc, rin, rout, pw)


CONV_PAD = SUBLANES


def _causal_conv(xpad_ref, x, w, b, first):
    rows = x.shape[0]

    @pl.when(first)
    def _():
        xpad_ref[0:CONV_PAD, :] = jnp.zeros((CONV_PAD, x.shape[1]), F32)

    xpad_ref[CONV_PAD:CONV_PAD + rows, :] = x
    k = w.shape[0]
    acc = b
    for j in range(k):
        s = CONV_PAD - (k - 1) + j
        acc = acc + w[j:j + 1, :] * xpad_ref[s:s + rows, :]
    xpad_ref[0:CONV_PAD, :] = xpad_ref[rows:rows + CONV_PAD, :]
    return acc


def _ssd_kernel(z_ref, xbc_ref, dt_ref, cw_ref, cb_ref, dtb_ref, a_ref, d_ref, nw_ref, tri_ref,
                o_ref, xpad_ref, state_ref, y_ref):
    c = pl.program_id(1)
    first = c == 0

    @pl.when(first)
    def _():
        state_ref[...] = jnp.zeros(state_ref.shape, F32)

    xbc = _causal_conv(xpad_ref, xbc_ref[0].astype(F32), cw_ref[...], cb_ref[...], first)
    xbc = _silu(xbc)
    xs = xbc[:, :SSD_WIDTH]
    bs = xbc[:, SSD_WIDTH:SSD_WIDTH + LANES]
    cs = xbc[:, SSD_WIDTH + LANES:]
    bs_t = bs.T.astype(BF16)
    bs_b = bs.astype(BF16)
    cs_b = cs.astype(BF16)
    dt = _softplus(dt_ref[0].astype(F32) + dtb_ref[...])
    da = dt * a_ref[...]
    acum = _split_dot_left(tri_ref[...], da)
    acum_t = acum.T
    causal = (lax.broadcasted_iota(jnp.int32, (CHUNK, CHUNK), 0)
              >= lax.broadcasted_iota(jnp.int32, (CHUNK, CHUNK), 1))
    for g in range(SSD_GROUPS):
        b_g = bs_b[:, g * SSD_STATE:(g + 1) * SSD_STATE]
        c_g = cs_b[:, g * SSD_STATE:(g + 1) * SSD_STATE]
        b_gt = bs_t[g * SSD_STATE:(g + 1) * SSD_STATE, :]
        cb = _dot_nt(c_g, b_g)
        for r in range(SSD_HEADS // SSD_GROUPS):
            hd = g * (SSD_HEADS // SSD_GROUPS) + r
            col = acum[:, hd:hd + 1]
            row = acum_t[hd:hd + 1, :]
            lmat = jnp.exp(jnp.where(causal, col - row, -jnp.inf))
            x_h = xs[:, hd * SSD_HEAD_DIM:(hd + 1) * SSD_HEAD_DIM]
            xdt = x_h * dt[:, hd:hd + 1]
            y = _dot((cb * lmat).astype(BF16), xdt.astype(BF16))
            last = acum[CHUNK - 1:CHUNK, hd:hd + 1]
            decay_end = jnp.exp(last - col)
            new_t = _dot(b_gt, (xdt * decay_end).astype(BF16))
            prev_t = state_ref[hd]
            y = y + _dot(c_g, prev_t.astype(BF16)) * jnp.exp(col)
            state_ref[hd] = prev_t * jnp.exp(last) + new_t
            y = y + d_ref[:, hd:hd + 1] * x_h
            y_ref[:, hd * SSD_HEAD_DIM:(hd + 1) * SSD_HEAD_DIM] = y
    y = y_ref[...] * _silu(z_ref[0].astype(F32))
    o_ref[0] = _rms(y, nw_ref[...]).astype(o_ref.dtype)


def _ssd(z, xbc, dt, conv_w, conv_b, dt_bias, a_log, d_skip, norm_w):
    bsz, seq, _ = z.shape
    nc = seq // CHUNK

    def lane_pad(v):
        return jnp.pad(v.astype(F32).reshape(1, -1), ((0, 0), (0, LANES - v.shape[-1])))

    tri = jnp.asarray(np.tril(np.ones((CHUNK, CHUNK), np.float32)), BF16)
    blk = lambda w: pl.BlockSpec((1, CHUNK, w), lambda b, c: (b, c, 0))
    return pl.pallas_call(
        _ssd_kernel,
        grid=(bsz, nc),
        in_specs=[blk(SSD_WIDTH), blk(SSD_CONV_DIM), blk(LANES),
                  _const_spec((SSD_CONV, SSD_CONV_DIM)), _const_spec((1, SSD_CONV_DIM)),
                  _const_spec((1, LANES)), _const_spec((1, LANES)), _const_spec((1, LANES)),
                  _const_spec((1, SSD_WIDTH)), _const_spec((CHUNK, CHUNK))],
        out_specs=blk(SSD_WIDTH),
        out_shape=jax.ShapeDtypeStruct((bsz, seq, SSD_WIDTH), BF16),
        scratch_shapes=[pltpu.VMEM((CHUNK + CONV_PAD, SSD_CONV_DIM), F32),
                        pltpu.VMEM((SSD_HEADS, SSD_STATE, SSD_HEAD_DIM), F32),
                        pltpu.VMEM((CHUNK, SSD_WIDTH), F32)],
        compiler_params=_params("parallel", "arbitrary"),
        name="ssd",
    )(z, xbc, dt, conv_w.astype(F32), conv_b.astype(F32).reshape(1, -1),
      lane_pad(dt_bias), lane_pad(-jnp.exp(a_log.astype(F32))), lane_pad(d_skip),
      norm_w.astype(F32).reshape(1, -1), tri)


def _retention_tables():
    h = np.arange(RET_HEADS, dtype=np.float64)
    log_gamma = np.log1p(-np.exp2(-5.0 - h))
    idx = np.arange(CHUNK, dtype=np.float64)
    diff = idx[:, None] - idx[None, :]
    dmat = np.where(diff >= 0, np.exp(np.maximum(diff, 0.0)[None] * log_gamma[:, None, None]), 0.0)
    k_decay = np.exp((CHUNK - 1.0 - idx)[:, None] * log_gamma)
    q_decay = np.exp((idx + 1.0)[:, None] * log_gamma)
    c_decay = np.exp(CHUNK * log_gamma)[None, :]
    rep = lambda a: np.repeat(a, RET_HEAD_DIM, axis=1).astype(np.float32)
    avg = np.kron(np.eye(RET_HEADS), np.full((RET_HEAD_DIM, RET_HEAD_DIM), 1.0 / RET_HEAD_DIM))
    return (dmat.astype(np.float32), rep(k_decay * RET_HEAD_DIM ** -0.5), rep(q_decay),
            rep(c_decay), avg.astype(np.float32))


def _rope(x, cos, sin_signed):
    half = RET_HEAD_DIM // 2
    width = x.shape[1]
    fwd = pltpu.roll(x, half, 1)
    bwd = pltpu.roll(x, width - half, 1)
    lane = lax.broadcasted_iota(jnp.int32, x.shape, 1)
    swapped = jnp.where((lane % RET_HEAD_DIM) < half, bwd, fwd)
    return x * cos + swapped * sin_signed


def _retention_kernel(q_ref, k_ref, v_ref, g_ref, cos_ref, sin_ref, dmat_ref, kdec_ref, qdec_ref,
                      cdec_ref, avg_ref, gn_ref, o_ref, state_ref, y_ref):
    c = pl.program_id(1)

    @pl.when(c == 0)
    def _():
        state_ref[...] = jnp.zeros(state_ref.shape, F32)

    reps = RET_WIDTH // LANES
    cos = jnp.tile(cos_ref[0], (1, reps))
    sin = jnp.tile(sin_ref[0], (1, reps))
    q = _rope(q_ref[0].astype(F32), cos, sin)
    k = _rope(k_ref[0].astype(F32), cos, sin)
    v = v_ref[0]
    qb = q.astype(BF16)
    k_t = k.T
    kd_t = (k * kdec_ref[...]).T.astype(BF16)
    k_tb = (k_t * (RET_HEAD_DIM ** -0.5)).astype(BF16)
    qd = qdec_ref[...]
    cd = cdec_ref[...]
    for hd in range(RET_HEADS):
        sl = slice(hd * RET_HEAD_DIM, (hd + 1) * RET_HEAD_DIM)
        q_h = qb[:, sl]
        v_h = v[:, sl]
        scores = _dot(q_h, k_tb[sl, :]) * dmat_ref[hd]
        y = _dot(scores.astype(BF16), v_h)
        prev = state_ref[hd]
        y = y + _dot(q_h, prev.astype(BF16)) * qd[:, sl]
        state_ref[hd] = prev * cd[:, sl] + _dot(kd_t[sl, :], v_h)
        y_ref[:, sl] = y
    y = y_ref[...]
    avg = avg_ref[...]
    mu = _split_dot(y, avg)
    yc = y - mu
    var = _split_dot(yc * yc, avg)
    yn = yc * lax.rsqrt(var + EPS) * gn_ref[...]
    o_ref[0] = (_silu(g_ref[0].astype(F32)) * yn).astype(o_ref.dtype)


def _retention(q, k, v, g, cos2, sin2, gn_w):
    bsz, seq, _ = q.shape
    nc = seq // CHUNK
    dmat, kdec, qdec, cdec, avg = (jnp.asarray(a) for a in _retention_tables())
    avg = avg.astype(BF16)
    blk = lambda w: pl.BlockSpec((1, CHUNK, w), lambda b, c: (b, c, 0))
    return pl.pallas_call(
        _retention_kernel,
        grid=(bsz, nc),
        in_specs=[blk(RET_WIDTH)] * 4 + [blk(LANES)] * 2 + [
            _const_spec(dmat.shape), _const_spec(kdec.shape), _const_spec(qdec.shape),
            _const_spec(cdec.shape), _const_spec(avg.shape), _const_spec((1, RET_WIDTH))],
        out_specs=blk(RET_WIDTH),
        out_shape=jax.ShapeDtypeStruct((bsz, seq, RET_WIDTH), BF16),
        scratch_shapes=[pltpu.VMEM((RET_HEADS, RET_HEAD_DIM, RET_HEAD_DIM), F32),
                        pltpu.VMEM((CHUNK, RET_WIDTH), F32)],
        compiler_params=_params("parallel", "arbitrary"),
        name="retention",
    )(q, k, v, g, cos2, sin2, dmat, kdec, qdec, cdec, avg, gn_w.astype(F32).reshape(1, -1))


def _rope_tables(positions):
    half = RET_HEAD_DIM // 2
    inv_freq = ROPE_BASE ** (-jnp.arange(half, dtype=F32) / half)
    ang = positions.astype(F32)[..., None] * inv_freq
    cos = jnp.cos(ang)
    sin = jnp.sin(ang)
    cos2 = jnp.concatenate([cos, cos, cos, cos], axis=-1)
    sin2 = jnp.concatenate([-sin, sin, -sin, sin], axis=-1)
    return cos2, sin2


def _lru_kernel(x_ref, gate_ref, cw_ref, cb_ref, w_ref, bias_ref, lamc_ref, o_ref, xpad_ref, h_ref):
    c = pl.program_id(1)
    first = c == 0

    @pl.when(first)
    def _():
        h_ref[...] = jnp.zeros(h_ref.shape, F32)

    rows = x_ref.shape[1]
    xc = _causal_conv(xpad_ref, x_ref[0].astype(F32), cw_ref[...], cb_ref[...], first)
    ri = _sigmoid(_dot(xc.astype(BF16), w_ref[...]) + bias_ref[...])
    log_a = lamc_ref[...] * ri[:, :LRU_WIDTH]
    a_all = jnp.exp(log_a)
    mult = jnp.sqrt(jnp.maximum(1.0 - jnp.exp(2.0 * log_a), 0.0))
    b_all = mult * ri[:, LRU_WIDTH:] * xc
    gate = _gelu(gate_ref[0].astype(F32))
    n_steps = int(math.log2(rows))
    row0 = lax.broadcasted_iota(jnp.int32, (rows, LANES), 0) == 0
    for j in range(LRU_WIDTH // LANES):
        sl = slice(j * LANES, (j + 1) * LANES)
        a = a_all[:, sl]
        b = b_all[:, sl]
        b = b + jnp.where(row0, a * h_ref[0:1, sl], 0.0)
        for s in range(n_steps):
            k = 1 << s
            b = b + a * _shift_rows(b, k)
            if s + 1 < n_steps:
                a = a * _shift_rows(a, k, 1.0)
        h_ref[0:1, sl] = b[rows - 1:rows, :]
        o_ref[0, :, sl] = (b * gate[:, sl]).astype(o_ref.dtype)


def _block_diag(w):
    nb, n, _ = w.shape
    eye = jnp.eye(nb, dtype=w.dtype)
    return jnp.einsum('bij,bc->bicj', w, eye).reshape(nb * n, nb * n)


def _lru(x, gate, conv_w, conv_b, wa, ba, wx, bx, lam):
    bsz, seq, _ = x.shape
    rows = min(LRU_ROWS, seq)
    w = jnp.concatenate([_block_diag(wa), _block_diag(wx)], axis=1).astype(BF16)
    bias = jnp.concatenate([ba, bx]).astype(F32).reshape(1, -1)
    lamc = (-LRU_C * jax.nn.softplus(-lam.astype(F32))).reshape(1, -1)
    blk = lambda wd: pl.BlockSpec((1, rows, wd), lambda b, c: (b, c, 0))
    return pl.pallas_call(
        _lru_kernel,
        grid=(bsz, seq // rows),
        in_specs=[blk(LRU_WIDTH), blk(LRU_WIDTH),
                  _const_spec((LRU_CONV, LRU_WIDTH)), _const_spec((1, LRU_WIDTH)),
                  _const_spec((LRU_WIDTH, 2 * LRU_WIDTH)), _const_spec((1, 2 * LRU_WIDTH)),
                  _const_spec((1, LRU_WIDTH))],
        out_specs=blk(LRU_WIDTH),
        out_shape=jax.ShapeDtypeStruct((bsz, seq, LRU_WIDTH), BF16),
        scratch_shapes=[pltpu.VMEM((rows + CONV_PAD, LRU_WIDTH), F32),
                        pltpu.VMEM((SUBLANES, LRU_WIDTH), F32)],
        compiler_params=_params("parallel", "arbitrary"),
        name="rglru",
    )(x, gate, conv_w.astype(F32), conv_b.astype(F32).reshape(1, -1), w, bias, lamc)


def _merge_kernel(h_ref, u_ref, ys5_ref, b1_ref, b2_ref, b3_ref, gl_ref, bg_ref, d_ref, wglu_ref,
                  wb_ref, wo_ref, o_ref, yscr_ref):
    folded = yscr_ref.shape[1] // S5_SUB
    for q in range(S5_Q):
        z = ys5_ref[q].astype(F32) + d_ref[q] * u_ref[q].astype(F32)
        for t in range(S5_SUB):
            yscr_ref[q, pl.ds(t, folded, stride=S5_SUB), :] = z[:, t * LANES:(t + 1) * LANES]
    y = _gelu(jnp.concatenate([yscr_ref[q] for q in range(S5_Q)], axis=1))
    b0 = (y * _sigmoid(_dot(y.astype(BF16), wglu_ref[...]))).astype(BF16)
    branches = (b0, b1_ref[...], b2_ref[...], b3_ref[...])
    merged = None
    for i, br in enumerate(branches):
        sl = slice(i * D_MODEL, (i + 1) * D_MODEL)
        gate = _sigmoid(gl_ref[:, sl].astype(F32) + bg_ref[:, sl])
        term = gate * _dot(br, wb_ref[i])
        merged = term if merged is None else merged + term
    o_ref[...] = h_ref[...] + _dot(merged.astype(BF16), wo_ref[...])


def _merge(h, u, ys5, b1, b2, b3, gate_logits, b_gate, s5_d, w_glu, w_branch, w_out):
    t = h.shape[0]
    tm = min(ROW_TILE, t)
    row = lambda w: pl.BlockSpec((tm, w), lambda i: (i, 0))
    s5_blk = pl.BlockSpec((S5_Q, tm // S5_SUB, S5_ROW), lambda i: (0, i, 0))
    d_fold = jnp.tile(s5_d.astype(F32).reshape(S5_Q, 1, LANES), (1, 1, S5_SUB))
    return pl.pallas_call(
        _merge_kernel,
        grid=(t // tm,),
        in_specs=[row(D_MODEL), s5_blk, s5_blk] + [row(512)] * 3 + [row(N_BRANCH * D_MODEL),
                  _const_spec((1, N_BRANCH * D_MODEL)), _const_spec((S5_Q, 1, S5_ROW)),
                  _const_spec((S5_WIDTH, S5_WIDTH)),
                  _const_spec((N_BRANCH, 512, D_MODEL)), _const_spec((D_MODEL, D_MODEL))],
        out_specs=row(D_MODEL),
        out_shape=jax.ShapeDtypeStruct((t, D_MODEL), F32),
        scratch_shapes=[pltpu.VMEM((S5_Q, tm, LANES), F32)],
        compiler_params=_params("parallel"),
        name="merge",
    )(h, u, ys5, b1, b2, b3, gate_logits, b_gate.astype(F32).reshape(1, -1),
      d_fold, w_glu.astype(BF16), w_branch.astype(BF16), w_out.astype(BF16))


def _kv_kernel(mem_ref, nw_ref, w_ref, k_ref, v_ref):
    mb = _rms(mem_ref[0], nw_ref[...]).astype(BF16)
    for c in range(0, D_MODEL, DOT_COLS):
        k_ref[0, :, c:c + DOT_COLS] = _dot(mb, w_ref[:, c:c + DOT_COLS]).astype(BF16)
        v_ref[0, :, c:c + DOT_COLS] = _dot(
            mb, w_ref[:, D_MODEL + c:D_MODEL + c + DOT_COLS]).astype(BF16)


def _kv(mem, norm_w, wk, wv):
    bsz, n_mem, _ = mem.shape
    w = jnp.concatenate([wk, wv], axis=1).astype(BF16)
    blk = pl.BlockSpec((1, n_mem, D_MODEL), lambda b: (b, 0, 0))
    return pl.pallas_call(
        _kv_kernel,
        grid=(bsz,),
        in_specs=[blk, _const_spec((1, D_MODEL)), _const_spec((D_MODEL, 2 * D_MODEL))],
        out_specs=[blk, blk],
        out_shape=[jax.ShapeDtypeStruct((bsz, n_mem, D_MODEL), BF16)] * 2,
        compiler_params=_params("parallel"),
        name="xattn_kv",
    )(mem, norm_w.astype(F32).reshape(1, -1), w)


def _xattn_kernel(h_ref, nw_ref, wq_ref, k_ref, v_ref, wo_ref, o_ref, att_ref):
    h = h_ref[0]
    xb = _rms(h, nw_ref[...]).astype(BF16)
    scale = XA_HEAD_DIM ** -0.5
    for hd in range(XA_HEADS):
        sl = slice(hd * XA_HEAD_DIM, (hd + 1) * XA_HEAD_DIM)
        q = _dot(xb, wq_ref[:, sl]).astype(BF16)
        s = _dot_nt(q, k_ref[0, :, sl]) * scale
        p = jnp.exp(s - jnp.max(s, axis=-1, keepdims=True))
        denom = jnp.sum(p, axis=-1, keepdims=True)
        att_ref[:, sl] = (_dot(p.astype(BF16), v_ref[0, :, sl]) / denom).astype(BF16)
    o_ref[0] = h + _dot(att_ref[...], wo_ref[...])


def _xattn(h3, norm_w, wq, k, v, wo):
    bsz, seq, _ = h3.shape
    tq = min(ROW_TILE, seq)
    n_mem = k.shape[1]
    row = pl.BlockSpec((1, tq, D_MODEL), lambda b, i: (b, i, 0))
    kv = pl.BlockSpec((1, n_mem, D_MODEL), lambda b, i: (b, 0, 0))
    return pl.pallas_call(
        _xattn_kernel,
        grid=(bsz, seq // tq),
        in_specs=[row, _const_spec((1, D_MODEL)), _const_spec((D_MODEL, D_MODEL)), kv, kv,
                  _const_spec((D_MODEL, D_MODEL))],
        out_specs=row,
        out_shape=jax.ShapeDtypeStruct((bsz, seq, D_MODEL), F32),
        scratch_shapes=[pltpu.VMEM((tq, D_MODEL), BF16)],
        compiler_params=_params("parallel", "parallel"),
        name="xattn",
    )(h3, norm_w.astype(F32).reshape(1, -1), wq.astype(BF16), k, v, wo.astype(BF16))


FF_COLS = 256


def _swiglu_acc(xb, w1_ref, w3_ref, w2_ref, width, lead=()):
    acc = None
    for c in range(0, width, FF_COLS):
        a = _dot(xb, w1_ref[lead + (slice(None), slice(c, c + FF_COLS))])
        b = _dot(xb, w3_ref[lead + (slice(None), slice(c, c + FF_COLS))])
        g = (_silu(a) * b).astype(BF16)
        term = _dot(g, w2_ref[lead + (slice(c, c + FF_COLS), slice(None))])
        acc = term if acc is None else acc + term
    return acc


def _ffn_kernel(h_ref, nw_ref, w1_ref, w3_ref, w2_ref, o_ref):
    h = h_ref[...]
    xb = _rms(h, nw_ref[...]).astype(BF16)
    o_ref[...] = h + _swiglu_acc(xb, w1_ref, w3_ref, w2_ref, D_FF)


def _ffn(h, norm_w, w1, w3, w2):
    t = h.shape[0]
    tm = min(ROW_TILE, t)
    row = pl.BlockSpec((tm, D_MODEL), lambda i: (i, 0))
    return pl.pallas_call(
        _ffn_kernel,
        grid=(t // tm,),
        in_specs=[row, _const_spec((1, D_MODEL)), _const_spec((D_MODEL, D_FF)),
                  _const_spec((D_MODEL, D_FF)), _const_spec((D_FF, D_MODEL))],
        out_specs=row,
        out_shape=jax.ShapeDtypeStruct((t, D_MODEL), F32),
        compiler_params=_params("parallel"),
        name="ffn",
    )(h, norm_w.astype(F32).reshape(1, -1), w1.astype(BF16), w3.astype(BF16), w2.astype(BF16))


MOE_TILE = 512
TOP_K = 2
DISPATCH_ROWS = 512
COMBINE_ROWS = 256
DMA_UNROLL = 8


def _sorted_tiles(t):
    return pl.cdiv(TOP_K * t, MOE_TILE) + N_EXPERTS


def _router_kernel(h_ref, nw_ref, wr_ref, xn_ref, member_ref, wsel_ref, esel_ref):
    xn = _rms(h_ref[...], nw_ref[...])
    xn_ref[...] = xn
    logits = lax.dot_general(xn, wr_ref[...], (((1,), (0,)), ((), ())),
                             precision=lax.Precision.HIGHEST, preferred_element_type=F32)
    lane = lax.broadcasted_iota(jnp.int32, logits.shape, 1)
    logits = jnp.where(lane < N_EXPERTS, logits, -jnp.inf)
    m1 = jnp.max(logits, axis=-1, keepdims=True)
    i1 = jnp.min(jnp.where(logits == m1, lane, LANES), axis=-1, keepdims=True)
    rest = jnp.where(lane == i1, -jnp.inf, logits)
    m2 = jnp.max(rest, axis=-1, keepdims=True)
    i2 = jnp.min(jnp.where(rest == m2, lane, LANES), axis=-1, keepdims=True)
    e2 = jnp.exp(m2 - m1)
    w1 = 1.0 / (1.0 + e2)
    w2 = e2 / (1.0 + e2)
    member_ref[...] = jnp.where(lane == i1, 1.0, jnp.where(lane == i2, 1.0, 0.0)).astype(BF16)
    wsel_ref[...] = jnp.where(lane == 0, w1, jnp.where(lane == 1, w2, 0.0))
    esel_ref[...] = jnp.where(lane == 0, i1, jnp.where(lane == 1, i2, 0))


def _router(h, norm_w, w_router):
    t = h.shape[0]
    tm = min(ROW_TILE, t)
    wr = jnp.pad(w_router.astype(F32), ((0, 0), (0, LANES - N_EXPERTS)))
    row = lambda w: pl.BlockSpec((tm, w), lambda i: (i, 0))
    return pl.pallas_call(
        _router_kernel,
        grid=(t // tm,),
        in_specs=[row(D_MODEL), _const_spec((1, D_MODEL)), _const_spec((D_MODEL, LANES))],
        out_specs=[row(D_MODEL), row(LANES), row(LANES), row(LANES)],
        out_shape=[jax.ShapeDtypeStruct((t, D_MODEL), F32), jax.ShapeDtypeStruct((t, LANES), BF16),
                   jax.ShapeDtypeStruct((t, LANES), F32), jax.ShapeDtypeStruct((t, LANES), jnp.int32)],
        compiler_params=_params("parallel"),
        name="moe_router",
    )(h, norm_w.astype(F32).reshape(1, -1), wr)


def _positions_kernel(member_ref, esel_ref, pos_ref, meta_ref, cnt_ref, carry_ref, off_ref):
    phase = pl.program_id(0)
    i = pl.program_id(1)
    m = member_ref[...]
    tp = m.shape[0]
    col_sum = _dot(jnp.ones((SUBLANES, tp), BF16), m)

    @pl.when((phase == 0) & (i == 0))
    def _():
        cnt_ref[...] = jnp.zeros(cnt_ref.shape, F32)

    @pl.when(phase == 0)
    def _():
        cnt_ref[...] += col_sum

    @pl.when((phase == 1) & (i == 0))
    def _():
        tiles = jnp.floor((cnt_ref[...] + (MOE_TILE - 1.0)) * (1.0 / MOE_TILE))
        r = lax.broadcasted_iota(jnp.int32, (LANES, LANES), 0)
        c = lax.broadcasted_iota(jnp.int32, (LANES, LANES), 1)
        before = jnp.where(r < c, 1.0, 0.0).astype(BF16)
        first_tile = _dot(tiles.astype(BF16), before)
        off_ref[...] = first_tile * MOE_TILE
        carry_ref[...] = jnp.zeros(carry_ref.shape, F32)
        row = lax.broadcasted_iota(jnp.int32, (SUBLANES, LANES), 0)
        meta_ref[...] = jnp.where(row == 0, first_tile, tiles).astype(jnp.int32)

    @pl.when(phase == 1)
    def _():
        r = lax.broadcasted_iota(jnp.int32, (tp, tp), 0)
        c = lax.broadcasted_iota(jnp.int32, (tp, tp), 1)
        earlier = jnp.where(r > c, 1.0, 0.0).astype(BF16)
        posm = off_ref[0:1, :] + carry_ref[0:1, :] + _dot(earlier, m)
        lane = lax.broadcasted_iota(jnp.int32, posm.shape, 1)
        e = esel_ref[...]
        p0 = jnp.sum(jnp.where(lane == e[:, 0:1], posm, 0.0), axis=-1, keepdims=True)
        p1 = jnp.sum(jnp.where(lane == e[:, 1:2], posm, 0.0), axis=-1, keepdims=True)
        pos_ref[...] = jnp.where(lane == 0, p0, jnp.where(lane == 1, p1, 0.0)).astype(jnp.int32)
        carry_ref[...] += col_sum


def _positions(member, esel):
    t = member.shape[0]
    tp = min(ROW_TILE, t)
    return pl.pallas_call(
        _positions_kernel,
        grid=(2, t // tp),
        in_specs=[pl.BlockSpec((tp, LANES), lambda p, i: (i, 0)),
                  pl.BlockSpec((tp, LANES), lambda p, i: (i * p, 0))],
        out_specs=[pl.BlockSpec((tp, LANES), lambda p, i: (i * p, 0)),
                   pl.BlockSpec((SUBLANES, LANES), lambda p, i: (0, 0))],
        out_shape=[jax.ShapeDtypeStruct((t, LANES), jnp.int32),
                   jax.ShapeDtypeStruct((SUBLANES, LANES), jnp.int32)],
        scratch_shapes=[pltpu.VMEM((SUBLANES, LANES), F32)] * 3,
        compiler_params=_params("arbitrary", "arbitrary"),
        name="moe_positions",
    )(member, esel)


def _row_copy(src_ref, src_row, dst_ref, dst_row, sem):
    return pltpu.make_async_copy(src_ref.at[pl.ds(src_row, 1)], dst_ref.at[pl.ds(dst_row, 1)], sem)


def _dispatch_kernel(pos_ref, xn_ref, xs_in_ref, xs_ref, sem):
    del xs_in_ref
    rows = xn_ref.shape[0]

    def copies(r):
        return [_row_copy(xn_ref, r, xs_ref, pos_ref[0, k, r], sem) for k in range(TOP_K)]

    def issue(r, carry):
        for k, cp in enumerate(copies(r)):
            cp.start(priority=k)
        return carry

    def drain(r, carry):
        for cp in copies(r):
            cp.wait()
        return carry

    lax.fori_loop(0, rows, issue, 0, unroll=DMA_UNROLL)
    lax.fori_loop(0, rows, drain, 0, unroll=DMA_UNROLL)


def _slot_major(pos, rows):
    t = pos.shape[0]
    return jnp.transpose(pos[:, :TOP_K].reshape(t // rows, rows, TOP_K), (0, 2, 1))


def _dispatch(xn, pos, n_sorted):
    t = xn.shape[0]
    rows = min(DISPATCH_ROWS, t)
    xs0 = jnp.zeros((n_sorted, D_MODEL), F32)
    return pl.pallas_call(
        _dispatch_kernel,
        grid=(t // rows,),
        in_specs=[pl.BlockSpec((1, TOP_K, rows), lambda i: (i, 0, 0), memory_space=pltpu.SMEM),
                  pl.BlockSpec((rows, D_MODEL), lambda i: (i, 0)),
                  pl.BlockSpec(memory_space=pl.ANY)],
        out_specs=pl.BlockSpec(memory_space=pl.ANY),
        out_shape=jax.ShapeDtypeStruct((n_sorted, D_MODEL), F32),
        scratch_shapes=[pltpu.SemaphoreType.DMA(())],
        input_output_aliases={2: 0},
        compiler_params=_params("arbitrary"),
        name="moe_dispatch",
    )(_slot_major(pos, rows), xn, xs0)


def _experts_kernel(tile_expert_ref, n_used_ref, xs_ref, w1_ref, w3_ref, w2_ref, y_ref):
    del tile_expert_ref
    used = pl.program_id(0) < n_used_ref[0]

    @pl.when(used)
    def _():
        xb = xs_ref[...].astype(BF16)
        y_ref[...] = _swiglu_acc(xb, w1_ref, w3_ref, w2_ref, D_FF_EXPERT, lead=(0,))

    @pl.when(jnp.logical_not(used))
    def _():
        y_ref[...] = jnp.zeros(y_ref.shape, F32)


def _experts(xs, tile_expert, n_used, w1, w3, w2):
    n_tiles = xs.shape[0] // MOE_TILE
    row = pl.BlockSpec((MOE_TILE, D_MODEL), lambda i, te, nu: (i, 0))
    wspec = lambda shape: pl.BlockSpec((1,) + shape, lambda i, te, nu: (te[i], 0, 0),
                                       pipeline_mode=pl.Buffered(1))
    return pl.pallas_call(
        _experts_kernel,
        grid_spec=pltpu.PrefetchScalarGridSpec(
            num_scalar_prefetch=2,
            grid=(n_tiles,),
            in_specs=[row, wspec((D_MODEL, D_FF_EXPERT)), wspec((D_MODEL, D_FF_EXPERT)),
                      wspec((D_FF_EXPERT, D_MODEL))],
            out_specs=row),
        out_shape=jax.ShapeDtypeStruct(xs.shape, F32),
        compiler_params=_params("arbitrary"),
        name="moe_experts",
    )(tile_expert, n_used, xs, w1.astype(BF16), w3.astype(BF16), w2.astype(BF16))


def _combine_kernel(pos_ref, h_ref, wsel_ref, fw_ref, y_ref, o_ref, ybuf_ref, sem, *, final_norm):
    rows = h_ref.shape[0]

    def copies(r):
        return [_row_copy(y_ref, pos_ref[0, k, r], ybuf_ref.at[k], r, sem) for k in range(TOP_K)]

    def issue(r, carry):
        for k, cp in enumerate(copies(r)):
            cp.start(priority=k)
        return carry

    def drain(r, carry):
        for cp in copies(r):
            cp.wait()
        return carry

    lax.fori_loop(0, rows, issue, 0, unroll=DMA_UNROLL)
    lax.fori_loop(0, rows, drain, 0, unroll=DMA_UNROLL)
    w = wsel_ref[...]
    out = h_ref[...] + w[:, 0:1] * ybuf_ref[0] + w[:, 1:2] * ybuf_ref[1]
    if final_norm:
        out = _rms(out, fw_ref[...])
    o_ref[...] = out


def _combine(h, wsel, pos, y_sorted, final_w, final_norm):
    t = h.shape[0]
    rows = min(COMBINE_ROWS, t)
    return pl.pallas_call(
        functools.partial(_combine_kernel, final_norm=final_norm),
        grid=(t // rows,),
        in_specs=[pl.BlockSpec((1, TOP_K, rows), lambda i: (i, 0, 0), memory_space=pltpu.SMEM),
                  pl.BlockSpec((rows, D_MODEL), lambda i: (i, 0)),
                  pl.BlockSpec((rows, LANES), lambda i: (i, 0)),
                  _const_spec((1, D_MODEL)),
                  pl.BlockSpec(memory_space=pl.ANY)],
        out_specs=pl.BlockSpec((rows, D_MODEL), lambda i: (i, 0)),
        out_shape=jax.ShapeDtypeStruct((t, D_MODEL), F32),
        scratch_shapes=[pltpu.VMEM((TOP_K, rows, D_MODEL), F32), pltpu.SemaphoreType.DMA(())],
        compiler_params=_params("arbitrary"),
        name="moe_combine",
    )(_slot_major(pos, rows), h, wsel, final_w.astype(F32).reshape(1, -1), y_sorted)


def _moe(h, norm_w, w_router, w1, w3, w2, final_w, final_norm):
    t = h.shape[0]
    n_tiles = _sorted_tiles(t)
    xn, member, wsel, esel = _router(h, norm_w, w_router)
    pos, meta = _positions(member, esel)
    first_tile = meta[0, :N_EXPERTS]
    last_tile = first_tile + meta[1, :N_EXPERTS]
    tile = jnp.arange(n_tiles, dtype=jnp.int32)
    tile_expert = jnp.minimum(jnp.sum(tile[:, None] >= last_tile[None, :], axis=1), N_EXPERTS - 1)
    xs = _dispatch(xn, pos, n_tiles * MOE_TILE)
    ys = _experts(xs, tile_expert.astype(jnp.int32), last_tile[N_EXPERTS - 1:], w1, w3, w2)
    return _combine(h, wsel, pos, ys, final_w, final_norm)


def _final_norm_kernel(h_ref, w_ref, o_ref):
    o_ref[...] = _rms(h_ref[...], w_ref[...])


def _final_norm(h, w):
    t = h.shape[0]
    tm = min(ROW_TILE, t)
    row = pl.BlockSpec((tm, D_MODEL), lambda i: (i, 0))
    return pl.pallas_call(
        _final_norm_kernel, grid=(t // tm,), in_specs=[row, _const_spec((1, D_MODEL))],
        out_specs=row, out_shape=jax.ShapeDtypeStruct((t, D_MODEL), F32),
        compiler_params=_params("parallel"), name="final_norm",
    )(h, w.astype(F32).reshape(1, -1))


def _mixing_block(h, bsz, seq, cos2, sin2, norm_w, w_in, b_gate,
                  s5_lam_re, s5_lam_im, s5_b_re, s5_b_im, s5_c_re, s5_c_im, s5_d, s5_log_dt, s5_w_glu,
                  ssd_conv_w, ssd_conv_b, ssd_dt_bias, ssd_a_log, ssd_d, ssd_norm,
                  ret_norm,
                  lru_conv_w, lru_conv_b, lru_wa, lru_ba, lru_wx, lru_bx, lru_lam,
                  w_branch, w_out):
    (u_s5, z_ssd, xbc_ssd, dt_ssd, q_ret, k_ret, v_ret, g_ret, x_lru, gate_lru,
     gate_logits) = _inproj(h, norm_w, _pack_w_in(w_in))
    seq3 = lambda a: a.reshape(bsz, seq, a.shape[-1])
    tables = _s5_tables(s5_lam_re, s5_lam_im, s5_b_re, s5_b_im, s5_c_re, s5_c_im, s5_log_dt,
                        seq // S5_SUB)
    y_s5 = _s5_scan(u_s5, tables, bsz, seq)
    y_ssd = _ssd(seq3(z_ssd), seq3(xbc_ssd), seq3(dt_ssd), ssd_conv_w, ssd_conv_b, ssd_dt_bias,
                 ssd_a_log, ssd_d, ssd_norm)
    y_ret = _retention(seq3(q_ret), seq3(k_ret), seq3(v_ret), seq3(g_ret), cos2, sin2, ret_norm)
    y_lru = _lru(seq3(x_lru), seq3(gate_lru), lru_conv_w, lru_conv_b, lru_wa, lru_ba, lru_wx,
                 lru_bx, lru_lam)
    flat = lambda a: a.reshape(bsz * seq, a.shape[-1])
    return _merge(h, u_s5, y_s5, flat(y_ssd), flat(y_ret), flat(y_lru), gate_logits,
                  b_gate, s5_d, s5_w_glu, w_branch, w_out)


def kernel(x, mem, positions, norm_mix, w_in, b_gate, s5_lam_re, s5_lam_im, s5_b_re, s5_b_im, s5_c_re, s5_c_im, s5_d, s5_log_dt, s5_w_glu, ssd_conv_w, ssd_conv_b, ssd_dt_bias, ssd_a_log, ssd_d, ssd_norm, ret_norm, lru_conv_w, lru_conv_b, lru_wa, lru_ba, lru_wx, lru_bx, lru_lam, w_branch, w_out, norm_xa, norm_mem, xa_wq, xa_wk, xa_wv, xa_wo, norm_ffn, ffn_w1, ffn_w3, ffn_w2, moe_router, moe_w1, moe_w3, moe_w2, norm_final):
    bsz, seq, _ = x.shape
    depth = norm_mix.shape[0]
    cos2, sin2 = _rope_tables(positions)
    h = x.reshape(bsz * seq, D_MODEL)
    for i in range(depth):
        h = _mixing_block(h, bsz, seq, cos2, sin2, norm_mix[i], w_in[i], b_gate[i],
                          s5_lam_re[i], s5_lam_im[i], s5_b_re[i], s5_b_im[i], s5_c_re[i], s5_c_im[i],
                          s5_d[i], s5_log_dt[i], s5_w_glu[i],
                          ssd_conv_w[i], ssd_conv_b[i], ssd_dt_bias[i], ssd_a_log[i], ssd_d[i],
                          ssd_norm[i], ret_norm[i],
                          lru_conv_w[i], lru_conv_b[i], lru_wa[i], lru_ba[i], lru_wx[i], lru_bx[i],
                          lru_lam[i], w_branch[i], w_out[i])
        k, v = _kv(mem, norm_mem[i], xa_wk[i], xa_wv[i])
        h = _xattn(h.reshape(bsz, seq, D_MODEL), norm_xa[i], xa_wq[i], k, v, xa_wo[i])
        h = h.reshape(bsz * seq, D_MODEL)
        last = i == depth - 1
        if i % 2 == 0:
            h = _ffn(h, norm_ffn[i], ffn_w1[i // 2], ffn_w3[i // 2], ffn_w2[i // 2])
            if last:
                h = _final_norm(h, norm_final)
        else:
            h = _moe(h, norm_ffn[i], moe_router[i // 2], moe_w1[i // 2], moe_w3[i // 2],
                     moe_w2[i // 2], norm_final, last)
    return h.reshape(bsz, seq, D_MODEL)
```

```python
import functools
import math

import numpy as np
import jax
import jax.numpy as jnp
from jax import lax
from jax.experimental import pallas as pl
from jax.experimental.pallas import tpu as pltpu

F32 = jnp.float32
BF16 = jnp.bfloat16

D_MODEL = 1024
N_MEM = 256
EPS = 1e-6
CHUNK = 128
N_BRANCH = 4
S5_WIDTH = 512
S5_GROUP = 16
S5_GROUPS = 32
S5_STATE = 64
S5_SUB = 16
SSD_HEADS = 8
SSD_HEAD_DIM = 64
SSD_WIDTH = 512
SSD_GROUPS = 2
SSD_STATE = 64
SSD_CONV = 4
SSD_CONV_DIM = SSD_WIDTH + 2 * SSD_GROUPS * SSD_STATE
RET_HEADS = 8
RET_HEAD_DIM = 64
RET_WIDTH = 512
ROPE_BASE = 10000.0
LRU_WIDTH = 512
LRU_BLOCKS = 8
LRU_BLOCK = 64
LRU_CONV = 4
LRU_C = 8.0
XA_HEADS = 4
XA_HEAD_DIM = 256
D_FF = 2816
N_EXPERTS = 8
D_FF_EXPERT = 3584
SECTION_WIDTHS = (S5_WIDTH, SSD_WIDTH, SSD_CONV_DIM, SSD_HEADS,
                  RET_WIDTH, RET_WIDTH, RET_WIDTH, RET_WIDTH,
                  LRU_WIDTH, LRU_WIDTH, N_BRANCH * D_MODEL)

LANES = 128
SUBLANES = 8
VMEM_LIMIT = 56 * 1024 * 1024
ROW_TILE = 512
LRU_ROWS = 256


def _params(*sem):
    return pltpu.CompilerParams(dimension_semantics=sem, vmem_limit_bytes=VMEM_LIMIT)


def _const_spec(shape):
    nd = len(shape)
    return pl.BlockSpec(shape, lambda *_: (0,) * nd)


def _rms(x, w):
    return x * lax.rsqrt(jnp.mean(x * x, axis=-1, keepdims=True) + EPS) * w


def _sigmoid(x):
    return 0.5 + 0.5 * jnp.tanh(0.5 * x)


def _silu(x):
    return x * _sigmoid(x)


def _gelu(x):
    return jax.nn.gelu(x)


def _softplus(x):
    return jnp.maximum(x, 0.0) + jnp.log(1.0 + jnp.exp(-jnp.abs(x)))


def _dot(a, b):
    return jnp.dot(a, b, preferred_element_type=F32)


def _dot_nt(a, b):
    return lax.dot_general(a, b, (((1,), (1,)), ((), ())), preferred_element_type=F32)


def _split3(x):
    hi = x.astype(BF16)
    r1 = x - hi.astype(F32)
    mid = r1.astype(BF16)
    lo = (r1 - mid.astype(F32)).astype(BF16)
    return hi, mid, lo


def _split_dot(x, m_bf16):
    hi, mid, lo = _split3(x)
    return _dot(hi, m_bf16) + _dot(mid, m_bf16) + _dot(lo, m_bf16)


def _split_dot_left(m_bf16, x):
    hi, mid, lo = _split3(x)
    return _dot(m_bf16, hi) + _dot(m_bf16, mid) + _dot(m_bf16, lo)


def _shift_rows(x, k, fill=0.0):
    rows = x.shape[0]
    if k % SUBLANES == 0:
        return jnp.concatenate([jnp.full((k, x.shape[1]), fill, x.dtype), x[:rows - k]], axis=0)
    rolled = pltpu.roll(x, k, 0)
    row = lax.broadcasted_iota(jnp.int32, x.shape, 0)
    return jnp.where(row >= k, rolled, fill)


IN_WIDTHS = (S5_WIDTH, SSD_WIDTH, SSD_CONV_DIM, LANES,
             RET_WIDTH, RET_WIDTH, RET_WIDTH, RET_WIDTH,
             LRU_WIDTH, LRU_WIDTH)
IN_TOTAL_PADDED = sum(IN_WIDTHS)
DOT_COLS = 512


def _inproj_kernel(h_ref, nw_ref, w_ref, u_ref, *refs):
    out_refs, uscr_ref = refs[:-1], refs[-1]
    xb = _rms(h_ref[...], nw_ref[...]).astype(BF16)
    u = _dot(xb, w_ref[:, :S5_WIDTH])
    folded = u.shape[0] // S5_SUB
    for q in range(S5_Q):
        uscr_ref[q] = u[:, q * LANES:(q + 1) * LANES]
        for t in range(S5_SUB):
            u_ref[q, :, t * LANES:(t + 1) * LANES] = uscr_ref[
                q, pl.ds(t, folded, stride=S5_SUB), :].astype(u_ref.dtype)
    off = S5_WIDTH
    for o_ref, width in zip(out_refs, IN_WIDTHS[1:]):
        for c in range(0, width, DOT_COLS):
            n = min(DOT_COLS, width - c)
            o_ref[:, c:c + n] = _dot(xb, w_ref[:, off + c:off + c + n]).astype(o_ref.dtype)
        off += width


def _inproj(h, norm_w, w_cat):
    t = h.shape[0]
    tm = min(ROW_TILE, t)
    out_shape = ([jax.ShapeDtypeStruct((S5_Q, t // S5_SUB, S5_ROW), BF16)]
                 + [jax.ShapeDtypeStruct((t, w), BF16) for w in IN_WIDTHS[1:]])
    out_specs = ([pl.BlockSpec((S5_Q, tm // S5_SUB, S5_ROW), lambda i: (0, i, 0))]
                 + [pl.BlockSpec((tm, w), lambda i: (i, 0)) for w in IN_WIDTHS[1:]])
    return pl.pallas_call(
        _inproj_kernel,
        grid=(t // tm,),
        in_specs=[pl.BlockSpec((tm, D_MODEL), lambda i: (i, 0)),
                  _const_spec((1, D_MODEL)),
                  _const_spec((D_MODEL, IN_TOTAL_PADDED))],
        out_specs=out_specs,
        out_shape=out_shape,
        scratch_shapes=[pltpu.VMEM((S5_Q, tm, LANES), F32)],
        compiler_params=_params("parallel"),
        name="inproj",
    )(h, norm_w.reshape(1, D_MODEL), w_cat)


def _pack_w_in(w_in):
    pieces = []
    off = 0
    for width in SECTION_WIDTHS[:-1]:
        sec = w_in[:, off:off + width]
        if width == SSD_HEADS:
            sec = jnp.pad(sec, ((0, 0), (0, LANES - width)))
        pieces.append(sec)
        off += width
    w_gate = (0.5 * w_in[:, off:]).astype(BF16)
    return jnp.concatenate(pieces, axis=1).astype(BF16), w_gate


S5_Q = S5_WIDTH // LANES
S5_QG = S5_GROUPS // S5_Q
S5_QS = S5_QG * S5_STATE
S5_ROW = S5_SUB * LANES
S5_TILE = 256
S5_SEQS = 4


def _s5_expand_tables(wc_ref, xc_ref, rin_ref, rout_ref, win_ref, wx_ref):
    group_shift = int(math.log2(S5_GROUP))
    state_shift = int(math.log2(S5_STATE))
    for c in range(0, 2 * S5_QS, DOT_COLS):
        r = lax.broadcasted_iota(jnp.int32, (S5_ROW, DOT_COLS), 0)
        col = lax.broadcasted_iota(jnp.int32, (S5_ROW, DOT_COLS), 1) + c
        own = ((r >> group_shift) & (S5_QG - 1)) == ((col & (S5_QS - 1)) >> state_shift)
        rep = _dot(wc_ref[0], rin_ref[:, c:c + DOT_COLS])
        win_ref[:, c:c + DOT_COLS] = jnp.where(own, rep, 0.0).astype(BF16)
    for c in range(0, S5_ROW, DOT_COLS):
        r = lax.broadcasted_iota(jnp.int32, (2 * S5_QS, DOT_COLS), 0)
        col = lax.broadcasted_iota(jnp.int32, (2 * S5_QS, DOT_COLS), 1) + c
        own = ((r & (S5_QS - 1)) >> state_shift) == ((col >> group_shift) & (S5_QG - 1))
        rep = _dot(xc_ref[0], rout_ref[:, c:c + DOT_COLS])
        wx_ref[:, c:c + DOT_COLS] = jnp.where(own, rep, 0.0).astype(BF16)


def _s5_kernel(u_ref, toe_ref, wc_ref, xc_ref, rin_ref, rout_ref, pw_ref, y_ref,
               win_ref, wx_ref, x_ref, sp_ref, *, rows):
    @pl.when(pl.program_id(1) == 0)
    def _():
        _s5_expand_tables(wc_ref, xc_ref, rin_ref, rout_ref, win_ref, wx_ref)

    n_steps = int(math.log2(rows))
    for c in range(0, 2 * S5_QS, DOT_COLS):
        x_ref[:, c:c + DOT_COLS] = _dot(u_ref[0], win_ref[:, c:c + DOT_COLS])

    def seq(b, carry):
        r0 = pl.multiple_of(b * rows, rows)
        for lc in range(S5_QS // LANES):
            re = slice(lc * LANES, (lc + 1) * LANES)
            im = slice(S5_QS + lc * LANES, S5_QS + (lc + 1) * LANES)
            sr = x_ref[pl.ds(r0, rows), re]
            si = x_ref[pl.ds(r0, rows), im]
            for j in range(n_steps):
                k = 1 << j
                pr = pw_ref[0, 2 * j:2 * j + 1, re]
                pi = pw_ref[0, 2 * j + 1:2 * j + 2, re]
                shr = _shift_rows(sr, k)
                shi = _shift_rows(si, k)
                sr, si = sr + pr * shr - pi * shi, si + pr * shi + pi * shr
            sp_ref[pl.ds(r0, rows), re] = _shift_rows(sr, 1).astype(BF16)
            sp_ref[pl.ds(r0, rows), im] = _shift_rows(si, 1).astype(BF16)
        return carry

    lax.fori_loop(0, u_ref.shape[1] // rows, seq, 0)
    for nn in range(S5_ROW // S5_TILE):
        cols = slice(nn * S5_TILE, (nn + 1) * S5_TILE)
        acc = _dot(sp_ref[...], wx_ref[:, cols])
        for kk in range(nn + 1):
            acc = acc + _dot(u_ref[0, :, kk * S5_TILE:(kk + 1) * S5_TILE], toe_ref[0, nn - kk])
        y_ref[0, :, cols] = acc.astype(y_ref.dtype)


def _s5_tables(lam_re, lam_im, b_re, b_im, c_re, c_im, log_dt, rows):
    lr = lam_re.astype(F32)
    li = lam_im.astype(F32)
    step = jnp.exp(log_dt.astype(F32))[:, None]
    mag = jnp.exp(lr * step)
    ar = mag * jnp.cos(li * step)
    ai = mag * jnp.sin(li * step)
    inv = 1.0 / (lr * lr + li * li)
    cr = ((ar - 1.0) * lr + ai * li) * inv
    ci = (ai * lr - (ar - 1.0) * li) * inv
    bbr = cr[..., None] * b_re - ci[..., None] * b_im
    bbi = cr[..., None] * b_im + ci[..., None] * b_re

    def apow(e):
        e = jnp.asarray(e, F32)[:, None, None]
        m = jnp.exp(lr * step * e)
        return m * jnp.cos(li * step * e), m * jnp.sin(li * step * e)

    sub = S5_SUB
    pr, pi = apow(np.arange(sub + 1))
    m_r = pr[:sub, :, :, None] * bbr - pi[:sub, :, :, None] * bbi
    m_i = pr[:sub, :, :, None] * bbi + pi[:sub, :, :, None] * bbr
    kern = (jnp.einsum('gon,tgni->tgoi', c_re, m_r)
            - jnp.einsum('gon,tgni->tgoi', c_im, m_i))
    eye = jnp.eye(S5_QG, dtype=F32)
    blk = lambda a: a.reshape(a.shape[0], S5_Q, S5_QG, *a.shape[2:])
    kern_bd = jnp.einsum('tqgoi,gh->tqgiho', blk(kern), eye).reshape(sub, S5_Q, LANES, LANES)
    kern_bd = jnp.concatenate([jnp.zeros_like(kern_bd[:1]), kern_bd], axis=0)
    steps_per_tile = S5_TILE // LANES
    d = np.arange(sub // steps_per_tile)[:, None, None]
    tau = steps_per_tile * d + np.arange(steps_per_tile)[None, None, :] - np.arange(steps_per_tile)[None, :, None]
    toe = kern_bd[tau + 1]
    toe = jnp.transpose(toe, (3, 0, 1, 4, 2, 5)).reshape(S5_Q, sub // steps_per_tile, S5_TILE, S5_TILE)
    er, ei = pr[sub - 1 - np.arange(sub)], pi[sub - 1 - np.arange(sub)]
    w_r = er[..., None] * bbr - ei[..., None] * bbi
    w_i = er[..., None] * bbi + ei[..., None] * bbr
    fold_in = lambda w: jnp.transpose(blk(w), (1, 0, 2, 4, 3)).reshape(S5_Q, S5_ROW, S5_STATE)
    wc = jnp.concatenate([fold_in(w_r), fold_in(w_i)], axis=2)
    qr, qi = pr[1:], pi[1:]
    x_r = c_re[None] * qr[:, :, None, :] - c_im[None] * qi[:, :, None, :]
    x_i = -(c_re[None] * qi[:, :, None, :] + c_im[None] * qr[:, :, None, :])
    fold_out = lambda w: jnp.transpose(blk(w), (1, 2, 4, 0, 3)).reshape(S5_Q, S5_QS, sub * S5_GROUP)
    xc = jnp.concatenate([fold_out(x_r), fold_out(x_i)], axis=1)
    n_steps = int(math.log2(rows))
    sr_, si_ = apow(sub * (2.0 ** np.arange(n_steps)))
    pw = jnp.stack([sr_, si_], axis=1).reshape(2 * n_steps, S5_Q, S5_QS)
    pw = jnp.transpose(pw, (1, 0, 2))
    pw = jnp.pad(pw, ((0, 0), (0, 2 * SUBLANES - 2 * n_steps), (0, 0)))
    return toe.astype(BF16), wc.astype(BF16), xc.astype(BF16), pw


def _s5_replicators():
    k = np.arange(2 * S5_STATE)[:, None]
    c = np.arange(2 * S5_QS)[None, :]
    rin = ((k // S5_STATE) == (c // S5_QS)) & ((k % S5_STATE) == (c % S5_STATE))
    r = np.arange(S5_SUB * S5_GROUP)[:, None]
    c = np.arange(S5_ROW)[None, :]
    rout = ((r // S5_GROUP) == (c // LANES)) & ((r % S5_GROUP) == (c % S5_GROUP))
    return jnp.asarray(rin, BF16), jnp.asarray(rout, BF16)


def _s5_scan(u5, tables, bsz, seq):
    toe, wc, xc, pw = tables
    rin, rout = _s5_replicators()
    rows = seq // S5_SUB
    nb = math.gcd(S5_SEQS, bsz)
    row_blk = pl.BlockSpec((1, nb * rows, S5_ROW), lambda q, j: (q, j, 0))
    per_q = lambda a: pl.BlockSpec((1,) + a.shape[1:], lambda q, j: (q,) + (0,) * (a.ndim - 1))
    const = lambda a: pl.BlockSpec(a.shape, lambda q, j: (0,) * a.ndim)
    return pl.pallas_call(
        functools.partial(_s5_kernel, rows=rows),
        grid=(S5_Q, bsz // nb),
        in_specs=[row_blk, per_q(toe), per_q(wc), per_q(xc), const(rin), const(rout), per_q(pw)],
        out_specs=row_blk,
        out_shape=jax.ShapeDtypeStruct(u5.shape, BF16),
        scratch_shapes=[pltpu.VMEM((S5_ROW, 2 * S5_QS), BF16),
                        pltpu.VMEM((2 * S5_QS, S5_ROW), BF16),
                        pltpu.VMEM((nb * rows, 2 * S5_QS), F32),
                        pltpu.VMEM((nb * rows, 2 * S5_QS), BF16)],
        compiler_params=_params("parallel", "arbitrary"),
        name="s5_scan",
    )(u5, toe, wc, xc, rin, rout, pw)


CONV_PAD = SUBLANES
SEQS_PER_STEP = 2


def _causal_conv(xpad_ref, x, w, b):
    rows = x.shape[0]
    xpad_ref[CONV_PAD:CONV_PAD + rows, :] = x
    k = w.shape[0]
    acc = b
    for j in range(k):
        s = CONV_PAD - (k - 1) + j
        acc = acc + w[j:j + 1, :] * xpad_ref[s:s + rows, :]
    xpad_ref[0:CONV_PAD, :] = xpad_ref[rows:rows + CONV_PAD, :]
    return acc


def _ssd_kernel(z_ref, xbc_ref, dt_ref, cw_ref, cb_ref, dtb_ref, a_ref, d_ref, nw_ref, tri_ref,
                o_ref, xpad_ref, state_ref, y_ref):
    @pl.when(pl.program_id(1) == 0)
    def _():
        state_ref[...] = jnp.zeros(state_ref.shape, F32)
        xpad_ref[:, 0:CONV_PAD, :] = jnp.zeros((xpad_ref.shape[0], CONV_PAD, xpad_ref.shape[2]), F32)

    for s in range(z_ref.shape[0]):
        _ssd_chunk(z_ref.at[s], xbc_ref.at[s], dt_ref.at[s], cw_ref, cb_ref, dtb_ref, a_ref, d_ref,
                   nw_ref, tri_ref, o_ref.at[s], xpad_ref.at[s], state_ref.at[s], y_ref.at[s])


def _ssd_chunk(z_ref, xbc_ref, dt_ref, cw_ref, cb_ref, dtb_ref, a_ref, d_ref, nw_ref, tri_ref,
               o_ref, xpad_ref, state_ref, y_ref):
    xbc = _causal_conv(xpad_ref, xbc_ref[...].astype(F32), cw_ref[...], cb_ref[...])
    xbc = _silu(xbc)
    xs = xbc[:, :SSD_WIDTH]
    bs = xbc[:, SSD_WIDTH:SSD_WIDTH + LANES]
    cs = xbc[:, SSD_WIDTH + LANES:]
    bs_t = bs.T.astype(BF16)
    bs_b = bs.astype(BF16)
    cs_b = cs.astype(BF16)
    dt = _softplus(dt_ref[...].astype(F32) + dtb_ref[...])
    da = dt * a_ref[...]
    acum = _split_dot_left(tri_ref[...], da)
    acum_t = acum.T
    causal = (lax.broadcasted_iota(jnp.int32, (CHUNK, CHUNK), 0)
              >= lax.broadcasted_iota(jnp.int32, (CHUNK, CHUNK), 1))
    for g in range(SSD_GROUPS):
        b_g = bs_b[:, g * SSD_STATE:(g + 1) * SSD_STATE]
        c_g = cs_b[:, g * SSD_STATE:(g + 1) * SSD_STATE]
        b_gt = bs_t[g * SSD_STATE:(g + 1) * SSD_STATE, :]
        cb = _dot_nt(c_g, b_g)
        for r in range(SSD_HEADS // SSD_GROUPS):
            hd = g * (SSD_HEADS // SSD_GROUPS) + r
            col = acum[:, hd:hd + 1]
            row = acum_t[hd:hd + 1, :]
            lmat = jnp.exp(jnp.where(causal, col - row, -jnp.inf))
            x_h = xs[:, hd * SSD_HEAD_DIM:(hd + 1) * SSD_HEAD_DIM]
            xdt = x_h * dt[:, hd:hd + 1]
            y = _dot((cb * lmat).astype(BF16), xdt.astype(BF16))
            last = acum[CHUNK - 1:CHUNK, hd:hd + 1]
            decay_end = jnp.exp(last - col)
            new_t = _dot(b_gt, (xdt * decay_end).astype(BF16))
            prev_t = state_ref[hd]
            y = y + _dot(c_g, prev_t.astype(BF16)) * jnp.exp(col)
            state_ref[hd] = prev_t * jnp.exp(last) + new_t
            y = y + d_ref[:, hd:hd + 1] * x_h
            y_ref[:, hd * SSD_HEAD_DIM:(hd + 1) * SSD_HEAD_DIM] = y
    y = y_ref[...] * _silu(z_ref[...].astype(F32))
    o_ref[...] = _rms(y, nw_ref[...]).astype(o_ref.dtype)


def _ssd(z, xbc, dt, conv_w, conv_b, dt_bias, a_log, d_skip, norm_w):
    bsz, seq, _ = z.shape
    nc = seq // CHUNK
    ns = math.gcd(SEQS_PER_STEP, bsz)

    def lane_pad(v):
        return jnp.pad(v.astype(F32).reshape(1, -1), ((0, 0), (0, LANES - v.shape[-1])))

    tri = jnp.asarray(np.tril(np.ones((CHUNK, CHUNK), np.float32)), BF16)
    blk = lambda w: pl.BlockSpec((ns, CHUNK, w), lambda b, c: (b, c, 0))
    return pl.pallas_call(
        _ssd_kernel,
        grid=(bsz // ns, nc),
        in_specs=[blk(SSD_WIDTH), blk(SSD_CONV_DIM), blk(LANES),
                  _const_spec((SSD_CONV, SSD_CONV_DIM)), _const_spec((1, SSD_CONV_DIM)),
                  _const_spec((1, LANES)), _const_spec((1, LANES)), _const_spec((1, LANES)),
                  _const_spec((1, SSD_WIDTH)), _const_spec((CHUNK, CHUNK))],
        out_specs=blk(SSD_WIDTH),
        out_shape=jax.ShapeDtypeStruct((bsz, seq, SSD_WIDTH), BF16),
        scratch_shapes=[pltpu.VMEM((ns, CHUNK + CONV_PAD, SSD_CONV_DIM), F32),
                        pltpu.VMEM((ns, SSD_HEADS, SSD_STATE, SSD_HEAD_DIM), F32),
                        pltpu.VMEM((ns, CHUNK, SSD_WIDTH), F32)],
        compiler_params=_params("parallel", "arbitrary"),
        name="ssd",
    )(z, xbc, dt, conv_w.astype(F32), conv_b.astype(F32).reshape(1, -1),
      lane_pad(dt_bias), lane_pad(-jnp.exp(a_log.astype(F32))), lane_pad(d_skip),
      norm_w.astype(F32).reshape(1, -1), tri)


def _retention_tables():
    h = np.arange(RET_HEADS, dtype=np.float64)
    log_gamma = np.log1p(-np.exp2(-5.0 - h))
    idx = np.arange(CHUNK, dtype=np.float64)
    diff = idx[:, None] - idx[None, :]
    dmat = np.where(diff >= 0, np.exp(np.maximum(diff, 0.0)[None] * log_gamma[:, None, None]), 0.0)
    k_decay = np.exp((CHUNK - 1.0 - idx)[:, None] * log_gamma)
    q_decay = np.exp((idx + 1.0)[:, None] * log_gamma)
    c_decay = np.exp(CHUNK * log_gamma)[None, :]
    rep = lambda a: np.repeat(a, RET_HEAD_DIM, axis=1).astype(np.float32)
    avg = np.kron(np.eye(RET_HEADS), np.full((RET_HEAD_DIM, RET_HEAD_DIM), 1.0 / RET_HEAD_DIM))
    return (dmat.astype(np.float32), rep(k_decay * RET_HEAD_DIM ** -0.5), rep(q_decay),
            rep(c_decay), avg.astype(np.float32))


def _rope(x, cos, sin_signed):
    half = RET_HEAD_DIM // 2
    width = x.shape[1]
    fwd = pltpu.roll(x, half, 1)
    bwd = pltpu.roll(x, width - half, 1)
    lane = lax.broadcasted_iota(jnp.int32, x.shape, 1)
    swapped = jnp.where((lane % RET_HEAD_DIM) < half, bwd, fwd)
    return x * cos + swapped * sin_signed


def _retention_kernel(q_ref, k_ref, v_ref, g_ref, cos_ref, sin_ref, dmat_ref, kdec_ref, qdec_ref,
                      cdec_ref, avg_ref, gn_ref, o_ref, state_ref, y_ref):
    @pl.when(pl.program_id(1) == 0)
    def _():
        state_ref[...] = jnp.zeros(state_ref.shape, F32)

    for s in range(q_ref.shape[0]):
        _retention_chunk(q_ref.at[s], k_ref.at[s], v_ref.at[s], g_ref.at[s], cos_ref.at[s],
                         sin_ref.at[s], dmat_ref, kdec_ref, qdec_ref, cdec_ref, avg_ref, gn_ref,
                         o_ref.at[s], state_ref.at[s], y_ref.at[s])


def _retention_chunk(q_ref, k_ref, v_ref, g_ref, cos_ref, sin_ref, dmat_ref, kdec_ref, qdec_ref,
                     cdec_ref, avg_ref, gn_ref, o_ref, state_ref, y_ref):
    reps = RET_WIDTH // LANES
    cos = jnp.tile(cos_ref[...], (1, reps))
    sin = jnp.tile(sin_ref[...], (1, reps))
    q = _rope(q_ref[...].astype(F32), cos, sin)
    k = _rope(k_ref[...].astype(F32), cos, sin)
    v = v_ref[...]
    qb = q.astype(BF16)
    k_t = k.T
    kd_t = (k * kdec_ref[...]).T.astype(BF16)
    k_tb = (k_t * (RET_HEAD_DIM ** -0.5)).astype(BF16)
    qd = qdec_ref[...]
    cd = cdec_ref[...]
    for hd in range(RET_HEADS):
        sl = slice(hd * RET_HEAD_DIM, (hd + 1) * RET_HEAD_DIM)
        q_h = qb[:, sl]
        v_h = v[:, sl]
        scores = _dot(q_h, k_tb[sl, :]) * dmat_ref[hd]
        y = _dot(scores.astype(BF16), v_h)
        prev = state_ref[hd]
        y = y + _dot(q_h, prev.astype(BF16)) * qd[:, sl]
        state_ref[hd] = prev * cd[:, sl] + _dot(kd_t[sl, :], v_h)
        y_ref[:, sl] = y
    y = y_ref[...]
    avg = avg_ref[...]
    mu = _split_dot(y, avg)
    yc = y - mu
    var = _split_dot(yc * yc, avg)
    yn = yc * lax.rsqrt(var + EPS) * gn_ref[...]
    o_ref[...] = (_silu(g_ref[...].astype(F32)) * yn).astype(o_ref.dtype)


def _retention(q, k, v, g, cos2, sin2, gn_w):
    bsz, seq, _ = q.shape
    nc = seq // CHUNK
    ns = math.gcd(SEQS_PER_STEP, bsz)
    dmat, kdec, qdec, cdec, avg = (jnp.asarray(a) for a in _retention_tables())
    avg = avg.astype(BF16)
    blk = lambda w: pl.BlockSpec((ns, CHUNK, w), lambda b, c: (b, c, 0))
    return pl.pallas_call(
        _retention_kernel,
        grid=(bsz // ns, nc),
        in_specs=[blk(RET_WIDTH)] * 4 + [blk(LANES)] * 2 + [
            _const_spec(dmat.shape), _const_spec(kdec.shape), _const_spec(qdec.shape),
            _const_spec(cdec.shape), _const_spec(avg.shape), _const_spec((1, RET_WIDTH))],
        out_specs=blk(RET_WIDTH),
        out_shape=jax.ShapeDtypeStruct((bsz, seq, RET_WIDTH), BF16),
        scratch_shapes=[pltpu.VMEM((ns, RET_HEADS, RET_HEAD_DIM, RET_HEAD_DIM), F32),
                        pltpu.VMEM((ns, CHUNK, RET_WIDTH), F32)],
        compiler_params=_params("parallel", "arbitrary"),
        name="retention",
    )(q, k, v, g, cos2, sin2, dmat, kdec, qdec, cdec, avg, gn_w.astype(F32).reshape(1, -1))


def _rope_tables(positions):
    half = RET_HEAD_DIM // 2
    inv_freq = ROPE_BASE ** (-jnp.arange(half, dtype=F32) / half)
    ang = positions.astype(F32)[..., None] * inv_freq
    cos = jnp.cos(ang)
    sin = jnp.sin(ang)
    cos2 = jnp.concatenate([cos, cos, cos, cos], axis=-1)
    sin2 = jnp.concatenate([-sin, sin, -sin, sin], axis=-1)
    return cos2, sin2


def _lru_kernel(x_ref, gate_ref, cw_ref, cb_ref, w_ref, bias_ref, lamc_ref, o_ref, xpad_ref, h_ref):
    c = pl.program_id(1)

    @pl.when(c == 0)
    def _():
        h_ref[...] = jnp.zeros(h_ref.shape, F32)
        xpad_ref[0:CONV_PAD, :] = jnp.zeros((CONV_PAD, xpad_ref.shape[1]), F32)

    rows = x_ref.shape[1]
    xc = _causal_conv(xpad_ref, x_ref[0].astype(F32), cw_ref[...], cb_ref[...])
    ri = _sigmoid(_dot(xc.astype(BF16), w_ref[...]) + bias_ref[...])
    log_a = lamc_ref[...] * ri[:, :LRU_WIDTH]
    a_all = jnp.exp(log_a)
    mult = jnp.sqrt(jnp.maximum(1.0 - jnp.exp(2.0 * log_a), 0.0))
    b_all = mult * ri[:, LRU_WIDTH:] * xc
    gate = _gelu(gate_ref[0].astype(F32))
    n_steps = int(math.log2(rows))
    row0 = lax.broadcasted_iota(jnp.int32, (rows, LANES), 0) == 0
    for j in range(LRU_WIDTH // LANES):
        sl = slice(j * LANES, (j + 1) * LANES)
        a = a_all[:, sl]
        b = b_all[:, sl]
        b = b + jnp.where(row0, a * h_ref[0:1, sl], 0.0)
        for s in range(n_steps):
            k = 1 << s
            b = b + a * _shift_rows(b, k)
            if s + 1 < n_steps:
                a = a * _shift_rows(a, k, 1.0)
        h_ref[0:1, sl] = b[rows - 1:rows, :]
        o_ref[0, :, sl] = (b * gate[:, sl]).astype(o_ref.dtype)


def _block_diag(w):
    nb, n, _ = w.shape
    eye = jnp.eye(nb, dtype=w.dtype)
    return jnp.einsum('bij,bc->bicj', w, eye).reshape(nb * n, nb * n)


def _lru(x, gate, conv_w, conv_b, wa, ba, wx, bx, lam):
    bsz, seq, _ = x.shape
    rows = min(LRU_ROWS, seq)
    w = jnp.concatenate([_block_diag(wa), _block_diag(wx)], axis=1).astype(BF16)
    bias = jnp.concatenate([ba, bx]).astype(F32).reshape(1, -1)
    lamc = (-LRU_C * jax.nn.softplus(-lam.astype(F32))).reshape(1, -1)
    blk = lambda wd: pl.BlockSpec((1, rows, wd), lambda b, c: (b, c, 0))
    return pl.pallas_call(
        _lru_kernel,
        grid=(bsz, seq // rows),
        in_specs=[blk(LRU_WIDTH), blk(LRU_WIDTH),
                  _const_spec((LRU_CONV, LRU_WIDTH)), _const_spec((1, LRU_WIDTH)),
                  _const_spec((LRU_WIDTH, 2 * LRU_WIDTH)), _const_spec((1, 2 * LRU_WIDTH)),
                  _const_spec((1, LRU_WIDTH))],
        out_specs=blk(LRU_WIDTH),
        out_shape=jax.ShapeDtypeStruct((bsz, seq, LRU_WIDTH), BF16),
        scratch_shapes=[pltpu.VMEM((rows + CONV_PAD, LRU_WIDTH), F32),
                        pltpu.VMEM((SUBLANES, LRU_WIDTH), F32)],
        compiler_params=_params("parallel", "arbitrary"),
        name="rglru",
    )(x, gate, conv_w.astype(F32), conv_b.astype(F32).reshape(1, -1), w, bias, lamc)


def _merge_kernel(h_ref, nw_ref, u_ref, ys5_ref, b1_ref, b2_ref, b3_ref, wg_ref, bg_ref, d_ref,
                  wglu_ref, wb_ref, wo_ref, o_ref, yscr_ref):
    folded = yscr_ref.shape[1] // S5_SUB
    for q in range(S5_Q):
        z = ys5_ref[q].astype(F32) + d_ref[q] * u_ref[q].astype(F32)
        for t in range(S5_SUB):
            yscr_ref[q, pl.ds(t, folded, stride=S5_SUB), :] = z[:, t * LANES:(t + 1) * LANES]
    y = _gelu(jnp.concatenate([yscr_ref[q] for q in range(S5_Q)], axis=1))
    b0 = (y * _sigmoid(_dot(y.astype(BF16), wglu_ref[...]))).astype(BF16)
    branches = (b0, b1_ref[...], b2_ref[...], b3_ref[...])
    h = h_ref[...]
    xb = _rms(h, nw_ref[...]).astype(BF16)
    merged = None
    for i, br in enumerate(branches):
        sl = slice(i * D_MODEL, (i + 1) * D_MODEL)
        th = jnp.tanh(_dot(xb, wg_ref[:, sl]) + bg_ref[:, sl])
        p = _dot(br, wb_ref[i])
        term = p + th * p
        merged = term if merged is None else merged + term
    o_ref[...] = h + _dot(merged.astype(BF16), wo_ref[...])


def _merge(h, norm_w, u, ys5, b1, b2, b3, w_gate_half, b_gate, s5_d, w_glu, w_branch, w_out):
    t = h.shape[0]
    tm = min(ROW_TILE, t)
    row = lambda w: pl.BlockSpec((tm, w), lambda i: (i, 0))
    s5_blk = pl.BlockSpec((S5_Q, tm // S5_SUB, S5_ROW), lambda i: (0, i, 0))
    d_fold = jnp.tile(s5_d.astype(F32).reshape(S5_Q, 1, LANES), (1, 1, S5_SUB))
    return pl.pallas_call(
        _merge_kernel,
        grid=(t // tm,),
        in_specs=[row(D_MODEL), _const_spec((1, D_MODEL)), s5_blk, s5_blk] + [row(512)] * 3 + [
                  _const_spec((D_MODEL, N_BRANCH * D_MODEL)),
                  _const_spec((1, N_BRANCH * D_MODEL)), _const_spec((S5_Q, 1, S5_ROW)),
                  _const_spec((S5_WIDTH, S5_WIDTH)),
                  _const_spec((N_BRANCH, 512, D_MODEL)), _const_spec((D_MODEL, D_MODEL))],
        out_specs=row(D_MODEL),
        out_shape=jax.ShapeDtypeStruct((t, D_MODEL), F32),
        scratch_shapes=[pltpu.VMEM((S5_Q, tm, LANES), F32)],
        compiler_params=_params("parallel"),
        name="merge",
    )(h, norm_w.astype(F32).reshape(1, -1), u, ys5, b1, b2, b3, w_gate_half,
      0.5 * b_gate.astype(F32).reshape(1, -1), d_fold, w_glu.astype(BF16), w_branch.astype(BF16),
      (0.5 * w_out).astype(BF16))


def _kv_kernel(mem_ref, nw_ref, w_ref, k_ref, v_ref):
    mb = _rms(mem_ref[0], nw_ref[...]).astype(BF16)
    for c in range(0, D_MODEL, DOT_COLS):
        k_ref[0, :, c:c + DOT_COLS] = _dot(mb, w_ref[:, c:c + DOT_COLS]).astype(BF16)
        v_ref[0, :, c:c + DOT_COLS] = _dot(
            mb, w_ref[:, D_MODEL + c:D_MODEL + c + DOT_COLS]).astype(BF16)


def _kv(mem, norm_w, wk, wv):
    bsz, n_mem, _ = mem.shape
    w = jnp.concatenate([wk, wv], axis=1).astype(BF16)
    blk = pl.BlockSpec((1, n_mem, D_MODEL), lambda b: (b, 0, 0))
    return pl.pallas_call(
        _kv_kernel,
        grid=(bsz,),
        in_specs=[blk, _const_spec((1, D_MODEL)), _const_spec((D_MODEL, 2 * D_MODEL))],
        out_specs=[blk, blk],
        out_shape=[jax.ShapeDtypeStruct((bsz, n_mem, D_MODEL), BF16)] * 2,
        compiler_params=_params("parallel"),
        name="xattn_kv",
    )(mem, norm_w.astype(F32).reshape(1, -1), w)


def _xattn_kernel(h_ref, nw_ref, wq_ref, k_ref, v_ref, wo_ref, o_ref, att_ref):
    h = h_ref[0]
    xb = _rms(h, nw_ref[...]).astype(BF16)
    scale = XA_HEAD_DIM ** -0.5
    for hd in range(XA_HEADS):
        sl = slice(hd * XA_HEAD_DIM, (hd + 1) * XA_HEAD_DIM)
        q = _dot(xb, wq_ref[:, sl]).astype(BF16)
        s = _dot_nt(q, k_ref[0, :, sl]) * scale
        p = jnp.exp(s - jnp.max(s, axis=-1, keepdims=True))
        denom = jnp.sum(p, axis=-1, keepdims=True)
        att_ref[:, sl] = (_dot(p.astype(BF16), v_ref[0, :, sl]) / denom).astype(BF16)
    o_ref[0] = h + _dot(att_ref[...], wo_ref[...])


def _xattn(h3, norm_w, wq, k, v, wo):
    bsz, seq, _ = h3.shape
    tq = min(ROW_TILE, seq)
    n_mem = k.shape[1]
    row = pl.BlockSpec((1, tq, D_MODEL), lambda b, i: (b, i, 0))
    kv = pl.BlockSpec((1, n_mem, D_MODEL), lambda b, i: (b, 0, 0))
    return pl.pallas_call(
        _xattn_kernel,
        grid=(bsz, seq // tq),
        in_specs=[row, _const_spec((1, D_MODEL)), _const_spec((D_MODEL, D_MODEL)), kv, kv,
                  _const_spec((D_MODEL, D_MODEL))],
        out_specs=row,
        out_shape=jax.ShapeDtypeStruct((bsz, seq, D_MODEL), F32),
        scratch_shapes=[pltpu.VMEM((tq, D_MODEL), BF16)],
        compiler_params=_params("parallel", "parallel"),
        name="xattn",
    )(h3, norm_w.astype(F32).reshape(1, -1), wq.astype(BF16), k, v, wo.astype(BF16))


FF_COLS = 256


def _swiglu_acc(xb, w1_ref, w3_ref, w2_ref, width, lead=()):
    acc = None
    for c in range(0, width, FF_COLS):
        a = _dot(xb, w1_ref[lead + (slice(None), slice(c, c + FF_COLS))])
        b = _dot(xb, w3_ref[lead + (slice(None), slice(c, c + FF_COLS))])
        g = (_silu(a) * b).astype(BF16)
        term = _dot(g, w2_ref[lead + (slice(c, c + FF_COLS), slice(None))])
        acc = term if acc is None else acc + term
    return acc


def _ffn_kernel(h_ref, nw_ref, w1_ref, w3_ref, w2_ref, o_ref):
    h = h_ref[...]
    xb = _rms(h, nw_ref[...]).astype(BF16)
    o_ref[...] = h + _swiglu_acc(xb, w1_ref, w3_ref, w2_ref, D_FF)


def _ffn(h, norm_w, w1, w3, w2):
    t = h.shape[0]
    tm = min(ROW_TILE, t)
    row = pl.BlockSpec((tm, D_MODEL), lambda i: (i, 0))
    return pl.pallas_call(
        _ffn_kernel,
        grid=(t // tm,),
        in_specs=[row, _const_spec((1, D_MODEL)), _const_spec((D_MODEL, D_FF)),
                  _const_spec((D_MODEL, D_FF)), _const_spec((D_FF, D_MODEL))],
        out_specs=row,
        out_shape=jax.ShapeDtypeStruct((t, D_MODEL), F32),
        compiler_params=_params("parallel"),
        name="ffn",
    )(h, norm_w.astype(F32).reshape(1, -1), w1.astype(BF16), w3.astype(BF16), w2.astype(BF16))


MOE_TILE = 512
TOP_K = 2
DISPATCH_ROWS = 512
COMBINE_ROWS = 512
DMA_UNROLL = 8


def _sorted_tiles(t):
    return pl.cdiv(TOP_K * t, MOE_TILE) + N_EXPERTS


def _router_kernel(h_ref, nw_ref, wr_ref, xn_ref, member_ref, wsel_ref, esel_ref):
    xn = _rms(h_ref[...], nw_ref[...])
    xn_ref[...] = xn
    logits = lax.dot_general(xn, wr_ref[...], (((1,), (0,)), ((), ())),
                             precision=lax.Precision.HIGHEST, preferred_element_type=F32)
    lane = lax.broadcasted_iota(jnp.int32, logits.shape, 1)
    logits = jnp.where(lane < N_EXPERTS, logits, -jnp.inf)
    m1 = jnp.max(logits, axis=-1, keepdims=True)
    i1 = jnp.min(jnp.where(logits == m1, lane, LANES), axis=-1, keepdims=True)
    rest = jnp.where(lane == i1, -jnp.inf, logits)
    m2 = jnp.max(rest, axis=-1, keepdims=True)
    i2 = jnp.min(jnp.where(rest == m2, lane, LANES), axis=-1, keepdims=True)
    e2 = jnp.exp(m2 - m1)
    w1 = 1.0 / (1.0 + e2)
    w2 = e2 / (1.0 + e2)
    member_ref[...] = jnp.where(lane == i1, 1.0, jnp.where(lane == i2, 1.0, 0.0)).astype(BF16)
    wsel_ref[...] = jnp.where(lane == 0, w1, jnp.where(lane == 1, w2, 0.0))
    esel_ref[...] = jnp.where(lane == 0, i1, jnp.where(lane == 1, i2, 0))


def _router(h, norm_w, w_router):
    t = h.shape[0]
    tm = min(ROW_TILE, t)
    wr = jnp.pad(w_router.astype(F32), ((0, 0), (0, LANES - N_EXPERTS)))
    row = lambda w: pl.BlockSpec((tm, w), lambda i: (i, 0))
    return pl.pallas_call(
        _router_kernel,
        grid=(t // tm,),
        in_specs=[row(D_MODEL), _const_spec((1, D_MODEL)), _const_spec((D_MODEL, LANES))],
        out_specs=[row(D_MODEL), row(LANES), row(LANES), row(LANES)],
        out_shape=[jax.ShapeDtypeStruct((t, D_MODEL), F32), jax.ShapeDtypeStruct((t, LANES), BF16),
                   jax.ShapeDtypeStruct((t, LANES), F32), jax.ShapeDtypeStruct((t, LANES), jnp.int32)],
        compiler_params=_params("parallel"),
        name="moe_router",
    )(h, norm_w.astype(F32).reshape(1, -1), wr)


def _positions_kernel(member_ref, esel_ref, pos_ref, meta_ref, cnt_ref, carry_ref, off_ref):
    phase = pl.program_id(0)
    i = pl.program_id(1)
    m = member_ref[...]
    tp = m.shape[0]
    col_sum = _dot(jnp.ones((SUBLANES, tp), BF16), m)

    @pl.when((phase == 0) & (i == 0))
    def _():
        cnt_ref[...] = jnp.zeros(cnt_ref.shape, F32)

    @pl.when(phase == 0)
    def _():
        cnt_ref[...] += col_sum

    @pl.when((phase == 1) & (i == 0))
    def _():
        tiles = jnp.floor((cnt_ref[...] + (MOE_TILE - 1.0)) * (1.0 / MOE_TILE))
        r = lax.broadcasted_iota(jnp.int32, (LANES, LANES), 0)
        c = lax.broadcasted_iota(jnp.int32, (LANES, LANES), 1)
        before = jnp.where(r < c, 1.0, 0.0).astype(BF16)
        first_tile = _dot(tiles.astype(BF16), before)
        off_ref[...] = first_tile * MOE_TILE
        carry_ref[...] = jnp.zeros(carry_ref.shape, F32)
        row = lax.broadcasted_iota(jnp.int32, (SUBLANES, LANES), 0)
        meta_ref[...] = jnp.where(row == 0, first_tile,
                                  jnp.where(row == 1, tiles, cnt_ref[...])).astype(jnp.int32)

    @pl.when(phase == 1)
    def _():
        r = lax.broadcasted_iota(jnp.int32, (tp, tp), 0)
        c = lax.broadcasted_iota(jnp.int32, (tp, tp), 1)
        earlier = jnp.where(r > c, 1.0, 0.0).astype(BF16)
        posm = off_ref[0:1, :] + carry_ref[0:1, :] + _dot(earlier, m)
        lane = lax.broadcasted_iota(jnp.int32, posm.shape, 1)
        e = esel_ref[...]
        p0 = jnp.sum(jnp.where(lane == e[:, 0:1], posm, 0.0), axis=-1, keepdims=True)
        p1 = jnp.sum(jnp.where(lane == e[:, 1:2], posm, 0.0), axis=-1, keepdims=True)
        pos_ref[...] = jnp.where(lane == 0, p0, jnp.where(lane == 1, p1, 0.0)).astype(jnp.int32)
        carry_ref[...] += col_sum


def _positions(member, esel):
    t = member.shape[0]
    tp = min(ROW_TILE, t)
    return pl.pallas_call(
        _positions_kernel,
        grid=(2, t // tp),
        in_specs=[pl.BlockSpec((tp, LANES), lambda p, i: (i, 0)),
                  pl.BlockSpec((tp, LANES), lambda p, i: (i * p, 0))],
        out_specs=[pl.BlockSpec((tp, LANES), lambda p, i: (i * p, 0)),
                   pl.BlockSpec((SUBLANES, LANES), lambda p, i: (0, 0))],
        out_shape=[jax.ShapeDtypeStruct((t, LANES), jnp.int32),
                   jax.ShapeDtypeStruct((SUBLANES, LANES), jnp.int32)],
        scratch_shapes=[pltpu.VMEM((SUBLANES, LANES), F32)] * 3,
        compiler_params=_params("arbitrary", "arbitrary"),
        name="moe_positions",
    )(member, esel)


def _row_copy(src_ref, src_row, dst_ref, dst_row, sem):
    return pltpu.make_async_copy(src_ref.at[pl.ds(src_row, 1)], dst_ref.at[pl.ds(dst_row, 1)], sem)


def _dispatch_kernel(pad_lo_ref, pad_hi_ref, pos_ref, xn_ref, xs_ref, zero_ref, sem):
    rows = xn_ref.shape[0]

    @pl.when(pl.program_id(0) == 0)
    def _():
        zero_ref[...] = jnp.zeros(zero_ref.shape, F32)
        for e in range(N_EXPERTS):
            lo, hi = pad_lo_ref[e], pad_hi_ref[e]

            def fill(r, carry):
                _row_copy(zero_ref, 0, xs_ref, r, sem).start()
                return carry

            def filled(r, carry):
                _row_copy(zero_ref, 0, xs_ref, r, sem).wait()
                return carry

            lax.fori_loop(lo, hi, fill, 0)
            lax.fori_loop(lo, hi, filled, 0)

    def copies(r):
        return [_row_copy(xn_ref, r, xs_ref, pos_ref[0, k, r], sem) for k in range(TOP_K)]

    def issue(r, carry):
        for k, cp in enumerate(copies(r)):
            cp.start(priority=k)
        return carry

    def drain(r, carry):
        for cp in copies(r):
            cp.wait()
        return carry

    lax.fori_loop(0, rows, issue, 0, unroll=DMA_UNROLL)
    lax.fori_loop(0, rows, drain, 0, unroll=DMA_UNROLL)


def _slot_major(pos, rows):
    t = pos.shape[0]
    return jnp.transpose(pos[:, :TOP_K].reshape(t // rows, rows, TOP_K), (0, 2, 1))


def _dispatch(xn, pos, pad_lo, pad_hi, n_sorted):
    t = xn.shape[0]
    rows = min(DISPATCH_ROWS, t)
    return pl.pallas_call(
        _dispatch_kernel,
        grid_spec=pltpu.PrefetchScalarGridSpec(
            num_scalar_prefetch=2,
            grid=(t // rows,),
            in_specs=[pl.BlockSpec((1, TOP_K, rows), lambda i, lo, hi: (i, 0, 0),
                                   memory_space=pltpu.SMEM),
                      pl.BlockSpec((rows, D_MODEL), lambda i, lo, hi: (i, 0))],
            out_specs=pl.BlockSpec(memory_space=pl.ANY),
            scratch_shapes=[pltpu.VMEM((SUBLANES, D_MODEL), F32), pltpu.SemaphoreType.DMA(())]),
        out_shape=jax.ShapeDtypeStruct((n_sorted, D_MODEL), F32),
        compiler_params=_params("arbitrary"),
        name="moe_dispatch",
    )(pad_lo, pad_hi, _slot_major(pos, rows), xn)


MOE_FF_SPLIT = 2
MOE_FF_BLOCK = D_FF_EXPERT // MOE_FF_SPLIT


def _experts_kernel(tile_expert_ref, n_used_ref, xs_ref, w1_ref, w3_ref, w2_ref, y_ref):
    del tile_expert_ref
    used = pl.program_id(0) < n_used_ref[0]
    f = pl.program_id(1)

    def part():
        xb = xs_ref[...].astype(BF16)
        return _swiglu_acc(xb, w1_ref, w3_ref, w2_ref, MOE_FF_BLOCK, lead=(0,))

    @pl.when(used & (f == 0))
    def _():
        y_ref[...] = part()

    @pl.when(used & (f > 0))
    def _():
        y_ref[...] += part()

    @pl.when(jnp.logical_not(used) & (f == 0))
    def _():
        y_ref[...] = jnp.zeros(y_ref.shape, F32)


def _experts(xs, tile_expert, n_used, w1, w3, w2):
    n_tiles = xs.shape[0] // MOE_TILE
    row = pl.BlockSpec((MOE_TILE, D_MODEL), lambda i, f, te, nu: (i, 0))
    row_in = pl.BlockSpec((MOE_TILE, D_MODEL), lambda i, f, te, nu: (jnp.minimum(i, nu[0] - 1), 0))
    ff = lambda i, f, nu: jnp.where(i < nu[0], f, MOE_FF_SPLIT - 1)
    w_up = pl.BlockSpec((1, D_MODEL, MOE_FF_BLOCK), lambda i, f, te, nu: (te[i], 0, ff(i, f, nu)))
    w_down = pl.BlockSpec((1, MOE_FF_BLOCK, D_MODEL), lambda i, f, te, nu: (te[i], ff(i, f, nu), 0))
    return pl.pallas_call(
        _experts_kernel,
        grid_spec=pltpu.PrefetchScalarGridSpec(
            num_scalar_prefetch=2,
            grid=(n_tiles, MOE_FF_SPLIT),
            in_specs=[row_in, w_up, w_up, w_down],
            out_specs=row),
        out_shape=jax.ShapeDtypeStruct(xs.shape, F32),
        compiler_params=_params("arbitrary", "arbitrary"),
        name="moe_experts",
    )(tile_expert, n_used, xs, w1.astype(BF16), w3.astype(BF16), w2.astype(BF16))


def _combine_kernel(pos_ref, h_ref, wsel_ref, fw_ref, y_ref, o_ref, ybuf_ref, sem, *, final_norm):
    rows = h_ref.shape[0]

    def copies(r):
        return [_row_copy(y_ref, pos_ref[0, k, r], ybuf_ref.at[k], r, sem) for k in range(TOP_K)]

    def issue(r, carry):
        for k, cp in enumerate(copies(r)):
            cp.start(priority=k)
        return carry

    def drain(r, carry):
        for cp in copies(r):
            cp.wait()
        return carry

    lax.fori_loop(0, rows, issue, 0, unroll=DMA_UNROLL)
    lax.fori_loop(0, rows, drain, 0, unroll=DMA_UNROLL)
    w = wsel_ref[...]
    out = h_ref[...] + w[:, 0:1] * ybuf_ref[0] + w[:, 1:2] * ybuf_ref[1]
    if final_norm:
        out = _rms(out, fw_ref[...])
    o_ref[...] = out


def _combine(h, wsel, pos, y_sorted, final_w, final_norm):
    t = h.shape[0]
    rows = min(COMBINE_ROWS, t)
    return pl.pallas_call(
        functools.partial(_combine_kernel, final_norm=final_norm),
        grid=(t // rows,),
        in_specs=[pl.BlockSpec((1, TOP_K, rows), lambda i: (i, 0, 0), memory_space=pltpu.SMEM),
                  pl.BlockSpec((rows, D_MODEL), lambda i: (i, 0)),
                  pl.BlockSpec((rows, LANES), lambda i: (i, 0)),
                  _const_spec((1, D_MODEL)),
                  pl.BlockSpec(memory_space=pl.ANY)],
        out_specs=pl.BlockSpec((rows, D_MODEL), lambda i: (i, 0)),
        out_shape=jax.ShapeDtypeStruct((t, D_MODEL), F32),
        scratch_shapes=[pltpu.VMEM((TOP_K, rows, D_MODEL), F32), pltpu.SemaphoreType.DMA(())],
        compiler_params=_params("arbitrary"),
        name="moe_combine",
    )(_slot_major(pos, rows), h, wsel, final_w.astype(F32).reshape(1, -1), y_sorted)


def _moe(h, norm_w, w_router, w1, w3, w2, final_w, final_norm):
    t = h.shape[0]
    n_tiles = _sorted_tiles(t)
    xn, member, wsel, esel = _router(h, norm_w, w_router)
    pos, meta = _positions(member, esel)
    first_tile = meta[0, :N_EXPERTS]
    last_tile = first_tile + meta[1, :N_EXPERTS]
    tile = jnp.arange(n_tiles, dtype=jnp.int32)
    tile_expert = jnp.minimum(jnp.sum(tile[:, None] >= last_tile[None, :], axis=1), N_EXPERTS - 1)
    pad_lo = first_tile * MOE_TILE + meta[2, :N_EXPERTS]
    xs = _dispatch(xn, pos, pad_lo, last_tile * MOE_TILE, n_tiles * MOE_TILE)
    ys = _experts(xs, tile_expert.astype(jnp.int32), last_tile[N_EXPERTS - 1:], w1, w3, w2)
    return _combine(h, wsel, pos, ys, final_w, final_norm)


def _final_norm_kernel(h_ref, w_ref, o_ref):
    o_ref[...] = _rms(h_ref[...], w_ref[...])


def _final_norm(h, w):
    t = h.shape[0]
    tm = min(ROW_TILE, t)
    row = pl.BlockSpec((tm, D_MODEL), lambda i: (i, 0))
    return pl.pallas_call(
        _final_norm_kernel, grid=(t // tm,), in_specs=[row, _const_spec((1, D_MODEL))],
        out_specs=row, out_shape=jax.ShapeDtypeStruct((t, D_MODEL), F32),
        compiler_params=_params("parallel"), name="final_norm",
    )(h, w.astype(F32).reshape(1, -1))


def _mixing_block(h, bsz, seq, cos2, sin2, norm_w, w_in, b_gate,
                  s5_lam_re, s5_lam_im, s5_b_re, s5_b_im, s5_c_re, s5_c_im, s5_d, s5_log_dt, s5_w_glu,
                  ssd_conv_w, ssd_conv_b, ssd_dt_bias, ssd_a_log, ssd_d, ssd_norm,
                  ret_norm,
                  lru_conv_w, lru_conv_b, lru_wa, lru_ba, lru_wx, lru_bx, lru_lam,
                  w_branch, w_out):
    w_cat, w_gate_half = _pack_w_in(w_in)
    (u_s5, z_ssd, xbc_ssd, dt_ssd, q_ret, k_ret, v_ret, g_ret, x_lru,
     gate_lru) = _inproj(h, norm_w, w_cat)
    seq3 = lambda a: a.reshape(bsz, seq, a.shape[-1])
    tables = _s5_tables(s5_lam_re, s5_lam_im, s5_b_re, s5_b_im, s5_c_re, s5_c_im, s5_log_dt,
                        seq // S5_SUB)
    y_s5 = _s5_scan(u_s5, tables, bsz, seq)
    y_ssd = _ssd(seq3(z_ssd), seq3(xbc_ssd), seq3(dt_ssd), ssd_conv_w, ssd_conv_b, ssd_dt_bias,
                 ssd_a_log, ssd_d, ssd_norm)
    y_ret = _retention(seq3(q_ret), seq3(k_ret), seq3(v_ret), seq3(g_ret), cos2, sin2, ret_norm)
    y_lru = _lru(seq3(x_lru), seq3(gate_lru), lru_conv_w, lru_conv_b, lru_wa, lru_ba, lru_wx,
                 lru_bx, lru_lam)
    flat = lambda a: a.reshape(bsz * seq, a.shape[-1])
    return _merge(h, norm_w, u_s5, y_s5, flat(y_ssd), flat(y_ret), flat(y_lru), w_gate_half,
                  b_gate, s5_d, s5_w_glu, w_branch, w_out)


def kernel(x, mem, positions, norm_mix, w_in, b_gate, s5_lam_re, s5_lam_im, s5_b_re, s5_b_im, s5_c_re, s5_c_im, s5_d, s5_log_dt, s5_w_glu, ssd_conv_w, ssd_conv_b, ssd_dt_bias, ssd_a_log, ssd_d, ssd_norm, ret_norm, lru_conv_w, lru_conv_b, lru_wa, lru_ba, lru_wx, lru_bx, lru_lam, w_branch, w_out, norm_xa, norm_mem, xa_wq, xa_wk, xa_wv, xa_wo, norm_ffn, ffn_w1, ffn_w3, ffn_w2, moe_router, moe_w1, moe_w3, moe_w2, norm_final):
    bsz, seq, _ = x.shape
    depth = norm_mix.shape[0]
    cos2, sin2 = _rope_tables(positions)
    h = x.reshape(bsz * seq, D_MODEL)
    for i in range(depth):
        h = _mixing_block(h, bsz, seq, cos2, sin2, norm_mix[i], w_in[i], b_gate[i],
                          s5_lam_re[i], s5_lam_im[i], s5_b_re[i], s5_b_im[i], s5_c_re[i], s5_c_im[i],
                          s5_d[i], s5_log_dt[i], s5_w_glu[i],
                          ssd_conv_w[i], ssd_conv_b[i], ssd_dt_bias[i], ssd_a_log[i], ssd_d[i],
                          ssd_norm[i], ret_norm[i],
                          lru_conv_w[i], lru_conv_b[i], lru_wa[i], lru_ba[i], lru_wx[i], lru_bx[i],
                          lru_lam[i], w_branch[i], w_out[i])
        k, v = _kv(mem, norm_mem[i], xa_wk[i], xa_wv[i])
        h = _xattn(h.reshape(bsz, seq, D_MODEL), norm_xa[i], xa_wq[i], k, v, xa_wo[i])
        h = h.reshape(bsz * seq, D_MODEL)
        last = i == depth - 1
        if i % 2 == 0:
            h = _ffn(h, norm_ffn[i], ffn_w1[i // 2], ffn_w3[i // 2], ffn_w2[i // 2])
            if last:
                h = _final_norm(h, norm_final)
        else:
            h = _moe(h, norm_ffn[i], moe_router[i // 2], moe_w1[i // 2], moe_w3[i // 2],
                     moe_w2[i // 2], norm_final, last)
    return h.reshape(bsz, seq, D_MODEL)
```

```python
import functools
import math

import numpy as np
import jax
import jax.numpy as jnp
from jax import lax
from jax.experimental import pallas as pl
from jax.experimental.pallas import tpu as pltpu

F32 = jnp.float32
BF16 = jnp.bfloat16

D_MODEL = 1024
N_MEM = 256
EPS = 1e-6
CHUNK = 128
N_BRANCH = 4
S5_WIDTH = 512
S5_GROUP = 16
S5_GROUPS = 32
S5_STATE = 64
S5_SUB = 16
SSD_HEADS = 8
SSD_HEAD_DIM = 64
SSD_WIDTH = 512
SSD_GROUPS = 2
SSD_STATE = 64
SSD_CONV = 4
SSD_CONV_DIM = SSD_WIDTH + 2 * SSD_GROUPS * SSD_STATE
RET_HEADS = 8
RET_HEAD_DIM = 64
RET_WIDTH = 512
ROPE_BASE = 10000.0
LRU_WIDTH = 512
LRU_BLOCKS = 8
LRU_BLOCK = 64
LRU_CONV = 4
LRU_C = 8.0
XA_HEADS = 4
XA_HEAD_DIM = 256
D_FF = 2816
N_EXPERTS = 8
D_FF_EXPERT = 3584
SECTION_WIDTHS = (S5_WIDTH, SSD_WIDTH, SSD_CONV_DIM, SSD_HEADS,
                  RET_WIDTH, RET_WIDTH, RET_WIDTH, RET_WIDTH,
                  LRU_WIDTH, LRU_WIDTH, N_BRANCH * D_MODEL)

LANES = 128
SUBLANES = 8
VMEM_LIMIT = 56 * 1024 * 1024
ROW_TILE = 512
LRU_ROWS = 256


def _params(*sem):
    return pltpu.CompilerParams(dimension_semantics=sem, vmem_limit_bytes=VMEM_LIMIT)


def _const_spec(shape):
    nd = len(shape)
    return pl.BlockSpec(shape, lambda *_: (0,) * nd)


def _rms(x, w):
    return x * lax.rsqrt(jnp.mean(x * x, axis=-1, keepdims=True) + EPS) * w


def _sigmoid(x):
    return 0.5 + 0.5 * jnp.tanh(0.5 * x)


def _silu(x):
    return x * _sigmoid(x)


def _gelu(x):
    return jax.nn.gelu(x)


def _softplus(x):
    return jnp.maximum(x, 0.0) + jnp.log(1.0 + jnp.exp(-jnp.abs(x)))


def _dot(a, b):
    return jnp.dot(a, b, preferred_element_type=F32)


def _dot_nt(a, b):
    return lax.dot_general(a, b, (((1,), (1,)), ((), ())), preferred_element_type=F32)


def _split3(x):
    hi = x.astype(BF16)
    r1 = x - hi.astype(F32)
    mid = r1.astype(BF16)
    lo = (r1 - mid.astype(F32)).astype(BF16)
    return hi, mid, lo


def _split_dot(x, m_bf16):
    hi, mid, lo = _split3(x)
    return _dot(hi, m_bf16) + _dot(mid, m_bf16) + _dot(lo, m_bf16)


def _split_dot_left(m_bf16, x):
    hi, mid, lo = _split3(x)
    return _dot(m_bf16, hi) + _dot(m_bf16, mid) + _dot(m_bf16, lo)


def _shift_rows(x, k, fill=0.0):
    rows = x.shape[0]
    if k % SUBLANES == 0:
        return jnp.concatenate([jnp.full((k, x.shape[1]), fill, x.dtype), x[:rows - k]], axis=0)
    rolled = pltpu.roll(x, k, 0)
    row = lax.broadcasted_iota(jnp.int32, x.shape, 0)
    return jnp.where(row >= k, rolled, fill)


IN_WIDTHS = (S5_WIDTH, SSD_WIDTH, SSD_CONV_DIM, LANES,
             RET_WIDTH, RET_WIDTH, RET_WIDTH, RET_WIDTH,
             LRU_WIDTH, LRU_WIDTH)
IN_TOTAL_PADDED = sum(IN_WIDTHS)
DOT_COLS = 512


def _inproj_kernel(h_ref, nw_ref, w_ref, u_ref, *refs):
    out_refs, uscr_ref = refs[:-1], refs[-1]
    xb = _rms(h_ref[...], nw_ref[...]).astype(BF16)
    u = _dot(xb, w_ref[:, :S5_WIDTH])
    folded = u.shape[0] // S5_SUB
    for q in range(S5_Q):
        uscr_ref[q] = u[:, q * LANES:(q + 1) * LANES]
        for t in range(S5_SUB):
            u_ref[q, :, t * LANES:(t + 1) * LANES] = uscr_ref[
                q, pl.ds(t, folded, stride=S5_SUB), :].astype(u_ref.dtype)
    off = S5_WIDTH
    for o_ref, width in zip(out_refs, IN_WIDTHS[1:]):
        for c in range(0, width, DOT_COLS):
            n = min(DOT_COLS, width - c)
            o_ref[:, c:c + n] = _dot(xb, w_ref[:, off + c:off + c + n]).astype(o_ref.dtype)
        off += width


def _inproj(h, norm_w, w_cat):
    t = h.shape[0]
    tm = min(ROW_TILE, t)
    out_shape = ([jax.ShapeDtypeStruct((S5_Q, t // S5_SUB, S5_ROW), BF16)]
                 + [jax.ShapeDtypeStruct((t, w), BF16) for w in IN_WIDTHS[1:]])
    out_specs = ([pl.BlockSpec((S5_Q, tm // S5_SUB, S5_ROW), lambda i: (0, i, 0))]
                 + [pl.BlockSpec((tm, w), lambda i: (i, 0)) for w in IN_WIDTHS[1:]])
    return pl.pallas_call(
        _inproj_kernel,
        grid=(t // tm,),
        in_specs=[pl.BlockSpec((tm, D_MODEL), lambda i: (i, 0)),
                  _const_spec((1, D_MODEL)),
                  _const_spec((D_MODEL, IN_TOTAL_PADDED))],
        out_specs=out_specs,
        out_shape=out_shape,
        scratch_shapes=[pltpu.VMEM((S5_Q, tm, LANES), F32)],
        compiler_params=_params("parallel"),
        name="inproj",
    )(h, norm_w.reshape(1, D_MODEL), w_cat)


def _pack_w_in(w_in):
    pieces = []
    off = 0
    for width in SECTION_WIDTHS[:-1]:
        sec = w_in[:, off:off + width]
        if width == SSD_HEADS:
            sec = jnp.pad(sec, ((0, 0), (0, LANES - width)))
        pieces.append(sec)
        off += width
    w_gate = (0.5 * w_in[:, off:]).astype(BF16)
    return jnp.concatenate(pieces, axis=1).astype(BF16), w_gate


S5_Q = S5_WIDTH // LANES
S5_QG = S5_GROUPS // S5_Q
S5_QS = S5_QG * S5_STATE
S5_ROW = S5_SUB * LANES
S5_TILE = 256
S5_SEQS = 4


def _s5_expand_tables(wc_ref, xc_ref, rin_ref, rout_ref, win_ref, wx_ref):
    group_shift = int(math.log2(S5_GROUP))
    state_shift = int(math.log2(S5_STATE))
    for c in range(0, 2 * S5_QS, DOT_COLS):
        r = lax.broadcasted_iota(jnp.int32, (S5_ROW, DOT_COLS), 0)
        col = lax.broadcasted_iota(jnp.int32, (S5_ROW, DOT_COLS), 1) + c
        own = ((r >> group_shift) & (S5_QG - 1)) == ((col & (S5_QS - 1)) >> state_shift)
        rep = _dot(wc_ref[0], rin_ref[:, c:c + DOT_COLS])
        win_ref[:, c:c + DOT_COLS] = jnp.where(own, rep, 0.0).astype(BF16)
    for c in range(0, S5_ROW, DOT_COLS):
        r = lax.broadcasted_iota(jnp.int32, (2 * S5_QS, DOT_COLS), 0)
        col = lax.broadcasted_iota(jnp.int32, (2 * S5_QS, DOT_COLS), 1) + c
        own = ((r & (S5_QS - 1)) >> state_shift) == ((col >> group_shift) & (S5_QG - 1))
        rep = _dot(xc_ref[0], rout_ref[:, c:c + DOT_COLS])
        wx_ref[:, c:c + DOT_COLS] = jnp.where(own, rep, 0.0).astype(BF16)


def _s5_kernel(u_ref, toe_ref, wc_ref, xc_ref, rin_ref, rout_ref, pw_ref, y_ref,
               win_ref, wx_ref, x_ref, sp_ref, *, rows):
    @pl.when(pl.program_id(1) == 0)
    def _():
        _s5_expand_tables(wc_ref, xc_ref, rin_ref, rout_ref, win_ref, wx_ref)

    n_steps = int(math.log2(rows))
    for c in range(0, 2 * S5_QS, DOT_COLS):
        x_ref[:, c:c + DOT_COLS] = _dot(u_ref[0], win_ref[:, c:c + DOT_COLS])

    def seq(b, carry):
        r0 = pl.multiple_of(b * rows, rows)
        for lc in range(S5_QS // LANES):
            re = slice(lc * LANES, (lc + 1) * LANES)
            im = slice(S5_QS + lc * LANES, S5_QS + (lc + 1) * LANES)
            sr = x_ref[pl.ds(r0, rows), re]
            si = x_ref[pl.ds(r0, rows), im]
            for j in range(n_steps):
                k = 1 << j
                pr = pw_ref[0, 2 * j:2 * j + 1, re]
                pi = pw_ref[0, 2 * j + 1:2 * j + 2, re]
                shr = _shift_rows(sr, k)
                shi = _shift_rows(si, k)
                sr, si = sr + pr * shr - pi * shi, si + pr * shi + pi * shr
            sp_ref[pl.ds(r0, rows), re] = _shift_rows(sr, 1).astype(BF16)
            sp_ref[pl.ds(r0, rows), im] = _shift_rows(si, 1).astype(BF16)
        return carry

    lax.fori_loop(0, u_ref.shape[1] // rows, seq, 0)
    for nn in range(S5_ROW // S5_TILE):
        cols = slice(nn * S5_TILE, (nn + 1) * S5_TILE)
        acc = _dot(sp_ref[...], wx_ref[:, cols])
        for kk in range(nn + 1):
            acc = acc + _dot(u_ref[0, :, kk * S5_TILE:(kk + 1) * S5_TILE], toe_ref[0, nn - kk])
        y_ref[0, :, cols] = acc.astype(y_ref.dtype)


def _s5_tables(lam_re, lam_im, b_re, b_im, c_re, c_im, log_dt, rows):
    lr = lam_re.astype(F32)
    li = lam_im.astype(F32)
    step = jnp.exp(log_dt.astype(F32))[:, None]
    mag = jnp.exp(lr * step)
    ar = mag * jnp.cos(li * step)
    ai = mag * jnp.sin(li * step)
    inv = 1.0 / (lr * lr + li * li)
    cr = ((ar - 1.0) * lr + ai * li) * inv
    ci = (ai * lr - (ar - 1.0) * li) * inv
    bbr = cr[..., None] * b_re - ci[..., None] * b_im
    bbi = cr[..., None] * b_im + ci[..., None] * b_re

    def apow(e):
        e = jnp.asarray(e, F32)[:, None, None]
        m = jnp.exp(lr * step * e)
        return m * jnp.cos(li * step * e), m * jnp.sin(li * step * e)

    sub = S5_SUB
    pr, pi = apow(np.arange(sub + 1))
    m_r = pr[:sub, :, :, None] * bbr - pi[:sub, :, :, None] * bbi
    m_i = pr[:sub, :, :, None] * bbi + pi[:sub, :, :, None] * bbr
    kern = (jnp.einsum('gon,tgni->tgoi', c_re, m_r)
            - jnp.einsum('gon,tgni->tgoi', c_im, m_i))
    eye = jnp.eye(S5_QG, dtype=F32)
    blk = lambda a: a.reshape(a.shape[0], S5_Q, S5_QG, *a.shape[2:])
    kern_bd = jnp.einsum('tqgoi,gh->tqgiho', blk(kern), eye).reshape(sub, S5_Q, LANES, LANES)
    kern_bd = jnp.concatenate([jnp.zeros_like(kern_bd[:1]), kern_bd], axis=0)
    steps_per_tile = S5_TILE // LANES
    d = np.arange(sub // steps_per_tile)[:, None, None]
    tau = steps_per_tile * d + np.arange(steps_per_tile)[None, None, :] - np.arange(steps_per_tile)[None, :, None]
    toe = kern_bd[tau + 1]
    toe = jnp.transpose(toe, (3, 0, 1, 4, 2, 5)).reshape(S5_Q, sub // steps_per_tile, S5_TILE, S5_TILE)
    er, ei = pr[sub - 1 - np.arange(sub)], pi[sub - 1 - np.arange(sub)]
    w_r = er[..., None] * bbr - ei[..., None] * bbi
    w_i = er[..., None] * bbi + ei[..., None] * bbr
    fold_in = lambda w: jnp.transpose(blk(w), (1, 0, 2, 4, 3)).reshape(S5_Q, S5_ROW, S5_STATE)
    wc = jnp.concatenate([fold_in(w_r), fold_in(w_i)], axis=2)
    qr, qi = pr[1:], pi[1:]
    x_r = c_re[None] * qr[:, :, None, :] - c_im[None] * qi[:, :, None, :]
    x_i = -(c_re[None] * qi[:, :, None, :] + c_im[None] * qr[:, :, None, :])
    fold_out = lambda w: jnp.transpose(blk(w), (1, 2, 4, 0, 3)).reshape(S5_Q, S5_QS, sub * S5_GROUP)
    xc = jnp.concatenate([fold_out(x_r), fold_out(x_i)], axis=1)
    n_steps = int(math.log2(rows))
    sr_, si_ = apow(sub * (2.0 ** np.arange(n_steps)))
    pw = jnp.stack([sr_, si_], axis=1).reshape(2 * n_steps, S5_Q, S5_QS)
    pw = jnp.transpose(pw, (1, 0, 2))
    pw = jnp.pad(pw, ((0, 0), (0, 2 * SUBLANES - 2 * n_steps), (0, 0)))
    return toe.astype(BF16), wc.astype(BF16), xc.astype(BF16), pw


def _s5_replicators():
    k = np.arange(2 * S5_STATE)[:, None]
    c = np.arange(2 * S5_QS)[None, :]
    rin = ((k // S5_STATE) == (c // S5_QS)) & ((k % S5_STATE) == (c % S5_STATE))
    r = np.arange(S5_SUB * S5_GROUP)[:, None]
    c = np.arange(S5_ROW)[None, :]
    rout = ((r // S5_GROUP) == (c // LANES)) & ((r % S5_GROUP) == (c % S5_GROUP))
    return jnp.asarray(rin, BF16), jnp.asarray(rout, BF16)


def _s5_scan(u5, tables, bsz, seq):
    toe, wc, xc, pw = tables
    rin, rout = _s5_replicators()
    rows = seq // S5_SUB
    nb = math.gcd(S5_SEQS, bsz)
    row_blk = pl.BlockSpec((1, nb * rows, S5_ROW), lambda q, j: (q, j, 0))
    per_q = lambda a: pl.BlockSpec((1,) + a.shape[1:], lambda q, j: (q,) + (0,) * (a.ndim - 1))
    const = lambda a: pl.BlockSpec(a.shape, lambda q, j: (0,) * a.ndim)
    return pl.pallas_call(
        functools.partial(_s5_kernel, rows=rows),
        grid=(S5_Q, bsz // nb),
        in_specs=[row_blk, per_q(toe), per_q(wc), per_q(xc), const(rin), const(rout), per_q(pw)],
        out_specs=row_blk,
        out_shape=jax.ShapeDtypeStruct(u5.shape, BF16),
        scratch_shapes=[pltpu.VMEM((S5_ROW, 2 * S5_QS), BF16),
                        pltpu.VMEM((2 * S5_QS, S5_ROW), BF16),
                        pltpu.VMEM((nb * rows, 2 * S5_QS), F32),
                        pltpu.VMEM((nb * rows, 2 * S5_QS), BF16)],
        compiler_params=_params("parallel", "arbitrary"),
        name="s5_scan",
    )(u5, toe, wc, xc, rin, rout, pw)


CONV_PAD = SUBLANES
SEQS_PER_STEP = 2


def _causal_conv(xpad_ref, x, w, b):
    rows = x.shape[0]
    xpad_ref[CONV_PAD:CONV_PAD + rows, :] = x
    k = w.shape[0]
    acc = b
    for j in range(k):
        s = CONV_PAD - (k - 1) + j
        acc = acc + w[j:j + 1, :] * xpad_ref[s:s + rows, :]
    xpad_ref[0:CONV_PAD, :] = xpad_ref[rows:rows + CONV_PAD, :]
    return acc


def _ssd_kernel(z_ref, xbc_ref, dt_ref, cw_ref, cb_ref, dtb_ref, a_ref, d_ref, nw_ref, tri_ref,
                o_ref, xpad_ref, state_ref, y_ref):
    @pl.when(pl.program_id(1) == 0)
    def _():
        state_ref[...] = jnp.zeros(state_ref.shape, F32)
        xpad_ref[:, 0:CONV_PAD, :] = jnp.zeros((xpad_ref.shape[0], CONV_PAD, xpad_ref.shape[2]), F32)

    for s in range(z_ref.shape[0]):
        _ssd_chunk(z_ref.at[s], xbc_ref.at[s], dt_ref.at[s], cw_ref, cb_ref, dtb_ref, a_ref, d_ref,
                   nw_ref, tri_ref, o_ref.at[s], xpad_ref.at[s], state_ref.at[s], y_ref.at[s])


def _ssd_chunk(z_ref, xbc_ref, dt_ref, cw_ref, cb_ref, dtb_ref, a_ref, d_ref, nw_ref, tri_ref,
               o_ref, xpad_ref, state_ref, y_ref):
    xbc = _silu(_causal_conv(xpad_ref, xbc_ref[...].astype(F32), cw_ref[...], cb_ref[...]))
    xs_t = xbc[:, :SSD_WIDTH].T
    bs = xbc[:, SSD_WIDTH:SSD_WIDTH + LANES]
    cs_t = xbc[:, SSD_WIDTH + LANES:].T.astype(BF16)
    lane = lax.broadcasted_iota(jnp.int32, bs.shape, 1)
    dt = _softplus(dt_ref[...].astype(F32) + dtb_ref[...])
    acum = _split_dot_left(tri_ref[...], dt * a_ref[...])
    dt_t = dt.T
    acum_t = acum.T
    later = (lax.broadcasted_iota(jnp.int32, (CHUNK, CHUNK), 1)
             >= lax.broadcasted_iota(jnp.int32, (CHUNK, CHUNK), 0))
    for g in range(SSD_GROUPS):
        in_group = (lane >= g * SSD_STATE) & (lane < (g + 1) * SSD_STATE)
        b_g = jnp.where(in_group, bs, 0.0).astype(BF16)
        cb_t = _dot(b_g, cs_t)
        for r in range(SSD_HEADS // SSD_GROUPS):
            hd = g * (SSD_HEADS // SSD_GROUPS) + r
            x_t = xs_t[hd * SSD_HEAD_DIM:(hd + 1) * SSD_HEAD_DIM, :]
            a_row = acum_t[hd:hd + 1, :]
            a_col = acum[:, hd:hd + 1]
            last = acum_t[hd:hd + 1, CHUNK - 1:CHUNK]
            l_t = jnp.exp(jnp.where(later, a_row - a_col, -jnp.inf))
            xdt_t = x_t * dt_t[hd:hd + 1, :]
            y_t = _dot(xdt_t.astype(BF16), (cb_t * l_t).astype(BF16))
            new = _dot((xdt_t * jnp.exp(last - a_row)).astype(BF16), b_g)
            prev = state_ref[hd]
            y_t = y_t + _dot(prev.astype(BF16), cs_t) * jnp.exp(a_row)
            state_ref[hd] = prev * jnp.exp(last) + new
            y_ref[hd * SSD_HEAD_DIM:(hd + 1) * SSD_HEAD_DIM, :] = y_t + d_ref[:, hd:hd + 1] * x_t
    y = y_ref[...].T * _silu(z_ref[...].astype(F32))
    o_ref[...] = _rms(y, nw_ref[...]).astype(o_ref.dtype)


def _ssd(z, xbc, dt, conv_w, conv_b, dt_bias, a_log, d_skip, norm_w):
    bsz, seq, _ = z.shape
    nc = seq // CHUNK
    ns = math.gcd(SEQS_PER_STEP, bsz)

    def lane_pad(v):
        return jnp.pad(v.astype(F32).reshape(1, -1), ((0, 0), (0, LANES - v.shape[-1])))

    tri = jnp.asarray(np.tril(np.ones((CHUNK, CHUNK), np.float32)), BF16)
    blk = lambda w: pl.BlockSpec((ns, CHUNK, w), lambda b, c: (b, c, 0))
    return pl.pallas_call(
        _ssd_kernel,
        grid=(bsz // ns, nc),
        in_specs=[blk(SSD_WIDTH), blk(SSD_CONV_DIM), blk(LANES),
                  _const_spec((SSD_CONV, SSD_CONV_DIM)), _const_spec((1, SSD_CONV_DIM)),
                  _const_spec((1, LANES)), _const_spec((1, LANES)), _const_spec((1, LANES)),
                  _const_spec((1, SSD_WIDTH)), _const_spec((CHUNK, CHUNK))],
        out_specs=blk(SSD_WIDTH),
        out_shape=jax.ShapeDtypeStruct((bsz, seq, SSD_WIDTH), BF16),
        scratch_shapes=[pltpu.VMEM((ns, CHUNK + CONV_PAD, SSD_CONV_DIM), F32),
                        pltpu.VMEM((ns, SSD_HEADS, SSD_HEAD_DIM, SSD_GROUPS * SSD_STATE), F32),
                        pltpu.VMEM((ns, SSD_WIDTH, CHUNK), F32)],
        compiler_params=_params("parallel", "arbitrary"),
        name="ssd",
    )(z, xbc, dt, conv_w.astype(F32), conv_b.astype(F32).reshape(1, -1),
      lane_pad(dt_bias), lane_pad(-jnp.exp(a_log.astype(F32))), lane_pad(d_skip),
      norm_w.astype(F32).reshape(1, -1), tri)


GN_TILE = 256


def _head_means(x, avg_bf16):
    hi = x.astype(BF16)
    lo = (x - hi.astype(F32)).astype(BF16)
    out = []
    for c in range(0, RET_WIDTH, GN_TILE):
        out.append(_dot(hi[:, c:c + GN_TILE], avg_bf16) + _dot(lo[:, c:c + GN_TILE], avg_bf16))
    return jnp.concatenate(out, axis=1)


def _retention_tables():
    h = np.arange(RET_HEADS, dtype=np.float64)
    log_gamma = np.log1p(-np.exp2(-5.0 - h))
    idx = np.arange(CHUNK, dtype=np.float64)
    diff = idx[:, None] - idx[None, :]
    dmat = np.where(diff >= 0, np.exp(np.maximum(diff, 0.0)[None] * log_gamma[:, None, None]), 0.0)
    k_decay = np.exp((CHUNK - 1.0 - idx)[:, None] * log_gamma)
    q_decay = np.exp((idx + 1.0)[:, None] * log_gamma)
    c_decay = np.exp(CHUNK * log_gamma)[None, :]
    rep = lambda a: np.repeat(a, RET_HEAD_DIM, axis=1).astype(np.float32)
    avg = np.kron(np.eye(GN_TILE // RET_HEAD_DIM),
                  np.full((RET_HEAD_DIM, RET_HEAD_DIM), 1.0 / RET_HEAD_DIM))
    return (dmat.astype(np.float32), rep(k_decay * RET_HEAD_DIM ** -0.5), rep(q_decay),
            rep(c_decay), avg.astype(np.float32))


def _rope(x, cos, sin_signed):
    half = RET_HEAD_DIM // 2
    width = x.shape[1]
    fwd = pltpu.roll(x, half, 1)
    bwd = pltpu.roll(x, width - half, 1)
    lane = lax.broadcasted_iota(jnp.int32, x.shape, 1)
    swapped = jnp.where((lane % RET_HEAD_DIM) < half, bwd, fwd)
    return x * cos + swapped * sin_signed


def _retention_kernel(q_ref, k_ref, v_ref, g_ref, cos_ref, sin_ref, dmat_ref, kdec_ref, qdec_ref,
                      cdec_ref, avg_ref, gn_ref, o_ref, state_ref):
    @pl.when(pl.program_id(1) == 0)
    def _():
        state_ref[...] = jnp.zeros(state_ref.shape, F32)

    for s in range(q_ref.shape[0]):
        _retention_chunk(q_ref.at[s], k_ref.at[s], v_ref.at[s], g_ref.at[s], cos_ref.at[s],
                         sin_ref.at[s], dmat_ref, kdec_ref, qdec_ref, cdec_ref, avg_ref, gn_ref,
                         o_ref.at[s], state_ref.at[s])


def _retention_chunk(q_ref, k_ref, v_ref, g_ref, cos_ref, sin_ref, dmat_ref, kdec_ref, qdec_ref,
                     cdec_ref, avg_ref, gn_ref, o_ref, state_ref):
    reps = RET_WIDTH // LANES
    cos = jnp.tile(cos_ref[...], (1, reps))
    sin = jnp.tile(sin_ref[...], (1, reps))
    q = _rope(q_ref[...].astype(F32), cos, sin)
    k = _rope(k_ref[...].astype(F32), cos, sin)
    v = v_ref[...].astype(F32)
    kd_t = (k * kdec_ref[...]).T.astype(BF16)
    k_tb = (k.T * (RET_HEAD_DIM ** -0.5)).astype(BF16)
    qd = qdec_ref[...]
    cd = cdec_ref[...]
    first = lax.broadcasted_iota(jnp.int32, (CHUNK, LANES), 1) < RET_HEAD_DIM
    r = lax.broadcasted_iota(jnp.int32, (LANES, LANES), 0) < RET_HEAD_DIM
    c = lax.broadcasted_iota(jnp.int32, (LANES, LANES), 1) < RET_HEAD_DIM
    same_head = r == c
    pairs = []
    for p in range(RET_HEADS // 2):
        sl = slice(p * LANES, (p + 1) * LANES)
        q_p, v_p, k_tp = q[:, sl], v[:, sl], k_tb[sl, :]
        prev = state_ref[p]
        y = _dot(q_p.astype(BF16), prev.astype(BF16)) * qd[:, sl]
        for half in range(2):
            keep = first if half == 0 else jnp.logical_not(first)
            q_h = jnp.where(keep, q_p, 0.0).astype(BF16)
            v_h = jnp.where(keep, v_p, 0.0).astype(BF16)
            scores = _dot(q_h, k_tp) * dmat_ref[2 * p + half]
            y = y + _dot(scores.astype(BF16), v_h)
        state_ref[p] = prev * cd[:, sl] + jnp.where(same_head, _dot(kd_t[sl, :], v_p.astype(BF16)), 0.0)
        pairs.append(y)
    y = jnp.concatenate(pairs, axis=1)
    mu = _head_means(y, avg_ref[...])
    yc = y - mu
    var = _head_means(yc * yc, avg_ref[...])
    yn = yc * lax.rsqrt(var + EPS) * gn_ref[...]
    o_ref[...] = (_silu(g_ref[...].astype(F32)) * yn).astype(o_ref.dtype)


def _retention(q, k, v, g, cos2, sin2, gn_w):
    bsz, seq, _ = q.shape
    nc = seq // CHUNK
    ns = math.gcd(SEQS_PER_STEP, bsz)
    dmat, kdec, qdec, cdec, avg = (jnp.asarray(a) for a in _retention_tables())
    avg = avg.astype(BF16)
    blk = lambda w: pl.BlockSpec((ns, CHUNK, w), lambda b, c: (b, c, 0))
    return pl.pallas_call(
        _retention_kernel,
        grid=(bsz // ns, nc),
        in_specs=[blk(RET_WIDTH)] * 4 + [blk(LANES)] * 2 + [
            _const_spec(dmat.shape), _const_spec(kdec.shape), _const_spec(qdec.shape),
            _const_spec(cdec.shape), _const_spec(avg.shape), _const_spec((1, RET_WIDTH))],
        out_specs=blk(RET_WIDTH),
        out_shape=jax.ShapeDtypeStruct((bsz, seq, RET_WIDTH), BF16),
        scratch_shapes=[pltpu.VMEM((ns, RET_HEADS // 2, LANES, LANES), F32)],
        compiler_params=_params("parallel", "arbitrary"),
        name="retention",
    )(q, k, v, g, cos2, sin2, dmat, kdec, qdec, cdec, avg, gn_w.astype(F32).reshape(1, -1))


def _rope_tables(positions):
    half = RET_HEAD_DIM // 2
    inv_freq = ROPE_BASE ** (-jnp.arange(half, dtype=F32) / half)
    ang = positions.astype(F32)[..., None] * inv_freq
    cos = jnp.cos(ang)
    sin = jnp.sin(ang)
    cos2 = jnp.concatenate([cos, cos, cos, cos], axis=-1)
    sin2 = jnp.concatenate([-sin, sin, -sin, sin], axis=-1)
    return cos2, sin2


def _lru_kernel(x_ref, gate_ref, cw_ref, cb_ref, w_ref, bias_ref, lamc_ref, o_ref, xpad_ref, h_ref):
    c = pl.program_id(1)

    @pl.when(c == 0)
    def _():
        h_ref[...] = jnp.zeros(h_ref.shape, F32)
        xpad_ref[0:CONV_PAD, :] = jnp.zeros((CONV_PAD, xpad_ref.shape[1]), F32)

    rows = x_ref.shape[1]
    xc = _causal_conv(xpad_ref, x_ref[0].astype(F32), cw_ref[...], cb_ref[...])
    ri = _sigmoid(_dot(xc.astype(BF16), w_ref[...]) + bias_ref[...])
    log_a = lamc_ref[...] * ri[:, :LRU_WIDTH]
    a_all = jnp.exp(log_a)
    mult = jnp.sqrt(jnp.maximum(1.0 - jnp.exp(2.0 * log_a), 0.0))
    b_all = mult * ri[:, LRU_WIDTH:] * xc
    gate = _gelu(gate_ref[0].astype(F32))
    n_steps = int(math.log2(rows))
    row0 = lax.broadcasted_iota(jnp.int32, (rows, LANES), 0) == 0
    for j in range(LRU_WIDTH // LANES):
        sl = slice(j * LANES, (j + 1) * LANES)
        a = a_all[:, sl]
        b = b_all[:, sl]
        b = b + jnp.where(row0, a * h_ref[0:1, sl], 0.0)
        for s in range(n_steps):
            k = 1 << s
            b = b + a * _shift_rows(b, k)
            if s + 1 < n_steps:
                a = a * _shift_rows(a, k, 1.0)
        h_ref[0:1, sl] = b[rows - 1:rows, :]
        o_ref[0, :, sl] = (b * gate[:, sl]).astype(o_ref.dtype)


def _block_diag(w):
    nb, n, _ = w.shape
    eye = jnp.eye(nb, dtype=w.dtype)
    return jnp.einsum('bij,bc->bicj', w, eye).reshape(nb * n, nb * n)


def _lru(x, gate, conv_w, conv_b, wa, ba, wx, bx, lam):
    bsz, seq, _ = x.shape
    rows = min(LRU_ROWS, seq)
    w = jnp.concatenate([_block_diag(wa), _block_diag(wx)], axis=1).astype(BF16)
    bias = jnp.concatenate([ba, bx]).astype(F32).reshape(1, -1)
    lamc = (-LRU_C * jax.nn.softplus(-lam.astype(F32))).reshape(1, -1)
    blk = lambda wd: pl.BlockSpec((1, rows, wd), lambda b, c: (b, c, 0))
    return pl.pallas_call(
        _lru_kernel,
        grid=(bsz, seq // rows),
        in_specs=[blk(LRU_WIDTH), blk(LRU_WIDTH),
                  _const_spec((LRU_CONV, LRU_WIDTH)), _const_spec((1, LRU_WIDTH)),
                  _const_spec((LRU_WIDTH, 2 * LRU_WIDTH)), _const_spec((1, 2 * LRU_WIDTH)),
                  _const_spec((1, LRU_WIDTH))],
        out_specs=blk(LRU_WIDTH),
        out_shape=jax.ShapeDtypeStruct((bsz, seq, LRU_WIDTH), BF16),
        scratch_shapes=[pltpu.VMEM((rows + CONV_PAD, LRU_WIDTH), F32),
                        pltpu.VMEM((SUBLANES, LRU_WIDTH), F32)],
        compiler_params=_params("parallel", "arbitrary"),
        name="rglru",
    )(x, gate, conv_w.astype(F32), conv_b.astype(F32).reshape(1, -1), w, bias, lamc)


def _merge_kernel(h_ref, nw_ref, u_ref, ys5_ref, b1_ref, b2_ref, b3_ref, wg_ref, bg_ref, d_ref,
                  wglu_ref, wb_ref, wo_ref, o_ref, yscr_ref):
    folded = yscr_ref.shape[1] // S5_SUB
    for q in range(S5_Q):
        z = ys5_ref[q].astype(F32) + d_ref[q] * u_ref[q].astype(F32)
        for t in range(S5_SUB):
            yscr_ref[q, pl.ds(t, folded, stride=S5_SUB), :] = z[:, t * LANES:(t + 1) * LANES]
    y = _gelu(jnp.concatenate([yscr_ref[q] for q in range(S5_Q)], axis=1))
    b0 = (y * _sigmoid(_dot(y.astype(BF16), wglu_ref[...]))).astype(BF16)
    branches = (b0, b1_ref[...], b2_ref[...], b3_ref[...])
    h = h_ref[...]
    xb = _rms(h, nw_ref[...]).astype(BF16)
    merged = None
    for i, br in enumerate(branches):
        sl = slice(i * D_MODEL, (i + 1) * D_MODEL)
        th = jnp.tanh(_dot(xb, wg_ref[:, sl]) + bg_ref[:, sl])
        p = _dot(br, wb_ref[i])
        term = p + th * p
        merged = term if merged is None else merged + term
    o_ref[...] = h + _dot(merged.astype(BF16), wo_ref[...])


def _merge(h, norm_w, u, ys5, b1, b2, b3, w_gate_half, b_gate, s5_d, w_glu, w_branch, w_out):
    t = h.shape[0]
    tm = min(ROW_TILE, t)
    row = lambda w: pl.BlockSpec((tm, w), lambda i: (i, 0))
    s5_blk = pl.BlockSpec((S5_Q, tm // S5_SUB, S5_ROW), lambda i: (0, i, 0))
    d_fold = jnp.tile(s5_d.astype(F32).reshape(S5_Q, 1, LANES), (1, 1, S5_SUB))
    return pl.pallas_call(
        _merge_kernel,
        grid=(t // tm,),
        in_specs=[row(D_MODEL), _const_spec((1, D_MODEL)), s5_blk, s5_blk] + [row(512)] * 3 + [
                  _const_spec((D_MODEL, N_BRANCH * D_MODEL)),
                  _const_spec((1, N_BRANCH * D_MODEL)), _const_spec((S5_Q, 1, S5_ROW)),
                  _const_spec((S5_WIDTH, S5_WIDTH)),
                  _const_spec((N_BRANCH, 512, D_MODEL)), _const_spec((D_MODEL, D_MODEL))],
        out_specs=row(D_MODEL),
        out_shape=jax.ShapeDtypeStruct((t, D_MODEL), F32),
        scratch_shapes=[pltpu.VMEM((S5_Q, tm, LANES), F32)],
        compiler_params=_params("parallel"),
        name="merge",
    )(h, norm_w.astype(F32).reshape(1, -1), u, ys5, b1, b2, b3, w_gate_half,
      0.5 * b_gate.astype(F32).reshape(1, -1), d_fold, w_glu.astype(BF16), w_branch.astype(BF16),
      (0.5 * w_out).astype(BF16))


def _kv_kernel(mem_ref, nw_ref, w_ref, k_ref, v_ref):
    mb = _rms(mem_ref[0], nw_ref[...]).astype(BF16)
    for c in range(0, D_MODEL, DOT_COLS):
        k_ref[0, :, c:c + DOT_COLS] = _dot(mb, w_ref[:, c:c + DOT_COLS]).astype(BF16)
        v_ref[0, :, c:c + DOT_COLS] = _dot(
            mb, w_ref[:, D_MODEL + c:D_MODEL + c + DOT_COLS]).astype(BF16)


def _kv(mem, norm_w, wk, wv):
    bsz, n_mem, _ = mem.shape
    w = jnp.concatenate([wk, wv], axis=1).astype(BF16)
    blk = pl.BlockSpec((1, n_mem, D_MODEL), lambda b: (b, 0, 0))
    return pl.pallas_call(
        _kv_kernel,
        grid=(bsz,),
        in_specs=[blk, _const_spec((1, D_MODEL)), _const_spec((D_MODEL, 2 * D_MODEL))],
        out_specs=[blk, blk],
        out_shape=[jax.ShapeDtypeStruct((bsz, n_mem, D_MODEL), BF16)] * 2,
        compiler_params=_params("parallel"),
        name="xattn_kv",
    )(mem, norm_w.astype(F32).reshape(1, -1), w)


def _xattn_kernel(h_ref, nw_ref, wq_ref, k_ref, v_ref, wo_ref, o_ref, att_ref):
    h = h_ref[0]
    xb = _rms(h, nw_ref[...]).astype(BF16)
    scale = XA_HEAD_DIM ** -0.5
    for hd in range(XA_HEADS):
        sl = slice(hd * XA_HEAD_DIM, (hd + 1) * XA_HEAD_DIM)
        q = _dot(xb, wq_ref[:, sl]).astype(BF16)
        s = _dot_nt(q, k_ref[0, :, sl]) * scale
        p = jnp.exp(s - jnp.max(s, axis=-1, keepdims=True))
        denom = jnp.sum(p, axis=-1, keepdims=True)
        att_ref[:, sl] = (_dot(p.astype(BF16), v_ref[0, :, sl]) / denom).astype(BF16)
    o_ref[0] = h + _dot(att_ref[...], wo_ref[...])


def _xattn(h3, norm_w, wq, k, v, wo):
    bsz, seq, _ = h3.shape
    tq = min(ROW_TILE, seq)
    n_mem = k.shape[1]
    row = pl.BlockSpec((1, tq, D_MODEL), lambda b, i: (b, i, 0))
    kv = pl.BlockSpec((1, n_mem, D_MODEL), lambda b, i: (b, 0, 0))
    return pl.pallas_call(
        _xattn_kernel,
        grid=(bsz, seq // tq),
        in_specs=[row, _const_spec((1, D_MODEL)), _const_spec((D_MODEL, D_MODEL)), kv, kv,
                  _const_spec((D_MODEL, D_MODEL))],
        out_specs=row,
        out_shape=jax.ShapeDtypeStruct((bsz, seq, D_MODEL), F32),
        scratch_shapes=[pltpu.VMEM((tq, D_MODEL), BF16)],
        compiler_params=_params("parallel", "parallel"),
        name="xattn",
    )(h3, norm_w.astype(F32).reshape(1, -1), wq.astype(BF16), k, v, wo.astype(BF16))


FF_COLS = 256


def _swiglu_acc(xb, w1_ref, w3_ref, w2_ref, width, lead=()):
    acc = None
    for c in range(0, width, FF_COLS):
        a = _dot(xb, w1_ref[lead + (slice(None), slice(c, c + FF_COLS))])
        b = _dot(xb, w3_ref[lead + (slice(None), slice(c, c + FF_COLS))])
        g = (_silu(a) * b).astype(BF16)
        term = _dot(g, w2_ref[lead + (slice(c, c + FF_COLS), slice(None))])
        acc = term if acc is None else acc + term
    return acc


def _ffn_kernel(h_ref, nw_ref, w1_ref, w3_ref, w2_ref, o_ref):
    h = h_ref[...]
    xb = _rms(h, nw_ref[...]).astype(BF16)
    o_ref[...] = h + _swiglu_acc(xb, w1_ref, w3_ref, w2_ref, D_FF)


def _ffn(h, norm_w, w1, w3, w2):
    t = h.shape[0]
    tm = min(ROW_TILE, t)
    row = pl.BlockSpec((tm, D_MODEL), lambda i: (i, 0))
    return pl.pallas_call(
        _ffn_kernel,
        grid=(t // tm,),
        in_specs=[row, _const_spec((1, D_MODEL)), _const_spec((D_MODEL, D_FF)),
                  _const_spec((D_MODEL, D_FF)), _const_spec((D_FF, D_MODEL))],
        out_specs=row,
        out_shape=jax.ShapeDtypeStruct((t, D_MODEL), F32),
        compiler_params=_params("parallel"),
        name="ffn",
    )(h, norm_w.astype(F32).reshape(1, -1), w1.astype(BF16), w3.astype(BF16), w2.astype(BF16))


MOE_TILE = 512
TOP_K = 2
DISPATCH_ROWS = 512
COMBINE_ROWS = 512
DMA_UNROLL = 8


def _sorted_tiles(t):
    return pl.cdiv(TOP_K * t, MOE_TILE) + N_EXPERTS


def _router_kernel(h_ref, nw_ref, wr_ref, xn_ref, member_ref, wsel_ref, esel_ref):
    xn = _rms(h_ref[...], nw_ref[...])
    xn_ref[...] = xn
    logits = lax.dot_general(xn, wr_ref[...], (((1,), (0,)), ((), ())),
                             precision=lax.Precision.HIGHEST, preferred_element_type=F32)
    lane = lax.broadcasted_iota(jnp.int32, logits.shape, 1)
    logits = jnp.where(lane < N_EXPERTS, logits, -jnp.inf)
    m1 = jnp.max(logits, axis=-1, keepdims=True)
    i1 = jnp.min(jnp.where(logits == m1, lane, LANES), axis=-1, keepdims=True)
    rest = jnp.where(lane == i1, -jnp.inf, logits)
    m2 = jnp.max(rest, axis=-1, keepdims=True)
    i2 = jnp.min(jnp.where(rest == m2, lane, LANES), axis=-1, keepdims=True)
    e2 = jnp.exp(m2 - m1)
    w1 = 1.0 / (1.0 + e2)
    w2 = e2 / (1.0 + e2)
    member_ref[...] = jnp.where(lane == i1, 1.0, jnp.where(lane == i2, 1.0, 0.0)).astype(BF16)
    wsel_ref[...] = jnp.where(lane == 0, w1, jnp.where(lane == 1, w2, 0.0))
    esel_ref[...] = jnp.where(lane == 0, i1, jnp.where(lane == 1, i2, 0))


def _router(h, norm_w, w_router):
    t = h.shape[0]
    tm = min(ROW_TILE, t)
    wr = jnp.pad(w_router.astype(F32), ((0, 0), (0, LANES - N_EXPERTS)))
    row = lambda w: pl.BlockSpec((tm, w), lambda i: (i, 0))
    return pl.pallas_call(
        _router_kernel,
        grid=(t // tm,),
        in_specs=[row(D_MODEL), _const_spec((1, D_MODEL)), _const_spec((D_MODEL, LANES))],
        out_specs=[row(D_MODEL), row(LANES), row(LANES), row(LANES)],
        out_shape=[jax.ShapeDtypeStruct((t, D_MODEL), F32), jax.ShapeDtypeStruct((t, LANES), BF16),
                   jax.ShapeDtypeStruct((t, LANES), F32), jax.ShapeDtypeStruct((t, LANES), jnp.int32)],
        compiler_params=_params("parallel"),
        name="moe_router",
    )(h, norm_w.astype(F32).reshape(1, -1), wr)


def _positions_kernel(member_ref, esel_ref, pos_ref, meta_ref, cnt_ref, carry_ref, off_ref):
    phase = pl.program_id(0)
    i = pl.program_id(1)
    m = member_ref[...]
    tp = m.shape[0]
    col_sum = _dot(jnp.ones((SUBLANES, tp), BF16), m)

    @pl.when((phase == 0) & (i == 0))
    def _():
        cnt_ref[...] = jnp.zeros(cnt_ref.shape, F32)

    @pl.when(phase == 0)
    def _():
        cnt_ref[...] += col_sum

    @pl.when((phase == 1) & (i == 0))
    def _():
        tiles = jnp.floor((cnt_ref[...] + (MOE_TILE - 1.0)) * (1.0 / MOE_TILE))
        r = lax.broadcasted_iota(jnp.int32, (LANES, LANES), 0)
        c = lax.broadcasted_iota(jnp.int32, (LANES, LANES), 1)
        before = jnp.where(r < c, 1.0, 0.0).astype(BF16)
        first_tile = _dot(tiles.astype(BF16), before)
        off_ref[...] = first_tile * MOE_TILE
        carry_ref[...] = jnp.zeros(carry_ref.shape, F32)
        row = lax.broadcasted_iota(jnp.int32, (SUBLANES, LANES), 0)
        meta_ref[...] = jnp.where(row == 0, first_tile,
                                  jnp.where(row == 1, tiles, cnt_ref[...])).astype(jnp.int32)

    @pl.when(phase == 1)
    def _():
        r = lax.broadcasted_iota(jnp.int32, (tp, tp), 0)
        c = lax.broadcasted_iota(jnp.int32, (tp, tp), 1)
        earlier = jnp.where(r > c, 1.0, 0.0).astype(BF16)
        posm = off_ref[0:1, :] + carry_ref[0:1, :] + _dot(earlier, m)
        lane = lax.broadcasted_iota(jnp.int32, posm.shape, 1)
        e = esel_ref[...]
        p0 = jnp.sum(jnp.where(lane == e[:, 0:1], posm, 0.0), axis=-1, keepdims=True)
        p1 = jnp.sum(jnp.where(lane == e[:, 1:2], posm, 0.0), axis=-1, keepdims=True)
        pos_ref[...] = jnp.where(lane == 0, p0, jnp.where(lane == 1, p1, 0.0)).astype(jnp.int32)
        carry_ref[...] += col_sum


def _positions(member, esel):
    t = member.shape[0]
    tp = min(ROW_TILE, t)
    return pl.pallas_call(
        _positions_kernel,
        grid=(2, t // tp),
        in_specs=[pl.BlockSpec((tp, LANES), lambda p, i: (i, 0)),
                  pl.BlockSpec((tp, LANES), lambda p, i: (i * p, 0))],
        out_specs=[pl.BlockSpec((tp, LANES), lambda p, i: (i * p, 0)),
                   pl.BlockSpec((SUBLANES, LANES), lambda p, i: (0, 0))],
        out_shape=[jax.ShapeDtypeStruct((t, LANES), jnp.int32),
                   jax.ShapeDtypeStruct((SUBLANES, LANES), jnp.int32)],
        scratch_shapes=[pltpu.VMEM((SUBLANES, LANES), F32)] * 3,
        compiler_params=_params("arbitrary", "arbitrary"),
        name="moe_positions",
    )(member, esel)


def _row_copy(src_ref, src_row, dst_ref, dst_row, sem):
    return pltpu.make_async_copy(src_ref.at[pl.ds(src_row, 1)], dst_ref.at[pl.ds(dst_row, 1)], sem)


def _dispatch_kernel(pad_lo_ref, pad_hi_ref, pos_ref, xn_ref, xs_ref, zero_ref, sem):
    rows = xn_ref.shape[0]

    @pl.when(pl.program_id(0) == 0)
    def _():
        zero_ref[...] = jnp.zeros(zero_ref.shape, F32)
        for e in range(N_EXPERTS):
            lo, hi = pad_lo_ref[e], pad_hi_ref[e]

            def fill(r, carry):
                _row_copy(zero_ref, 0, xs_ref, r, sem).start()
                return carry

            def filled(r, carry):
                _row_copy(zero_ref, 0, xs_ref, r, sem).wait()
                return carry

            lax.fori_loop(lo, hi, fill, 0)
            lax.fori_loop(lo, hi, filled, 0)

        def tail_copy(j):
            return pltpu.make_async_copy(
                zero_ref, xs_ref.at[pl.ds(pl.multiple_of(j * SUBLANES, SUBLANES), SUBLANES)], sem)

        def tail_fill(j, carry):
            tail_copy(j).start()
            return carry

        def tail_filled(j, carry):
            tail_copy(j).wait()
            return carry

        first = lax.shift_right_logical(pad_hi_ref[N_EXPERTS - 1], int(math.log2(SUBLANES)))
        lax.fori_loop(first, xs_ref.shape[0] // SUBLANES, tail_fill, 0)
        lax.fori_loop(first, xs_ref.shape[0] // SUBLANES, tail_filled, 0)

    def copies(r):
        return [_row_copy(xn_ref, r, xs_ref, pos_ref[0, k, r], sem) for k in range(TOP_K)]

    def issue(r, carry):
        for k, cp in enumerate(copies(r)):
            cp.start(priority=k)
        return carry

    def drain(r, carry):
        for cp in copies(r):
            cp.wait()
        return carry

    lax.fori_loop(0, rows, issue, 0, unroll=DMA_UNROLL)
    lax.fori_loop(0, rows, drain, 0, unroll=DMA_UNROLL)


def _slot_major(pos, rows):
    t = pos.shape[0]
    return jnp.transpose(pos[:, :TOP_K].reshape(t // rows, rows, TOP_K), (0, 2, 1))


def _dispatch(xn, pos, pad_lo, pad_hi, n_sorted):
    t = xn.shape[0]
    rows = min(DISPATCH_ROWS, t)
    return pl.pallas_call(
        _dispatch_kernel,
        grid_spec=pltpu.PrefetchScalarGridSpec(
            num_scalar_prefetch=2,
            grid=(t // rows,),
            in_specs=[pl.BlockSpec((1, TOP_K, rows), lambda i, lo, hi: (i, 0, 0),
                                   memory_space=pltpu.SMEM),
                      pl.BlockSpec((rows, D_MODEL), lambda i, lo, hi: (i, 0))],
            out_specs=pl.BlockSpec(memory_space=pl.ANY),
            scratch_shapes=[pltpu.VMEM((SUBLANES, D_MODEL), F32), pltpu.SemaphoreType.DMA(())]),
        out_shape=jax.ShapeDtypeStruct((n_sorted, D_MODEL), F32),
        compiler_params=_params("arbitrary"),
        name="moe_dispatch",
    )(pad_lo, pad_hi, _slot_major(pos, rows), xn)


MOE_FF_SPLIT = 2
MOE_FF_BLOCK = D_FF_EXPERT // MOE_FF_SPLIT


def _experts_kernel(tile_expert_ref, n_used_ref, xs_ref, w1_ref, w3_ref, w2_ref, y_ref):
    del tile_expert_ref
    used = pl.program_id(0) < n_used_ref[0]
    f = pl.program_id(1)

    def part():
        xb = xs_ref[...].astype(BF16)
        return _swiglu_acc(xb, w1_ref, w3_ref, w2_ref, MOE_FF_BLOCK, lead=(0,))

    @pl.when(used & (f == 0))
    def _():
        y_ref[...] = part()

    @pl.when(used & (f > 0))
    def _():
        y_ref[...] += part()

    @pl.when(jnp.logical_not(used) & (f == 0))
    def _():
        y_ref[...] = jnp.zeros(y_ref.shape, F32)


def _experts(xs, tile_expert, n_used, w1, w3, w2):
    n_tiles = xs.shape[0] // MOE_TILE
    row = pl.BlockSpec((MOE_TILE, D_MODEL), lambda i, f, te, nu: (i, 0))
    row_in = pl.BlockSpec((MOE_TILE, D_MODEL), lambda i, f, te, nu: (jnp.minimum(i, nu[0] - 1), 0))
    ff = lambda i, f, nu: jnp.where(i < nu[0], f, MOE_FF_SPLIT - 1)
    w_up = pl.BlockSpec((1, D_MODEL, MOE_FF_BLOCK), lambda i, f, te, nu: (te[i], 0, ff(i, f, nu)))
    w_down = pl.BlockSpec((1, MOE_FF_BLOCK, D_MODEL), lambda i, f, te, nu: (te[i], ff(i, f, nu), 0))
    return pl.pallas_call(
        _experts_kernel,
        grid_spec=pltpu.PrefetchScalarGridSpec(
            num_scalar_prefetch=2,
            grid=(n_tiles, MOE_FF_SPLIT),
            in_specs=[row_in, w_up, w_up, w_down],
            out_specs=row),
        out_shape=jax.ShapeDtypeStruct(xs.shape, F32),
        compiler_params=_params("arbitrary", "arbitrary"),
        name="moe_experts",
    )(tile_expert, n_used, xs, w1.astype(BF16), w3.astype(BF16), w2.astype(BF16))


def _combine_kernel(pos_ref, h_ref, wsel_ref, fw_ref, y_ref, o_ref, ybuf_ref, sem, *, final_norm):
    rows = h_ref.shape[0]

    def copies(r):
        return [_row_copy(y_ref, pos_ref[0, k, r], ybuf_ref.at[k], r, sem) for k in range(TOP_K)]

    def issue(r, carry):
        for k, cp in enumerate(copies(r)):
            cp.start(priority=k)
        return carry

    def drain(r, carry):
        for cp in copies(r):
            cp.wait()
        return carry

    lax.fori_loop(0, rows, issue, 0, unroll=DMA_UNROLL)
    lax.fori_loop(0, rows, drain, 0, unroll=DMA_UNROLL)
    w = wsel_ref[...]
    out = h_ref[...] + w[:, 0:1] * ybuf_ref[0] + w[:, 1:2] * ybuf_ref[1]
    if final_norm:
        out = _rms(out, fw_ref[...])
    o_ref[...] = out


def _combine(h, wsel, pos, y_sorted, final_w, final_norm):
    t = h.shape[0]
    rows = min(COMBINE_ROWS, t)
    return pl.pallas_call(
        functools.partial(_combine_kernel, final_norm=final_norm),
        grid=(t // rows,),
        in_specs=[pl.BlockSpec((1, TOP_K, rows), lambda i: (i, 0, 0), memory_space=pltpu.SMEM),
                  pl.BlockSpec((rows, D_MODEL), lambda i: (i, 0)),
                  pl.BlockSpec((rows, LANES), lambda i: (i, 0)),
                  _const_spec((1, D_MODEL)),
                  pl.BlockSpec(memory_space=pl.ANY)],
        out_specs=pl.BlockSpec((rows, D_MODEL), lambda i: (i, 0)),
        out_shape=jax.ShapeDtypeStruct((t, D_MODEL), F32),
        scratch_shapes=[pltpu.VMEM((TOP_K, rows, D_MODEL), F32), pltpu.SemaphoreType.DMA(())],
        compiler_params=_params("arbitrary"),
        name="moe_combine",
    )(_slot_major(pos, rows), h, wsel, final_w.astype(F32).reshape(1, -1), y_sorted)


def _moe(h, norm_w, w_router, w1, w3, w2, final_w, final_norm):
    t = h.shape[0]
    n_tiles = _sorted_tiles(t)
    xn, member, wsel, esel = _router(h, norm_w, w_router)
    pos, meta = _positions(member, esel)
    first_tile = meta[0, :N_EXPERTS]
    last_tile = first_tile + meta[1, :N_EXPERTS]
    tile = jnp.arange(n_tiles, dtype=jnp.int32)
    tile_expert = jnp.minimum(jnp.sum(tile[:, None] >= last_tile[None, :], axis=1), N_EXPERTS - 1)
    pad_lo = first_tile * MOE_TILE + meta[2, :N_EXPERTS]
    xs = _dispatch(xn, pos, pad_lo, last_tile * MOE_TILE, n_tiles * MOE_TILE)
    ys = _experts(xs, tile_expert.astype(jnp.int32), last_tile[N_EXPERTS - 1:], w1, w3, w2)
    return _combine(h, wsel, pos, ys, final_w, final_norm)


def _final_norm_kernel(h_ref, w_ref, o_ref):
    o_ref[...] = _rms(h_ref[...], w_ref[...])


def _final_norm(h, w):
    t = h.shape[0]
    tm = min(ROW_TILE, t)
    row = pl.BlockSpec((tm, D_MODEL), lambda i: (i, 0))
    return pl.pallas_call(
        _final_norm_kernel, grid=(t // tm,), in_specs=[row, _const_spec((1, D_MODEL))],
        out_specs=row, out_shape=jax.ShapeDtypeStruct((t, D_MODEL), F32),
        compiler_params=_params("parallel"), name="final_norm",
    )(h, w.astype(F32).reshape(1, -1))


def _mixing_block(h, bsz, seq, cos2, sin2, norm_w, w_in, b_gate,
                  s5_lam_re, s5_lam_im, s5_b_re, s5_b_im, s5_c_re, s5_c_im, s5_d, s5_log_dt, s5_w_glu,
                  ssd_conv_w, ssd_conv_b, ssd_dt_bias, ssd_a_log, ssd_d, ssd_norm,
                  ret_norm,
                  lru_conv_w, lru_conv_b, lru_wa, lru_ba, lru_wx, lru_bx, lru_lam,
                  w_branch, w_out):
    w_cat, w_gate_half = _pack_w_in(w_in)
    (u_s5, z_ssd, xbc_ssd, dt_ssd, q_ret, k_ret, v_ret, g_ret, x_lru,
     gate_lru) = _inproj(h, norm_w, w_cat)
    seq3 = lambda a: a.reshape(bsz, seq, a.shape[-1])
    tables = _s5_tables(s5_lam_re, s5_lam_im, s5_b_re, s5_b_im, s5_c_re, s5_c_im, s5_log_dt,
                        seq // S5_SUB)
    y_s5 = _s5_scan(u_s5, tables, bsz, seq)
    y_ssd = _ssd(seq3(z_ssd), seq3(xbc_ssd), seq3(dt_ssd), ssd_conv_w, ssd_conv_b, ssd_dt_bias,
                 ssd_a_log, ssd_d, ssd_norm)
    y_ret = _retention(seq3(q_ret), seq3(k_ret), seq3(v_ret), seq3(g_ret), cos2, sin2, ret_norm)
    y_lru = _lru(seq3(x_lru), seq3(gate_lru), lru_conv_w, lru_conv_b, lru_wa, lru_ba, lru_wx,
                 lru_bx, lru_lam)
    flat = lambda a: a.reshape(bsz * seq, a.shape[-1])
    return _merge(h, norm_w, u_s5, y_s5, flat(y_ssd), flat(y_ret), flat(y_lru), w_gate_half,
                  b_gate, s5_d, s5_w_glu, w_branch, w_out)


def kernel(x, mem, positions, norm_mix, w_in, b_gate, s5_lam_re, s5_lam_im, s5_b_re, s5_b_im, s5_c_re, s5_c_im, s5_d, s5_log_dt, s5_w_glu, ssd_conv_w, ssd_conv_b, ssd_dt_bias, ssd_a_log, ssd_d, ssd_norm, ret_norm, lru_conv_w, lru_conv_b, lru_wa, lru_ba, lru_wx, lru_bx, lru_lam, w_branch, w_out, norm_xa, norm_mem, xa_wq, xa_wk, xa_wv, xa_wo, norm_ffn, ffn_w1, ffn_w3, ffn_w2, moe_router, moe_w1, moe_w3, moe_w2, norm_final):
    bsz, seq, _ = x.shape
    depth = norm_mix.shape[0]
    cos2, sin2 = _rope_tables(positions)
    h = x.reshape(bsz * seq, D_MODEL)
    for i in range(depth):
        h = _mixing_block(h, bsz, seq, cos2, sin2, norm_mix[i], w_in[i], b_gate[i],
                          s5_lam_re[i], s5_lam_im[i], s5_b_re[i], s5_b_im[i], s5_c_re[i], s5_c_im[i],
                          s5_d[i], s5_log_dt[i], s5_w_glu[i],
                          ssd_conv_w[i], ssd_conv_b[i], ssd_dt_bias[i], ssd_a_log[i], ssd_d[i],
                          ssd_norm[i], ret_norm[i],
                          lru_conv_w[i], lru_conv_b[i], lru_wa[i], lru_ba[i], lru_wx[i], lru_bx[i],
                          lru_lam[i], w_branch[i], w_out[i])
        k, v = _kv(mem, norm_mem[i], xa_wk[i], xa_wv[i])
        h = _xattn(h.reshape(bsz, seq, D_MODEL), norm_xa[i], xa_wq[i], k, v, xa_wo[i])
        h = h.reshape(bsz * seq, D_MODEL)
        last = i == depth - 1
        if i % 2 == 0:
            h = _ffn(h, norm_ffn[i], ffn_w1[i // 2], ffn_w3[i // 2], ffn_w2[i // 2])
            if last:
                h = _final_norm(h, norm_final)
        else:
            h = _moe(h, norm_ffn[i], moe_router[i // 2], moe_w1[i // 2], moe_w3[i // 2],
                     moe_w2[i // 2], norm_final, last)
    return h.reshape(bsz, seq, D_MODEL)
```

```python
import functools
import math

import numpy as np
import jax
import jax.numpy as jnp
from jax import lax
from jax.experimental import pallas as pl
from jax.experimental.pallas import tpu as pltpu

F32 = jnp.float32
BF16 = jnp.bfloat16

D_MODEL = 1024
N_MEM = 256
EPS = 1e-6
CHUNK = 128
N_BRANCH = 4
S5_WIDTH = 512
S5_GROUP = 16
S5_GROUPS = 32
S5_STATE = 64
S5_SUB = 16
SSD_HEADS = 8
SSD_HEAD_DIM = 64
SSD_WIDTH = 512
SSD_GROUPS = 2
SSD_STATE = 64
SSD_CONV = 4
SSD_CONV_DIM = SSD_WIDTH + 2 * SSD_GROUPS * SSD_STATE
RET_HEADS = 8
RET_HEAD_DIM = 64
RET_WIDTH = 512
ROPE_BASE = 10000.0
LRU_WIDTH = 512
LRU_BLOCKS = 8
LRU_BLOCK = 64
LRU_CONV = 4
LRU_C = 8.0
XA_HEADS = 4
XA_HEAD_DIM = 256
D_FF = 2816
N_EXPERTS = 8
D_FF_EXPERT = 3584
SECTION_WIDTHS = (S5_WIDTH, SSD_WIDTH, SSD_CONV_DIM, SSD_HEADS,
                  RET_WIDTH, RET_WIDTH, RET_WIDTH, RET_WIDTH,
                  LRU_WIDTH, LRU_WIDTH, N_BRANCH * D_MODEL)

LANES = 128
SUBLANES = 8
VMEM_LIMIT = 56 * 1024 * 1024
ROW_TILE = 512
LRU_ROWS = 256
LRU_SCAN_ROWS = 8


def _params(*sem):
    return pltpu.CompilerParams(dimension_semantics=sem, vmem_limit_bytes=VMEM_LIMIT)


def _const_spec(shape):
    nd = len(shape)
    return pl.BlockSpec(shape, lambda *_: (0,) * nd)


def _rms(x, w):
    return x * lax.rsqrt(jnp.mean(x * x, axis=-1, keepdims=True) + EPS) * w


def _sigmoid(x):
    return 0.5 + 0.5 * jnp.tanh(0.5 * x)


def _silu(x):
    return x * _sigmoid(x)


def _gelu(x):
    return jax.nn.gelu(x)


def _softplus(x):
    return jnp.maximum(x, 0.0) + jnp.log(1.0 + jnp.exp(-jnp.abs(x)))


def _dot(a, b):
    return jnp.dot(a, b, preferred_element_type=F32)


def _dot_nt(a, b):
    return lax.dot_general(a, b, (((1,), (1,)), ((), ())), preferred_element_type=F32)


def _split3(x):
    hi = x.astype(BF16)
    r1 = x - hi.astype(F32)
    mid = r1.astype(BF16)
    lo = (r1 - mid.astype(F32)).astype(BF16)
    return hi, mid, lo


def _split_dot(x, m_bf16):
    hi, mid, lo = _split3(x)
    return _dot(hi, m_bf16) + _dot(mid, m_bf16) + _dot(lo, m_bf16)


def _split_dot_left(m_bf16, x):
    hi, mid, lo = _split3(x)
    return _dot(m_bf16, hi) + _dot(m_bf16, mid) + _dot(m_bf16, lo)


def _shift_rows(x, k, fill=0.0):
    rows = x.shape[0]
    if k % SUBLANES == 0:
        return jnp.concatenate([jnp.full((k, x.shape[1]), fill, x.dtype), x[:rows - k]], axis=0)
    rolled = pltpu.roll(x, k, 0)
    row = lax.broadcasted_iota(jnp.int32, x.shape, 0)
    return jnp.where(row >= k, rolled, fill)


IN_WIDTHS = (S5_WIDTH, SSD_WIDTH, SSD_CONV_DIM, LANES,
             RET_WIDTH, RET_WIDTH, RET_WIDTH, RET_WIDTH,
             LRU_WIDTH, LRU_WIDTH)
IN_TOTAL_PADDED = sum(IN_WIDTHS)
DOT_COLS = 512


def _inproj_kernel(h_ref, nw_ref, w_ref, u_ref, *refs):
    out_refs, uscr_ref = refs[:-1], refs[-1]
    xb = _rms(h_ref[...], nw_ref[...]).astype(BF16)
    u = _dot(xb, w_ref[:, :S5_WIDTH])
    folded = u.shape[0] // S5_SUB
    for q in range(S5_Q):
        uscr_ref[q] = u[:, q * LANES:(q + 1) * LANES]
        for t in range(S5_SUB):
            u_ref[q, :, t * LANES:(t + 1) * LANES] = uscr_ref[
                q, pl.ds(t, folded, stride=S5_SUB), :].astype(u_ref.dtype)
    off = S5_WIDTH
    for o_ref, width in zip(out_refs, IN_WIDTHS[1:]):
        for c in range(0, width, DOT_COLS):
            n = min(DOT_COLS, width - c)
            o_ref[:, c:c + n] = _dot(xb, w_ref[:, off + c:off + c + n]).astype(o_ref.dtype)
        off += width


def _inproj(h, norm_w, w_cat):
    t = h.shape[0]
    tm = min(ROW_TILE, t)
    out_shape = ([jax.ShapeDtypeStruct((S5_Q, t // S5_SUB, S5_ROW), BF16)]
                 + [jax.ShapeDtypeStruct((t, w), BF16) for w in IN_WIDTHS[1:]])
    out_specs = ([pl.BlockSpec((S5_Q, tm // S5_SUB, S5_ROW), lambda i: (0, i, 0))]
                 + [pl.BlockSpec((tm, w), lambda i: (i, 0)) for w in IN_WIDTHS[1:]])
    return pl.pallas_call(
        _inproj_kernel,
        grid=(t // tm,),
        in_specs=[pl.BlockSpec((tm, D_MODEL), lambda i: (i, 0)),
                  _const_spec((1, D_MODEL)),
                  _const_spec((D_MODEL, IN_TOTAL_PADDED))],
        out_specs=out_specs,
        out_shape=out_shape,
        scratch_shapes=[pltpu.VMEM((S5_Q, tm, LANES), F32)],
        compiler_params=_params("parallel"),
        name="inproj",
    )(h, norm_w.reshape(1, D_MODEL), w_cat)


def _pack_w_in(w_in):
    pieces = []
    off = 0
    for width in SECTION_WIDTHS[:-1]:
        sec = w_in[:, off:off + width]
        if width == SSD_HEADS:
            sec = jnp.pad(sec, ((0, 0), (0, LANES - width)))
        pieces.append(sec)
        off += width
    w_gate = (0.5 * w_in[:, off:]).astype(BF16)
    return jnp.concatenate(pieces, axis=1).astype(BF16), w_gate


S5_Q = S5_WIDTH // LANES
S5_QG = S5_GROUPS // S5_Q
S5_QS = S5_QG * S5_STATE
S5_ROW = S5_SUB * LANES
S5_TILE = 256
S5_SEQS = 8


def _s5_expand_tables(wc_ref, xc_ref, rin_ref, rout_ref, win_ref, wx_ref):
    group_shift = int(math.log2(S5_GROUP))
    state_shift = int(math.log2(S5_STATE))
    for c in range(0, 2 * S5_QS, DOT_COLS):
        r = lax.broadcasted_iota(jnp.int32, (S5_ROW, DOT_COLS), 0)
        col = lax.broadcasted_iota(jnp.int32, (S5_ROW, DOT_COLS), 1) + c
        own = ((r >> group_shift) & (S5_QG - 1)) == ((col & (S5_QS - 1)) >> state_shift)
        rep = _dot(wc_ref[0], rin_ref[:, c:c + DOT_COLS])
        win_ref[:, c:c + DOT_COLS] = jnp.where(own, rep, 0.0).astype(BF16)
    for c in range(0, S5_ROW, DOT_COLS):
        r = lax.broadcasted_iota(jnp.int32, (2 * S5_QS, DOT_COLS), 0)
        col = lax.broadcasted_iota(jnp.int32, (2 * S5_QS, DOT_COLS), 1) + c
        own = ((r & (S5_QS - 1)) >> state_shift) == ((col >> group_shift) & (S5_QG - 1))
        rep = _dot(xc_ref[0], rout_ref[:, c:c + DOT_COLS])
        wx_ref[:, c:c + DOT_COLS] = jnp.where(own, rep, 0.0).astype(BF16)


def _s5_kernel(u_ref, toe_ref, wc_ref, xc_ref, rin_ref, rout_ref, pw_ref, y_ref,
               win_ref, wx_ref, x_ref, sp_ref, *, rows):
    @pl.when(pl.program_id(1) == 0)
    def _():
        _s5_expand_tables(wc_ref, xc_ref, rin_ref, rout_ref, win_ref, wx_ref)

    n_steps = int(math.log2(rows))
    for c in range(0, 2 * S5_QS, DOT_COLS):
        x_ref[:, c:c + DOT_COLS] = _dot(u_ref[0], win_ref[:, c:c + DOT_COLS])

    def seq(b, carry):
        r0 = pl.multiple_of(b * rows, rows)
        for lc in range(S5_QS // LANES):
            re = slice(lc * LANES, (lc + 1) * LANES)
            im = slice(S5_QS + lc * LANES, S5_QS + (lc + 1) * LANES)
            sr = x_ref[pl.ds(r0, rows), re]
            si = x_ref[pl.ds(r0, rows), im]
            for j in range(n_steps):
                k = 1 << j
                pr = pw_ref[0, 2 * j:2 * j + 1, re]
                pi = pw_ref[0, 2 * j + 1:2 * j + 2, re]
                shr = _shift_rows(sr, k)
                shi = _shift_rows(si, k)
                sr, si = sr + pr * shr - pi * shi, si + pr * shi + pi * shr
            sp_ref[pl.ds(r0, rows), re] = _shift_rows(sr, 1).astype(BF16)
            sp_ref[pl.ds(r0, rows), im] = _shift_rows(si, 1).astype(BF16)
        return carry

    lax.fori_loop(0, u_ref.shape[1] // rows, seq, 0)
    for nn in range(S5_ROW // S5_TILE):
        cols = slice(nn * S5_TILE, (nn + 1) * S5_TILE)
        acc = _dot(sp_ref[...], wx_ref[:, cols])
        for kk in range(nn + 1):
            acc = acc + _dot(u_ref[0, :, kk * S5_TILE:(kk + 1) * S5_TILE], toe_ref[0, nn - kk])
        y_ref[0, :, cols] = acc.astype(y_ref.dtype)


def _s5_tables(lam_re, lam_im, b_re, b_im, c_re, c_im, log_dt, rows):
    lr = lam_re.astype(F32)
    li = lam_im.astype(F32)
    step = jnp.exp(log_dt.astype(F32))[:, None]
    mag = jnp.exp(lr * step)
    ar = mag * jnp.cos(li * step)
    ai = mag * jnp.sin(li * step)
    inv = 1.0 / (lr * lr + li * li)
    cr = ((ar - 1.0) * lr + ai * li) * inv
    ci = (ai * lr - (ar - 1.0) * li) * inv
    bbr = cr[..., None] * b_re - ci[..., None] * b_im
    bbi = cr[..., None] * b_im + ci[..., None] * b_re

    def apow(e):
        e = jnp.asarray(e, F32)[:, None, None]
        m = jnp.exp(lr * step * e)
        return m * jnp.cos(li * step * e), m * jnp.sin(li * step * e)

    sub = S5_SUB
    pr, pi = apow(np.arange(sub + 1))
    m_r = pr[:sub, :, :, None] * bbr - pi[:sub, :, :, None] * bbi
    m_i = pr[:sub, :, :, None] * bbi + pi[:sub, :, :, None] * bbr
    kern = (jnp.einsum('gon,tgni->tgoi', c_re, m_r)
            - jnp.einsum('gon,tgni->tgoi', c_im, m_i))
    eye = jnp.eye(S5_QG, dtype=F32)
    blk = lambda a: a.reshape(a.shape[0], S5_Q, S5_QG, *a.shape[2:])
    kern_bd = jnp.einsum('tqgoi,gh->tqgiho', blk(kern), eye).reshape(sub, S5_Q, LANES, LANES)
    kern_bd = jnp.concatenate([jnp.zeros_like(kern_bd[:1]), kern_bd], axis=0)
    steps_per_tile = S5_TILE // LANES
    d = np.arange(sub // steps_per_tile)[:, None, None]
    tau = steps_per_tile * d + np.arange(steps_per_tile)[None, None, :] - np.arange(steps_per_tile)[None, :, None]
    toe = kern_bd[tau + 1]
    toe = jnp.transpose(toe, (3, 0, 1, 4, 2, 5)).reshape(S5_Q, sub // steps_per_tile, S5_TILE, S5_TILE)
    er, ei = pr[sub - 1 - np.arange(sub)], pi[sub - 1 - np.arange(sub)]
    w_r = er[..., None] * bbr - ei[..., None] * bbi
    w_i = er[..., None] * bbi + ei[..., None] * bbr
    fold_in = lambda w: jnp.transpose(blk(w), (1, 0, 2, 4, 3)).reshape(S5_Q, S5_ROW, S5_STATE)
    wc = jnp.concatenate([fold_in(w_r), fold_in(w_i)], axis=2)
    qr, qi = pr[1:], pi[1:]
    x_r = c_re[None] * qr[:, :, None, :] - c_im[None] * qi[:, :, None, :]
    x_i = -(c_re[None] * qi[:, :, None, :] + c_im[None] * qr[:, :, None, :])
    fold_out = lambda w: jnp.transpose(blk(w), (1, 2, 4, 0, 3)).reshape(S5_Q, S5_QS, sub * S5_GROUP)
    xc = jnp.concatenate([fold_out(x_r), fold_out(x_i)], axis=1)
    n_steps = int(math.log2(rows))
    sr_, si_ = apow(sub * (2.0 ** np.arange(n_steps)))
    pw = jnp.stack([sr_, si_], axis=1).reshape(2 * n_steps, S5_Q, S5_QS)
    pw = jnp.transpose(pw, (1, 0, 2))
    pw = jnp.pad(pw, ((0, 0), (0, 2 * SUBLANES - 2 * n_steps), (0, 0)))
    return toe.astype(BF16), wc.astype(BF16), xc.astype(BF16), pw


def _s5_replicators():
    k = np.arange(2 * S5_STATE)[:, None]
    c = np.arange(2 * S5_QS)[None, :]
    rin = ((k // S5_STATE) == (c // S5_QS)) & ((k % S5_STATE) == (c % S5_STATE))
    r = np.arange(S5_SUB * S5_GROUP)[:, None]
    c = np.arange(S5_ROW)[None, :]
    rout = ((r // S5_GROUP) == (c // LANES)) & ((r % S5_GROUP) == (c % S5_GROUP))
    return jnp.asarray(rin, BF16), jnp.asarray(rout, BF16)


def _s5_scan(u5, tables, bsz, seq):
    toe, wc, xc, pw = tables
    rin, rout = _s5_replicators()
    rows = seq // S5_SUB
    nb = math.gcd(S5_SEQS, bsz)
    row_blk = pl.BlockSpec((1, nb * rows, S5_ROW), lambda q, j: (q, j, 0))
    per_q = lambda a: pl.BlockSpec((1,) + a.shape[1:], lambda q, j: (q,) + (0,) * (a.ndim - 1))
    const = lambda a: pl.BlockSpec(a.shape, lambda q, j: (0,) * a.ndim)
    return pl.pallas_call(
        functools.partial(_s5_kernel, rows=rows),
        grid=(S5_Q, bsz // nb),
        in_specs=[row_blk, per_q(toe), per_q(wc), per_q(xc), const(rin), const(rout), per_q(pw)],
        out_specs=row_blk,
        out_shape=jax.ShapeDtypeStruct(u5.shape, BF16),
        scratch_shapes=[pltpu.VMEM((S5_ROW, 2 * S5_QS), BF16),
                        pltpu.VMEM((2 * S5_QS, S5_ROW), BF16),
                        pltpu.VMEM((nb * rows, 2 * S5_QS), F32),
                        pltpu.VMEM((nb * rows, 2 * S5_QS), BF16)],
        compiler_params=_params("parallel", "arbitrary"),
        name="s5_scan",
    )(u5, toe, wc, xc, rin, rout, pw)


CONV_PAD = SUBLANES
SEQS_PER_STEP = 2


def _causal_conv(xpad_ref, x, w, b):
    rows = x.shape[0]
    xpad_ref[CONV_PAD:CONV_PAD + rows, :] = x
    k = w.shape[0]
    acc = b
    for j in range(k):
        s = CONV_PAD - (k - 1) + j
        acc = acc + w[j:j + 1, :] * xpad_ref[s:s + rows, :]
    xpad_ref[0:CONV_PAD, :] = xpad_ref[rows:rows + CONV_PAD, :]
    return acc


def _ssd_kernel(z_ref, xbc_ref, dt_ref, cw_ref, cb_ref, dtb_ref, a_ref, d_ref, nw_ref, tri_ref,
                o_ref, xpad_ref, state_ref, y_ref):
    @pl.when(pl.program_id(1) == 0)
    def _():
        state_ref[...] = jnp.zeros(state_ref.shape, F32)
        xpad_ref[:, 0:CONV_PAD, :] = jnp.zeros((xpad_ref.shape[0], CONV_PAD, xpad_ref.shape[2]), F32)

    for s in range(z_ref.shape[0]):
        _ssd_chunk(z_ref.at[s], xbc_ref.at[s], dt_ref.at[s], cw_ref, cb_ref, dtb_ref, a_ref, d_ref,
                   nw_ref, tri_ref, o_ref.at[s], xpad_ref.at[s], state_ref.at[s], y_ref.at[s])


def _ssd_chunk(z_ref, xbc_ref, dt_ref, cw_ref, cb_ref, dtb_ref, a_ref, d_ref, nw_ref, tri_ref,
               o_ref, xpad_ref, state_ref, y_ref):
    xbc = _silu(_causal_conv(xpad_ref, xbc_ref[...].astype(F32), cw_ref[...], cb_ref[...]))
    xs_t = xbc[:, :SSD_WIDTH].T
    bs = xbc[:, SSD_WIDTH:SSD_WIDTH + LANES]
    cs_t = xbc[:, SSD_WIDTH + LANES:].T.astype(BF16)
    lane = lax.broadcasted_iota(jnp.int32, bs.shape, 1)
    dt = _softplus(dt_ref[...].astype(F32) + dtb_ref[...])
    acum = _split_dot_left(tri_ref[...], dt * a_ref[...])
    dt_t = dt.T
    acum_t = acum.T
    later = (lax.broadcasted_iota(jnp.int32, (CHUNK, CHUNK), 1)
             >= lax.broadcasted_iota(jnp.int32, (CHUNK, CHUNK), 0))
    heads_per_group = SSD_HEADS // SSD_GROUPS
    b_g, cb_t = [], []
    for g in range(SSD_GROUPS):
        in_group = (lane >= g * SSD_STATE) & (lane < (g + 1) * SSD_STATE)
        b_g.append(jnp.where(in_group, bs, 0.0).astype(BF16))
        cb_t.append(_dot(b_g[g], cs_t))
    x_t, a_row, last, m_t, xdt_t, xw_t = [], [], [], [], [], []
    for hd in range(SSD_HEADS):
        x_t.append(xs_t[hd * SSD_HEAD_DIM:(hd + 1) * SSD_HEAD_DIM, :])
        a_row.append(acum_t[hd:hd + 1, :])
        last.append(acum_t[hd:hd + 1, CHUNK - 1:CHUNK])
        l_t = jnp.exp(jnp.where(later, a_row[hd] - acum[:, hd:hd + 1], -jnp.inf))
        m_t.append((cb_t[hd // heads_per_group] * l_t).astype(BF16))
        xdt = x_t[hd] * dt_t[hd:hd + 1, :]
        xdt_t.append(xdt.astype(BF16))
        xw_t.append((xdt * jnp.exp(last[hd] - a_row[hd])).astype(BF16))
    prev = [state_ref[hd] for hd in range(SSD_HEADS)]
    y_in = [_dot(xdt_t[hd], m_t[hd]) for hd in range(SSD_HEADS)]
    y_off = [_dot(prev[hd].astype(BF16), cs_t) for hd in range(SSD_HEADS)]
    new = [_dot(xw_t[hd], b_g[hd // heads_per_group]) for hd in range(SSD_HEADS)]
    for hd in range(SSD_HEADS):
        state_ref[hd] = prev[hd] * jnp.exp(last[hd]) + new[hd]
        y_ref[hd * SSD_HEAD_DIM:(hd + 1) * SSD_HEAD_DIM, :] = (
            y_in[hd] + y_off[hd] * jnp.exp(a_row[hd]) + d_ref[:, hd:hd + 1] * x_t[hd])
    y = y_ref[...].T * _silu(z_ref[...].astype(F32))
    o_ref[...] = _rms(y, nw_ref[...]).astype(o_ref.dtype)


def _ssd(z, xbc, dt, conv_w, conv_b, dt_bias, a_log, d_skip, norm_w):
    bsz, seq, _ = z.shape
    nc = seq // CHUNK
    ns = math.gcd(SEQS_PER_STEP, bsz)

    def lane_pad(v):
        return jnp.pad(v.astype(F32).reshape(1, -1), ((0, 0), (0, LANES - v.shape[-1])))

    tri = jnp.asarray(np.tril(np.ones((CHUNK, CHUNK), np.float32)), BF16)
    blk = lambda w: pl.BlockSpec((ns, CHUNK, w), lambda b, c: (b, c, 0))
    return pl.pallas_call(
        _ssd_kernel,
        grid=(bsz // ns, nc),
        in_specs=[blk(SSD_WIDTH), blk(SSD_CONV_DIM), blk(LANES),
                  _const_spec((SSD_CONV, SSD_CONV_DIM)), _const_spec((1, SSD_CONV_DIM)),
                  _const_spec((1, LANES)), _const_spec((1, LANES)), _const_spec((1, LANES)),
                  _const_spec((1, SSD_WIDTH)), _const_spec((CHUNK, CHUNK))],
        out_specs=blk(SSD_WIDTH),
        out_shape=jax.ShapeDtypeStruct((bsz, seq, SSD_WIDTH), BF16),
        scratch_shapes=[pltpu.VMEM((ns, CHUNK + CONV_PAD, SSD_CONV_DIM), F32),
                        pltpu.VMEM((ns, SSD_HEADS, SSD_HEAD_DIM, SSD_GROUPS * SSD_STATE), F32),
                        pltpu.VMEM((ns, SSD_WIDTH, CHUNK), F32)],
        compiler_params=_params("parallel", "arbitrary"),
        name="ssd",
    )(z, xbc, dt, conv_w.astype(F32), conv_b.astype(F32).reshape(1, -1),
      lane_pad(dt_bias), lane_pad(-jnp.exp(a_log.astype(F32))), lane_pad(d_skip),
      norm_w.astype(F32).reshape(1, -1), tri)


GN_TILE = 256


def _head_means(x, avg_bf16):
    hi = x.astype(BF16)
    lo = (x - hi.astype(F32)).astype(BF16)
    out = []
    for c in range(0, RET_WIDTH, GN_TILE):
        out.append(_dot(hi[:, c:c + GN_TILE], avg_bf16) + _dot(lo[:, c:c + GN_TILE], avg_bf16))
    return jnp.concatenate(out, axis=1)


def _retention_tables():
    h = np.arange(RET_HEADS, dtype=np.float64)
    log_gamma = np.log1p(-np.exp2(-5.0 - h))
    idx = np.arange(CHUNK, dtype=np.float64)
    diff = idx[:, None] - idx[None, :]
    dmat = np.where(diff >= 0, np.exp(np.maximum(diff, 0.0)[None] * log_gamma[:, None, None]), 0.0)
    k_decay = np.exp((CHUNK - 1.0 - idx)[:, None] * log_gamma)
    q_decay = np.exp((idx + 1.0)[:, None] * log_gamma)
    c_decay = np.exp(CHUNK * log_gamma)[None, :]
    rep = lambda a: np.repeat(a, RET_HEAD_DIM, axis=1).astype(np.float32)
    avg = np.kron(np.eye(GN_TILE // RET_HEAD_DIM),
                  np.full((RET_HEAD_DIM, RET_HEAD_DIM), 1.0 / RET_HEAD_DIM))
    return (dmat.astype(np.float32), rep(k_decay * RET_HEAD_DIM ** -0.5), rep(q_decay),
            rep(c_decay), avg.astype(np.float32))


def _rope(x, cos, sin_signed):
    half = RET_HEAD_DIM // 2
    width = x.shape[1]
    fwd = pltpu.roll(x, half, 1)
    bwd = pltpu.roll(x, width - half, 1)
    lane = lax.broadcasted_iota(jnp.int32, x.shape, 1)
    swapped = jnp.where((lane % RET_HEAD_DIM) < half, bwd, fwd)
    return x * cos + swapped * sin_signed


def _retention_kernel(q_ref, k_ref, v_ref, g_ref, cos_ref, sin_ref, dmat_ref, kdec_ref, qdec_ref,
                      cdec_ref, avg_ref, gn_ref, o_ref, state_ref):
    @pl.when(pl.program_id(1) == 0)
    def _():
        state_ref[...] = jnp.zeros(state_ref.shape, F32)

    for s in range(q_ref.shape[0]):
        _retention_chunk(q_ref.at[s], k_ref.at[s], v_ref.at[s], g_ref.at[s], cos_ref.at[s],
                         sin_ref.at[s], dmat_ref, kdec_ref, qdec_ref, cdec_ref, avg_ref, gn_ref,
                         o_ref.at[s], state_ref.at[s])


def _retention_chunk(q_ref, k_ref, v_ref, g_ref, cos_ref, sin_ref, dmat_ref, kdec_ref, qdec_ref,
                     cdec_ref, avg_ref, gn_ref, o_ref, state_ref):
    reps = RET_WIDTH // LANES
    cos = jnp.tile(cos_ref[...], (1, reps))
    sin = jnp.tile(sin_ref[...], (1, reps))
    q = _rope(q_ref[...].astype(F32), cos, sin)
    k = _rope(k_ref[...].astype(F32), cos, sin)
    v = v_ref[...].astype(F32)
    kd_t = (k * kdec_ref[...]).T.astype(BF16)
    k_tb = (k.T * (RET_HEAD_DIM ** -0.5)).astype(BF16)
    qd = qdec_ref[...]
    cd = cdec_ref[...]
    first = lax.broadcasted_iota(jnp.int32, (CHUNK, LANES), 1) < RET_HEAD_DIM
    r = lax.broadcasted_iota(jnp.int32, (LANES, LANES), 0) < RET_HEAD_DIM
    c = lax.broadcasted_iota(jnp.int32, (LANES, LANES), 1) < RET_HEAD_DIM
    same_head = r == c
    n_pairs = RET_HEADS // 2
    lanes = [slice(p * LANES, (p + 1) * LANES) for p in range(n_pairs)]
    keeps = (first, jnp.logical_not(first))
    prev = [state_ref[p] for p in range(n_pairs)]
    q_h = [jnp.where(keeps[hd % 2], q[:, lanes[hd // 2]], 0.0).astype(BF16) for hd in range(RET_HEADS)]
    v_h = [jnp.where(keeps[hd % 2], v[:, lanes[hd // 2]], 0.0).astype(BF16) for hd in range(RET_HEADS)]
    scores = [(_dot(q_h[hd], k_tb[lanes[hd // 2], :]) * dmat_ref[hd]).astype(BF16)
              for hd in range(RET_HEADS)]
    inner = [_dot(scores[hd], v_h[hd]) for hd in range(RET_HEADS)]
    cross = [_dot(q[:, sl].astype(BF16), prev[p].astype(BF16)) for p, sl in enumerate(lanes)]
    kv = [_dot(kd_t[sl, :], v[:, sl].astype(BF16)) for sl in lanes]
    for p, sl in enumerate(lanes):
        state_ref[p] = prev[p] * cd[:, sl] + jnp.where(same_head, kv[p], 0.0)
    y = jnp.concatenate([cross[p] * qd[:, sl] + inner[2 * p] + inner[2 * p + 1]
                         for p, sl in enumerate(lanes)], axis=1)
    mu = _head_means(y, avg_ref[...])
    yc = y - mu
    var = _head_means(yc * yc, avg_ref[...])
    yn = yc * lax.rsqrt(var + EPS) * gn_ref[...]
    o_ref[...] = (_silu(g_ref[...].astype(F32)) * yn).astype(o_ref.dtype)


def _retention(q, k, v, g, cos2, sin2, gn_w):
    bsz, seq, _ = q.shape
    nc = seq // CHUNK
    ns = math.gcd(SEQS_PER_STEP, bsz)
    dmat, kdec, qdec, cdec, avg = (jnp.asarray(a) for a in _retention_tables())
    avg = avg.astype(BF16)
    blk = lambda w: pl.BlockSpec((ns, CHUNK, w), lambda b, c: (b, c, 0))
    return pl.pallas_call(
        _retention_kernel,
        grid=(bsz // ns, nc),
        in_specs=[blk(RET_WIDTH)] * 4 + [blk(LANES)] * 2 + [
            _const_spec(dmat.shape), _const_spec(kdec.shape), _const_spec(qdec.shape),
            _const_spec(cdec.shape), _const_spec(avg.shape), _const_spec((1, RET_WIDTH))],
        out_specs=blk(RET_WIDTH),
        out_shape=jax.ShapeDtypeStruct((bsz, seq, RET_WIDTH), BF16),
        scratch_shapes=[pltpu.VMEM((ns, RET_HEADS // 2, LANES, LANES), F32)],
        compiler_params=_params("parallel", "arbitrary"),
        name="retention",
    )(q, k, v, g, cos2, sin2, dmat, kdec, qdec, cdec, avg, gn_w.astype(F32).reshape(1, -1))


def _rope_tables(positions):
    half = RET_HEAD_DIM // 2
    inv_freq = ROPE_BASE ** (-jnp.arange(half, dtype=F32) / half)
    ang = positions.astype(F32)[..., None] * inv_freq
    cos = jnp.cos(ang)
    sin = jnp.sin(ang)
    cos2 = jnp.concatenate([cos, cos, cos, cos], axis=-1)
    sin2 = jnp.concatenate([-sin, sin, -sin, sin], axis=-1)
    return cos2, sin2


def _lru_kernel(x_ref, gate_ref, cw_ref, cb_ref, w_ref, bias_ref, lamc_ref, o_ref, xpad_ref, h_ref):
    c = pl.program_id(1)

    @pl.when(c == 0)
    def _():
        h_ref[...] = jnp.zeros(h_ref.shape, F32)
        xpad_ref[0:CONV_PAD, :] = jnp.zeros((CONV_PAD, xpad_ref.shape[1]), F32)

    rows = x_ref.shape[1]
    xc = _causal_conv(xpad_ref, x_ref[0].astype(F32), cw_ref[...], cb_ref[...])
    ri = _sigmoid(_dot(xc.astype(BF16), w_ref[...]) + bias_ref[...])
    log_a = lamc_ref[...] * ri[:, :LRU_WIDTH]
    a_all = jnp.exp(log_a)
    mult = jnp.sqrt(jnp.maximum(1.0 - jnp.exp(2.0 * log_a), 0.0))
    b_all = mult * ri[:, LRU_WIDTH:] * xc
    gate = _gelu(gate_ref[0].astype(F32))
    n_steps = int(math.log2(LRU_SCAN_ROWS))
    row0 = lax.broadcasted_iota(jnp.int32, (LRU_SCAN_ROWS, LANES), 0) == 0
    for j in range(LRU_WIDTH // LANES):
        sl = slice(j * LANES, (j + 1) * LANES)
        carry = h_ref[0:1, sl]
        for r0 in range(0, rows, LRU_SCAN_ROWS):
            a = a_all[r0:r0 + LRU_SCAN_ROWS, sl]
            b = b_all[r0:r0 + LRU_SCAN_ROWS, sl]
            b = b + jnp.where(row0, a * carry, 0.0)
            for s in range(n_steps):
                k = 1 << s
                b = b + a * _shift_rows(b, k)
                if s + 1 < n_steps:
                    a = a * _shift_rows(a, k, 1.0)
            carry = b[LRU_SCAN_ROWS - 1:LRU_SCAN_ROWS, :]
            o_ref[0, r0:r0 + LRU_SCAN_ROWS, sl] = (
                b * gate[r0:r0 + LRU_SCAN_ROWS, sl]).astype(o_ref.dtype)
        h_ref[0:1, sl] = carry


def _block_diag(w):
    nb, n, _ = w.shape
    eye = jnp.eye(nb, dtype=w.dtype)
    return jnp.einsum('bij,bc->bicj', w, eye).reshape(nb * n, nb * n)


def _lru(x, gate, conv_w, conv_b, wa, ba, wx, bx, lam):
    bsz, seq, _ = x.shape
    rows = min(LRU_ROWS, seq)
    w = jnp.concatenate([_block_diag(wa), _block_diag(wx)], axis=1).astype(BF16)
    bias = jnp.concatenate([ba, bx]).astype(F32).reshape(1, -1)
    lamc = (-LRU_C * jax.nn.softplus(-lam.astype(F32))).reshape(1, -1)
    blk = lambda wd: pl.BlockSpec((1, rows, wd), lambda b, c: (b, c, 0))
    return pl.pallas_call(
        _lru_kernel,
        grid=(bsz, seq // rows),
        in_specs=[blk(LRU_WIDTH), blk(LRU_WIDTH),
                  _const_spec((LRU_CONV, LRU_WIDTH)), _const_spec((1, LRU_WIDTH)),
                  _const_spec((LRU_WIDTH, 2 * LRU_WIDTH)), _const_spec((1, 2 * LRU_WIDTH)),
                  _const_spec((1, LRU_WIDTH))],
        out_specs=blk(LRU_WIDTH),
        out_shape=jax.ShapeDtypeStruct((bsz, seq, LRU_WIDTH), BF16),
        scratch_shapes=[pltpu.VMEM((rows + CONV_PAD, LRU_WIDTH), F32),
                        pltpu.VMEM((SUBLANES, LRU_WIDTH), F32)],
        compiler_params=_params("parallel", "arbitrary"),
        name="rglru",
    )(x, gate, conv_w.astype(F32), conv_b.astype(F32).reshape(1, -1), w, bias, lamc)


def _merge_kernel(h_ref, nw_ref, u_ref, ys5_ref, b1_ref, b2_ref, b3_ref, wg_ref, bg_ref, d_ref,
                  wglu_ref, wb_ref, wo_ref, o_ref, yscr_ref):
    folded = yscr_ref.shape[1] // S5_SUB
    for q in range(S5_Q):
        z = ys5_ref[q].astype(F32) + d_ref[q] * u_ref[q].astype(F32)
        for t in range(S5_SUB):
            yscr_ref[q, pl.ds(t, folded, stride=S5_SUB), :] = z[:, t * LANES:(t + 1) * LANES]
    y = _gelu(jnp.concatenate([yscr_ref[q] for q in range(S5_Q)], axis=1))
    b0 = (y * _sigmoid(_dot(y.astype(BF16), wglu_ref[...]))).astype(BF16)
    branches = (b0, b1_ref[...], b2_ref[...], b3_ref[...])
    h = h_ref[...]
    xb = _rms(h, nw_ref[...]).astype(BF16)
    cols = [slice(i * D_MODEL, (i + 1) * D_MODEL) for i in range(N_BRANCH)]
    ths = [jnp.tanh(_dot(xb, wg_ref[:, sl]) + bg_ref[:, sl]) for sl in cols]
    ps = [_dot(br, wb_ref[i]) for i, br in enumerate(branches)]
    merged = None
    for th, p in zip(ths, ps):
        term = p + th * p
        merged = term if merged is None else merged + term
    o_ref[...] = h + _dot(merged.astype(BF16), wo_ref[...])


def _merge(h, norm_w, u, ys5, b1, b2, b3, w_gate_half, b_gate, s5_d, w_glu, w_branch, w_out):
    t = h.shape[0]
    tm = min(ROW_TILE, t)
    row = lambda w: pl.BlockSpec((tm, w), lambda i: (i, 0))
    s5_blk = pl.BlockSpec((S5_Q, tm // S5_SUB, S5_ROW), lambda i: (0, i, 0))
    d_fold = jnp.tile(s5_d.astype(F32).reshape(S5_Q, 1, LANES), (1, 1, S5_SUB))
    return pl.pallas_call(
        _merge_kernel,
        grid=(t // tm,),
        in_specs=[row(D_MODEL), _const_spec((1, D_MODEL)), s5_blk, s5_blk] + [row(512)] * 3 + [
                  _const_spec((D_MODEL, N_BRANCH * D_MODEL)),
                  _const_spec((1, N_BRANCH * D_MODEL)), _const_spec((S5_Q, 1, S5_ROW)),
                  _const_spec((S5_WIDTH, S5_WIDTH)),
                  _const_spec((N_BRANCH, 512, D_MODEL)), _const_spec((D_MODEL, D_MODEL))],
        out_specs=row(D_MODEL),
        out_shape=jax.ShapeDtypeStruct((t, D_MODEL), F32),
        scratch_shapes=[pltpu.VMEM((S5_Q, tm, LANES), F32)],
        compiler_params=_params("parallel"),
        name="merge",
    )(h, norm_w.astype(F32).reshape(1, -1), u, ys5, b1, b2, b3, w_gate_half,
      0.5 * b_gate.astype(F32).reshape(1, -1), d_fold, w_glu.astype(BF16), w_branch.astype(BF16),
      (0.5 * w_out).astype(BF16))


def _kv_kernel(mem_ref, nw_ref, w_ref, k_ref, v_ref):
    mb = _rms(mem_ref[0], nw_ref[...]).astype(BF16)
    for c in range(0, D_MODEL, DOT_COLS):
        k_ref[0, :, c:c + DOT_COLS] = _dot(mb, w_ref[:, c:c + DOT_COLS]).astype(BF16)
        v_ref[0, :, c:c + DOT_COLS] = _dot(
            mb, w_ref[:, D_MODEL + c:D_MODEL + c + DOT_COLS]).astype(BF16)


def _kv(mem, norm_w, wk, wv):
    bsz, n_mem, _ = mem.shape
    w = jnp.concatenate([wk, wv], axis=1).astype(BF16)
    blk = pl.BlockSpec((1, n_mem, D_MODEL), lambda b: (b, 0, 0))
    return pl.pallas_call(
        _kv_kernel,
        grid=(bsz,),
        in_specs=[blk, _const_spec((1, D_MODEL)), _const_spec((D_MODEL, 2 * D_MODEL))],
        out_specs=[blk, blk],
        out_shape=[jax.ShapeDtypeStruct((bsz, n_mem, D_MODEL), BF16)] * 2,
        compiler_params=_params("parallel"),
        name="xattn_kv",
    )(mem, norm_w.astype(F32).reshape(1, -1), w)


def _xattn_kernel(h_ref, nw_ref, wq_ref, k_ref, v_ref, wo_ref, o_ref, att_ref):
    h = h_ref[0]
    xb = _rms(h, nw_ref[...]).astype(BF16)
    heads = [slice(hd * XA_HEAD_DIM, (hd + 1) * XA_HEAD_DIM) for hd in range(XA_HEADS)]
    qs = [_dot(xb, wq_ref[:, sl]).astype(BF16) for sl in heads]
    ss = [_dot_nt(q, k_ref[0, :, sl]) for q, sl in zip(qs, heads)]
    ps = [jnp.exp(s - jnp.max(s, axis=-1, keepdims=True)) for s in ss]
    for p, sl in zip(ps, heads):
        inv = 1.0 / jnp.sum(p, axis=-1, keepdims=True)
        att_ref[:, sl] = (_dot(p.astype(BF16), v_ref[0, :, sl]) * inv).astype(BF16)
    o_ref[0] = h + _dot(att_ref[...], wo_ref[...])


def _xattn(h3, norm_w, wq, k, v, wo):
    bsz, seq, _ = h3.shape
    tq = min(ROW_TILE, seq)
    n_mem = k.shape[1]
    row = pl.BlockSpec((1, tq, D_MODEL), lambda b, i: (b, i, 0))
    kv = pl.BlockSpec((1, n_mem, D_MODEL), lambda b, i: (b, 0, 0))
    return pl.pallas_call(
        _xattn_kernel,
        grid=(bsz, seq // tq),
        in_specs=[row, _const_spec((1, D_MODEL)), _const_spec((D_MODEL, D_MODEL)), kv, kv,
                  _const_spec((D_MODEL, D_MODEL))],
        out_specs=row,
        out_shape=jax.ShapeDtypeStruct((bsz, seq, D_MODEL), F32),
        scratch_shapes=[pltpu.VMEM((tq, D_MODEL), BF16)],
        compiler_params=_params("parallel", "parallel"),
        name="xattn",
    )(h3, norm_w.astype(F32).reshape(1, -1), (wq * XA_HEAD_DIM ** -0.5).astype(BF16), k, v,
      wo.astype(BF16))


FF_COLS = 256


def _swiglu_acc(xb, w1_ref, w3_ref, w2_ref, width, lead=()):
    acc = None
    for c in range(0, width, FF_COLS):
        a = _dot(xb, w1_ref[lead + (slice(None), slice(c, c + FF_COLS))])
        b = _dot(xb, w3_ref[lead + (slice(None), slice(c, c + FF_COLS))])
        g = (_silu(a) * b).astype(BF16)
        term = _dot(g, w2_ref[lead + (slice(c, c + FF_COLS), slice(None))])
        acc = term if acc is None else acc + term
    return acc


def _ffn_kernel(h_ref, nw_ref, w1_ref, w3_ref, w2_ref, o_ref):
    h = h_ref[...]
    xb = _rms(h, nw_ref[...]).astype(BF16)
    o_ref[...] = h + _swiglu_acc(xb, w1_ref, w3_ref, w2_ref, D_FF)


def _ffn(h, norm_w, w1, w3, w2):
    t = h.shape[0]
    tm = min(ROW_TILE, t)
    row = pl.BlockSpec((tm, D_MODEL), lambda i: (i, 0))
    return pl.pallas_call(
        _ffn_kernel,
        grid=(t // tm,),
        in_specs=[row, _const_spec((1, D_MODEL)), _const_spec((D_MODEL, D_FF)),
                  _const_spec((D_MODEL, D_FF)), _const_spec((D_FF, D_MODEL))],
        out_specs=row,
        out_shape=jax.ShapeDtypeStruct((t, D_MODEL), F32),
        compiler_params=_params("parallel"),
        name="ffn",
    )(h, norm_w.astype(F32).reshape(1, -1), w1.astype(BF16), w3.astype(BF16), w2.astype(BF16))


MOE_TILE = 512
TOP_K = 2
DISPATCH_ROWS = 512
COMBINE_ROWS = 512
DMA_UNROLL = 8


def _sorted_tiles(t):
    return pl.cdiv(TOP_K * t, MOE_TILE) + N_EXPERTS


def _router_kernel(h_ref, nw_ref, wr_ref, xn_ref, member_ref, wsel_ref, esel_ref):
    xn = _rms(h_ref[...], nw_ref[...])
    xn_ref[...] = xn
    x_hi = xn.astype(BF16)
    x_lo = (xn - x_hi.astype(F32)).astype(BF16)
    w = wr_ref[...]
    w_hi = w.astype(BF16)
    w_lo = (w - w_hi.astype(F32)).astype(BF16)
    logits = _dot(x_hi, w_hi) + (_dot(x_lo, w_hi) + _dot(x_hi, w_lo))
    lane = lax.broadcasted_iota(jnp.int32, logits.shape, 1)
    logits = jnp.where(lane < N_EXPERTS, logits, -jnp.inf)
    m1 = jnp.max(logits, axis=-1, keepdims=True)
    i1 = jnp.min(jnp.where(logits == m1, lane, LANES), axis=-1, keepdims=True)
    rest = jnp.where(lane == i1, -jnp.inf, logits)
    m2 = jnp.max(rest, axis=-1, keepdims=True)
    i2 = jnp.min(jnp.where(rest == m2, lane, LANES), axis=-1, keepdims=True)
    e2 = jnp.exp(m2 - m1)
    w1 = 1.0 / (1.0 + e2)
    w2 = e2 / (1.0 + e2)
    member_ref[...] = jnp.where(lane == i1, 1.0, jnp.where(lane == i2, 1.0, 0.0)).astype(BF16)
    wsel_ref[...] = jnp.where(lane == 0, w1, jnp.where(lane == 1, w2, 0.0))
    esel_ref[...] = jnp.where(lane == 0, i1, jnp.where(lane == 1, i2, 0))


def _router(h, norm_w, w_router):
    t = h.shape[0]
    tm = min(ROW_TILE, t)
    wr = jnp.pad(w_router.astype(F32), ((0, 0), (0, LANES - N_EXPERTS)))
    row = lambda w: pl.BlockSpec((tm, w), lambda i: (i, 0))
    return pl.pallas_call(
        _router_kernel,
        grid=(t // tm,),
        in_specs=[row(D_MODEL), _const_spec((1, D_MODEL)), _const_spec((D_MODEL, LANES))],
        out_specs=[row(D_MODEL), row(LANES), row(LANES), row(LANES)],
        out_shape=[jax.ShapeDtypeStruct((t, D_MODEL), F32), jax.ShapeDtypeStruct((t, LANES), BF16),
                   jax.ShapeDtypeStruct((t, LANES), F32), jax.ShapeDtypeStruct((t, LANES), jnp.int32)],
        compiler_params=_params("parallel"),
        name="moe_router",
    )(h, norm_w.astype(F32).reshape(1, -1), wr)


def _positions_kernel(member_ref, esel_ref, pos_ref, meta_ref, cnt_ref, carry_ref, off_ref):
    phase = pl.program_id(0)
    i = pl.program_id(1)
    m = member_ref[...]
    tp = m.shape[0]
    col_sum = _dot(jnp.ones((SUBLANES, tp), BF16), m)

    @pl.when((phase == 0) & (i == 0))
    def _():
        cnt_ref[...] = jnp.zeros(cnt_ref.shape, F32)

    @pl.when(phase == 0)
    def _():
        cnt_ref[...] += col_sum

    @pl.when((phase == 1) & (i == 0))
    def _():
        tiles = jnp.floor((cnt_ref[...] + (MOE_TILE - 1.0)) * (1.0 / MOE_TILE))
        r = lax.broadcasted_iota(jnp.int32, (LANES, LANES), 0)
        c = lax.broadcasted_iota(jnp.int32, (LANES, LANES), 1)
        before = jnp.where(r < c, 1.0, 0.0).astype(BF16)
        first_tile = _dot(tiles.astype(BF16), before)
        off_ref[...] = first_tile * MOE_TILE
        carry_ref[...] = jnp.zeros(carry_ref.shape, F32)
        row = lax.broadcasted_iota(jnp.int32, (SUBLANES, LANES), 0)
        meta_ref[...] = jnp.where(row == 0, first_tile,
                                  jnp.where(row == 1, tiles, cnt_ref[...])).astype(jnp.int32)

    @pl.when(phase == 1)
    def _():
        r = lax.broadcasted_iota(jnp.int32, (tp, tp), 0)
        c = lax.broadcasted_iota(jnp.int32, (tp, tp), 1)
        earlier = jnp.where(r > c, 1.0, 0.0).astype(BF16)
        posm = off_ref[0:1, :] + carry_ref[0:1, :] + _dot(earlier, m)
        lane = lax.broadcasted_iota(jnp.int32, posm.shape, 1)
        e = esel_ref[...]
        p0 = jnp.sum(jnp.where(lane == e[:, 0:1], posm, 0.0), axis=-1, keepdims=True)
        p1 = jnp.sum(jnp.where(lane == e[:, 1:2], posm, 0.0), axis=-1, keepdims=True)
        pos_ref[...] = jnp.where(lane == 0, p0, jnp.where(lane == 1, p1, 0.0)).astype(jnp.int32)
        carry_ref[...] += col_sum


def _positions(member, esel):
    t = member.shape[0]
    tp = min(ROW_TILE, t)
    return pl.pallas_call(
        _positions_kernel,
        grid=(2, t // tp),
        in_specs=[pl.BlockSpec((tp, LANES), lambda p, i: (i, 0)),
                  pl.BlockSpec((tp, LANES), lambda p, i: (i * p, 0))],
        out_specs=[pl.BlockSpec((tp, LANES), lambda p, i: (i * p, 0)),
                   pl.BlockSpec((SUBLANES, LANES), lambda p, i: (0, 0))],
        out_shape=[jax.ShapeDtypeStruct((t, LANES), jnp.int32),
                   jax.ShapeDtypeStruct((SUBLANES, LANES), jnp.int32)],
        scratch_shapes=[pltpu.VMEM((SUBLANES, LANES), F32)] * 3,
        compiler_params=_params("arbitrary", "arbitrary"),
        name="moe_positions",
    )(member, esel)


def _row_copy(src_ref, src_row, dst_ref, dst_row, sem):
    return pltpu.make_async_copy(src_ref.at[pl.ds(src_row, 1)], dst_ref.at[pl.ds(dst_row, 1)], sem)


def _dispatch_kernel(pad_lo_ref, pad_hi_ref, pos_ref, xn_ref, xs_ref, zero_ref, sem):
    rows = xn_ref.shape[0]

    @pl.when(pl.program_id(0) == 0)
    def _():
        zero_ref[...] = jnp.zeros(zero_ref.shape, F32)
        for e in range(N_EXPERTS):
            lo, hi = pad_lo_ref[e], pad_hi_ref[e]

            def fill(r, carry):
                _row_copy(zero_ref, 0, xs_ref, r, sem).start()
                return carry

            def filled(r, carry):
                _row_copy(zero_ref, 0, xs_ref, r, sem).wait()
                return carry

            lax.fori_loop(lo, hi, fill, 0)
            lax.fori_loop(lo, hi, filled, 0)

        def tail_copy(j):
            return pltpu.make_async_copy(
                zero_ref, xs_ref.at[pl.ds(pl.multiple_of(j * SUBLANES, SUBLANES), SUBLANES)], sem)

        def tail_fill(j, carry):
            tail_copy(j).start()
            return carry

        def tail_filled(j, carry):
            tail_copy(j).wait()
            return carry

        first = lax.shift_right_logical(pad_hi_ref[N_EXPERTS - 1], int(math.log2(SUBLANES)))
        lax.fori_loop(first, xs_ref.shape[0] // SUBLANES, tail_fill, 0)
        lax.fori_loop(first, xs_ref.shape[0] // SUBLANES, tail_filled, 0)

    def copies(r):
        return [_row_copy(xn_ref, r, xs_ref, pos_ref[0, k, r], sem) for k in range(TOP_K)]

    def issue(r, carry):
        for k, cp in enumerate(copies(r)):
            cp.start(priority=k)
        return carry

    def drain(r, carry):
        for cp in copies(r):
            cp.wait()
        return carry

    lax.fori_loop(0, rows, issue, 0, unroll=DMA_UNROLL)
    lax.fori_loop(0, rows, drain, 0, unroll=DMA_UNROLL)


def _slot_major(pos, rows):
    t = pos.shape[0]
    return jnp.transpose(pos[:, :TOP_K].reshape(t // rows, rows, TOP_K), (0, 2, 1))


def _dispatch(xn, pos, pad_lo, pad_hi, n_sorted):
    t = xn.shape[0]
    rows = min(DISPATCH_ROWS, t)
    return pl.pallas_call(
        _dispatch_kernel,
        grid_spec=pltpu.PrefetchScalarGridSpec(
            num_scalar_prefetch=2,
            grid=(t // rows,),
            in_specs=[pl.BlockSpec((1, TOP_K, rows), lambda i, lo, hi: (i, 0, 0),
                                   memory_space=pltpu.SMEM),
                      pl.BlockSpec((rows, D_MODEL), lambda i, lo, hi: (i, 0))],
            out_specs=pl.BlockSpec(memory_space=pl.ANY),
            scratch_shapes=[pltpu.VMEM((SUBLANES, D_MODEL), F32), pltpu.SemaphoreType.DMA(())]),
        out_shape=jax.ShapeDtypeStruct((n_sorted, D_MODEL), F32),
        compiler_params=_params("arbitrary"),
        name="moe_dispatch",
    )(pad_lo, pad_hi, _slot_major(pos, rows), xn)


MOE_FF_SPLIT = 2
MOE_FF_BLOCK = D_FF_EXPERT // MOE_FF_SPLIT


def _experts_kernel(tile_expert_ref, n_used_ref, xs_ref, w1_ref, w3_ref, w2_ref, y_ref):
    del tile_expert_ref
    used = pl.program_id(0) < n_used_ref[0]
    f = pl.program_id(1)

    def part():
        xb = xs_ref[...].astype(BF16)
        return _swiglu_acc(xb, w1_ref, w3_ref, w2_ref, MOE_FF_BLOCK, lead=(0,))

    @pl.when(used & (f == 0))
    def _():
        y_ref[...] = part()

    @pl.when(used & (f > 0))
    def _():
        y_ref[...] += part()

    @pl.when(jnp.logical_not(used) & (f == 0))
    def _():
        y_ref[...] = jnp.zeros(y_ref.shape, F32)


def _experts(xs, tile_expert, n_used, w1, w3, w2):
    n_tiles = xs.shape[0] // MOE_TILE
    row = pl.BlockSpec((MOE_TILE, D_MODEL), lambda i, f, te, nu: (i, 0))
    row_in = pl.BlockSpec((MOE_TILE, D_MODEL), lambda i, f, te, nu: (jnp.minimum(i, nu[0] - 1), 0))
    ff = lambda i, f, nu: jnp.where(i < nu[0], f, MOE_FF_SPLIT - 1)
    w_up = pl.BlockSpec((1, D_MODEL, MOE_FF_BLOCK), lambda i, f, te, nu: (te[i], 0, ff(i, f, nu)))
    w_down = pl.BlockSpec((1, MOE_FF_BLOCK, D_MODEL), lambda i, f, te, nu: (te[i], ff(i, f, nu), 0))
    return pl.pallas_call(
        _experts_kernel,
        grid_spec=pltpu.PrefetchScalarGridSpec(
            num_scalar_prefetch=2,
            grid=(n_tiles, MOE_FF_SPLIT),
            in_specs=[row_in, w_up, w_up, w_down],
            out_specs=row),
        out_shape=jax.ShapeDtypeStruct(xs.shape, F32),
        compiler_params=_params("arbitrary", "arbitrary"),
        name="moe_experts",
    )(tile_expert, n_used, xs, w1.astype(BF16), w3.astype(BF16), w2.astype(BF16))


def _combine_kernel(pos_ref, h_ref, wsel_ref, fw_ref, y_ref, o_ref, ybuf_ref, sem, *, final_norm):
    rows = h_ref.shape[0]

    def copies(r):
        return [_row_copy(y_ref, pos_ref[0, k, r], ybuf_ref.at[k], r, sem) for k in range(TOP_K)]

    def issue(r, carry):
        for k, cp in enumerate(copies(r)):
            cp.start(priority=k)
        return carry

    def drain(r, carry):
        for cp in copies(r):
            cp.wait()
        return carry

    lax.fori_loop(0, rows, issue, 0, unroll=DMA_UNROLL)
    lax.fori_loop(0, rows, drain, 0, unroll=DMA_UNROLL)
    w = wsel_ref[...]
    out = h_ref[...] + w[:, 0:1] * ybuf_ref[0] + w[:, 1:2] * ybuf_ref[1]
    if final_norm:
        out = _rms(out, fw_ref[...])
    o_ref[...] = out


def _combine(h, wsel, pos, y_sorted, final_w, final_norm):
    t = h.shape[0]
    rows = min(COMBINE_ROWS, t)
    return pl.pallas_call(
        functools.partial(_combine_kernel, final_norm=final_norm),
        grid=(t // rows,),
        in_specs=[pl.BlockSpec((1, TOP_K, rows), lambda i: (i, 0, 0), memory_space=pltpu.SMEM),
                  pl.BlockSpec((rows, D_MODEL), lambda i: (i, 0)),
                  pl.BlockSpec((rows, LANES), lambda i: (i, 0)),
                  _const_spec((1, D_MODEL)),
                  pl.BlockSpec(memory_space=pl.ANY)],
        out_specs=pl.BlockSpec((rows, D_MODEL), lambda i: (i, 0)),
        out_shape=jax.ShapeDtypeStruct((t, D_MODEL), F32),
        scratch_shapes=[pltpu.VMEM((TOP_K, rows, D_MODEL), F32), pltpu.SemaphoreType.DMA(())],
        compiler_params=_params("arbitrary"),
        name="moe_combine",
    )(_slot_major(pos, rows), h, wsel, final_w.astype(F32).reshape(1, -1), y_sorted)


def _moe(h, norm_w, w_router, w1, w3, w2, final_w, final_norm):
    t = h.shape[0]
    n_tiles = _sorted_tiles(t)
    xn, member, wsel, esel = _router(h, norm_w, w_router)
    pos, meta = _positions(member, esel)
    first_tile = meta[0, :N_EXPERTS]
    last_tile = first_tile + meta[1, :N_EXPERTS]
    tile = jnp.arange(n_tiles, dtype=jnp.int32)
    tile_expert = jnp.minimum(jnp.sum(tile[:, None] >= last_tile[None, :], axis=1), N_EXPERTS - 1)
    pad_lo = first_tile * MOE_TILE + meta[2, :N_EXPERTS]
    xs = _dispatch(xn, pos, pad_lo, last_tile * MOE_TILE, n_tiles * MOE_TILE)
    ys = _experts(xs, tile_expert.astype(jnp.int32), last_tile[N_EXPERTS - 1:], w1, w3, w2)
    return _combine(h, wsel, pos, ys, final_w, final_norm)


def _final_norm_kernel(h_ref, w_ref, o_ref):
    o_ref[...] = _rms(h_ref[...], w_ref[...])


def _final_norm(h, w):
    t = h.shape[0]
    tm = min(ROW_TILE, t)
    row = pl.BlockSpec((tm, D_MODEL), lambda i: (i, 0))
    return pl.pallas_call(
        _final_norm_kernel, grid=(t // tm,), in_specs=[row, _const_spec((1, D_MODEL))],
        out_specs=row, out_shape=jax.ShapeDtypeStruct((t, D_MODEL), F32),
        compiler_params=_params("parallel"), name="final_norm",
    )(h, w.astype(F32).reshape(1, -1))


def _mixing_block(h, bsz, seq, cos2, sin2, norm_w, w_in, b_gate,
                  s5_lam_re, s5_lam_im, s5_b_re, s5_b_im, s5_c_re, s5_c_im, s5_d, s5_log_dt, s5_w_glu,
                  ssd_conv_w, ssd_conv_b, ssd_dt_bias, ssd_a_log, ssd_d, ssd_norm,
                  ret_norm,
                  lru_conv_w, lru_conv_b, lru_wa, lru_ba, lru_wx, lru_bx, lru_lam,
                  w_branch, w_out):
    w_cat, w_gate_half = _pack_w_in(w_in)
    (u_s5, z_ssd, xbc_ssd, dt_ssd, q_ret, k_ret, v_ret, g_ret, x_lru,
     gate_lru) = _inproj(h, norm_w, w_cat)
    seq3 = lambda a: a.reshape(bsz, seq, a.shape[-1])
    tables = _s5_tables(s5_lam_re, s5_lam_im, s5_b_re, s5_b_im, s5_c_re, s5_c_im, s5_log_dt,
                        seq // S5_SUB)
    y_s5 = _s5_scan(u_s5, tables, bsz, seq)
    y_ssd = _ssd(seq3(z_ssd), seq3(xbc_ssd), seq3(dt_ssd), ssd_conv_w, ssd_conv_b, ssd_dt_bias,
                 ssd_a_log, ssd_d, ssd_norm)
    y_ret = _retention(seq3(q_ret), seq3(k_ret), seq3(v_ret), seq3(g_ret), cos2, sin2, ret_norm)
    y_lru = _lru(seq3(x_lru), seq3(gate_lru), lru_conv_w, lru_conv_b, lru_wa, lru_ba, lru_wx,
                 lru_bx, lru_lam)
    flat = lambda a: a.reshape(bsz * seq, a.shape[-1])
    return _merge(h, norm_w, u_s5, y_s5, flat(y_ssd), flat(y_ret), flat(y_lru), w_gate_half,
                  b_gate, s5_d, s5_w_glu, w_branch, w_out)


def kernel(x, mem, positions, norm_mix, w_in, b_gate, s5_lam_re, s5_lam_im, s5_b_re, s5_b_im, s5_c_re, s5_c_im, s5_d, s5_log_dt, s5_w_glu, ssd_conv_w, ssd_conv_b, ssd_dt_bias, ssd_a_log, ssd_d, ssd_norm, ret_norm, lru_conv_w, lru_conv_b, lru_wa, lru_ba, lru_wx, lru_bx, lru_lam, w_branch, w_out, norm_xa, norm_mem, xa_wq, xa_wk, xa_wv, xa_wo, norm_ffn, ffn_w1, ffn_w3, ffn_w2, moe_router, moe_w1, moe_w3, moe_w2, norm_final):
    bsz, seq, _ = x.shape
    depth = norm_mix.shape[0]
    cos2, sin2 = _rope_tables(positions)
    h = x.reshape(bsz * seq, D_MODEL)
    for i in range(depth):
        h = _mixing_block(h, bsz, seq, cos2, sin2, norm_mix[i], w_in[i], b_gate[i],
                          s5_lam_re[i], s5_lam_im[i], s5_b_re[i], s5_b_im[i], s5_c_re[i], s5_c_im[i],
                          s5_d[i], s5_log_dt[i], s5_w_glu[i],
                          ssd_conv_w[i], ssd_conv_b[i], ssd_dt_bias[i], ssd_a_log[i], ssd_d[i],
                          ssd_norm[i], ret_norm[i],
                          lru_conv_w[i], lru_conv_b[i], lru_wa[i], lru_ba[i], lru_wx[i], lru_bx[i],
                          lru_lam[i], w_branch[i], w_out[i])
        k, v = _kv(mem, norm_mem[i], xa_wk[i], xa_wv[i])
        h = _xattn(h.reshape(bsz, seq, D_MODEL), norm_xa[i], xa_wq[i], k, v, xa_wo[i])
        h = h.reshape(bsz * seq, D_MODEL)
        last = i == depth - 1
        if i % 2 == 0:
            h = _ffn(h, norm_ffn[i], ffn_w1[i // 2], ffn_w3[i // 2], ffn_w2[i // 2])
            if last:
                h = _final_norm(h, norm_final)
        else:
            h = _moe(h, norm_ffn[i], moe_router[i // 2], moe_w1[i // 2], moe_w3[i // 2],
                     moe_w2[i // 2], norm_final, last)
    return h.reshape(bsz, seq, D_MODEL)
```

```python
import functools
import math

import numpy as np
import jax
import jax.numpy as jnp
from jax import lax
from jax.experimental import pallas as pl
from jax.experimental.pallas import tpu as pltpu

F32 = jnp.float32
BF16 = jnp.bfloat16

D_MODEL = 1024
N_MEM = 256
EPS = 1e-6
CHUNK = 128
N_BRANCH = 4
S5_WIDTH = 512
S5_GROUP = 16
S5_GROUPS = 32
S5_STATE = 64
S5_SUB = 16
SSD_HEADS = 8
SSD_HEAD_DIM = 64
SSD_WIDTH = 512
SSD_GROUPS = 2
SSD_STATE = 64
SSD_CONV = 4
SSD_CONV_DIM = SSD_WIDTH + 2 * SSD_GROUPS * SSD_STATE
RET_HEADS = 8
RET_HEAD_DIM = 64
RET_WIDTH = 512
ROPE_BASE = 10000.0
LRU_WIDTH = 512
LRU_BLOCKS = 8
LRU_BLOCK = 64
LRU_CONV = 4
LRU_C = 8.0
XA_HEADS = 4
XA_HEAD_DIM = 256
D_FF = 2816
N_EXPERTS = 8
D_FF_EXPERT = 3584
SECTION_WIDTHS = (S5_WIDTH, SSD_WIDTH, SSD_CONV_DIM, SSD_HEADS,
                  RET_WIDTH, RET_WIDTH, RET_WIDTH, RET_WIDTH,
                  LRU_WIDTH, LRU_WIDTH, N_BRANCH * D_MODEL)

LANES = 128
SUBLANES = 8
VMEM_LIMIT = 56 * 1024 * 1024
ROW_TILE = 512
LRU_ROWS = 256
LRU_SCAN_ROWS = 8


def _params(*sem):
    return pltpu.CompilerParams(dimension_semantics=sem, vmem_limit_bytes=VMEM_LIMIT)


def _const_spec(shape):
    nd = len(shape)
    return pl.BlockSpec(shape, lambda *_: (0,) * nd)


def _rms(x, w):
    return x * lax.rsqrt(jnp.mean(x * x, axis=-1, keepdims=True) + EPS) * w


def _sigmoid(x):
    return 0.5 + 0.5 * jnp.tanh(0.5 * x)


def _silu(x):
    return x * _sigmoid(x)


def _gelu(x):
    return jax.nn.gelu(x)


def _softplus(x):
    return jnp.maximum(x, 0.0) + jnp.log(1.0 + jnp.exp(-jnp.abs(x)))


def _dot(a, b):
    return jnp.dot(a, b, preferred_element_type=F32)


def _dot_nt(a, b):
    return lax.dot_general(a, b, (((1,), (1,)), ((), ())), preferred_element_type=F32)


def _split3(x):
    hi = x.astype(BF16)
    r1 = x - hi.astype(F32)
    mid = r1.astype(BF16)
    lo = (r1 - mid.astype(F32)).astype(BF16)
    return hi, mid, lo


def _split_dot(x, m_bf16):
    hi, mid, lo = _split3(x)
    return _dot(hi, m_bf16) + _dot(mid, m_bf16) + _dot(lo, m_bf16)


def _split_dot_left(m_bf16, x):
    hi, mid, lo = _split3(x)
    return _dot(m_bf16, hi) + _dot(m_bf16, mid) + _dot(m_bf16, lo)


def _shift_rows(x, k, fill=0.0):
    rows = x.shape[0]
    if k % SUBLANES == 0:
        return jnp.concatenate([jnp.full((k, x.shape[1]), fill, x.dtype), x[:rows - k]], axis=0)
    rolled = pltpu.roll(x, k, 0)
    row = lax.broadcasted_iota(jnp.int32, x.shape, 0)
    return jnp.where(row >= k, rolled, fill)


IN_WIDTHS = (S5_WIDTH, SSD_WIDTH, SSD_CONV_DIM, LANES,
             RET_WIDTH, RET_WIDTH, RET_WIDTH, RET_WIDTH,
             LRU_WIDTH, LRU_WIDTH)
IN_TOTAL_PADDED = sum(IN_WIDTHS)
DOT_COLS = 512


def _inproj_kernel(h_ref, nw_ref, w_ref, u_ref, *refs):
    out_refs, uscr_ref = refs[:-1], refs[-1]
    xb = _rms(h_ref[...], nw_ref[...]).astype(BF16)
    u = _dot(xb, w_ref[:, :S5_WIDTH])
    folded = u.shape[0] // S5_SUB
    for q in range(S5_Q):
        uscr_ref[q] = u[:, q * LANES:(q + 1) * LANES]
        for t in range(S5_SUB):
            u_ref[q, :, t * LANES:(t + 1) * LANES] = uscr_ref[
                q, pl.ds(t, folded, stride=S5_SUB), :].astype(u_ref.dtype)
    off = S5_WIDTH
    for o_ref, width in zip(out_refs, IN_WIDTHS[1:]):
        for c in range(0, width, DOT_COLS):
            n = min(DOT_COLS, width - c)
            o_ref[:, c:c + n] = _dot(xb, w_ref[:, off + c:off + c + n]).astype(o_ref.dtype)
        off += width


def _inproj(h, norm_w, w_cat):
    t = h.shape[0]
    tm = min(ROW_TILE, t)
    out_shape = ([jax.ShapeDtypeStruct((S5_Q, t // S5_SUB, S5_ROW), BF16)]
                 + [jax.ShapeDtypeStruct((t, w), BF16) for w in IN_WIDTHS[1:]])
    out_specs = ([pl.BlockSpec((S5_Q, tm // S5_SUB, S5_ROW), lambda i: (0, i, 0))]
                 + [pl.BlockSpec((tm, w), lambda i: (i, 0)) for w in IN_WIDTHS[1:]])
    return pl.pallas_call(
        _inproj_kernel,
        grid=(t // tm,),
        in_specs=[pl.BlockSpec((tm, D_MODEL), lambda i: (i, 0)),
                  _const_spec((1, D_MODEL)),
                  _const_spec((D_MODEL, IN_TOTAL_PADDED))],
        out_specs=out_specs,
        out_shape=out_shape,
        scratch_shapes=[pltpu.VMEM((S5_Q, tm, LANES), F32)],
        compiler_params=_params("parallel"),
        name="inproj",
    )(h, norm_w.reshape(1, D_MODEL), w_cat)


def _pack_w_in(w_in):
    pieces = []
    off = 0
    for width in SECTION_WIDTHS[:-1]:
        sec = w_in[:, off:off + width]
        if width == SSD_HEADS:
            sec = jnp.pad(sec, ((0, 0), (0, LANES - width)))
        pieces.append(sec)
        off += width
    w_gate = (0.5 * w_in[:, off:]).astype(BF16)
    return jnp.concatenate(pieces, axis=1).astype(BF16), w_gate


S5_Q = S5_WIDTH // LANES
S5_QG = S5_GROUPS // S5_Q
S5_QS = S5_QG * S5_STATE
S5_ROW = S5_SUB * LANES
S5_TILE = 256
S5_SEQS = 8


def _s5_expand_tables(wc_ref, xc_ref, rin_ref, rout_ref, win_ref, wx_ref):
    group_shift = int(math.log2(S5_GROUP))
    state_shift = int(math.log2(S5_STATE))
    for c in range(0, 2 * S5_QS, DOT_COLS):
        r = lax.broadcasted_iota(jnp.int32, (S5_ROW, DOT_COLS), 0)
        col = lax.broadcasted_iota(jnp.int32, (S5_ROW, DOT_COLS), 1) + c
        own = ((r >> group_shift) & (S5_QG - 1)) == ((col & (S5_QS - 1)) >> state_shift)
        rep = _dot(wc_ref[0], rin_ref[:, c:c + DOT_COLS])
        win_ref[:, c:c + DOT_COLS] = jnp.where(own, rep, 0.0).astype(BF16)
    for c in range(0, S5_ROW, DOT_COLS):
        r = lax.broadcasted_iota(jnp.int32, (2 * S5_QS, DOT_COLS), 0)
        col = lax.broadcasted_iota(jnp.int32, (2 * S5_QS, DOT_COLS), 1) + c
        own = ((r & (S5_QS - 1)) >> state_shift) == ((col >> group_shift) & (S5_QG - 1))
        rep = _dot(xc_ref[0], rout_ref[:, c:c + DOT_COLS])
        wx_ref[:, c:c + DOT_COLS] = jnp.where(own, rep, 0.0).astype(BF16)


def _s5_kernel(u_ref, toe_ref, wc_ref, xc_ref, rin_ref, rout_ref, pw_ref, y_ref,
               win_ref, wx_ref, x_ref, sp_ref, *, rows):
    @pl.when(pl.program_id(1) == 0)
    def _():
        _s5_expand_tables(wc_ref, xc_ref, rin_ref, rout_ref, win_ref, wx_ref)

    n_steps = int(math.log2(rows))
    for c in range(0, 2 * S5_QS, DOT_COLS):
        x_ref[:, c:c + DOT_COLS] = _dot(u_ref[0], win_ref[:, c:c + DOT_COLS])

    def seq(b, carry):
        r0 = pl.multiple_of(b * rows, rows)
        for lc in range(S5_QS // LANES):
            re = slice(lc * LANES, (lc + 1) * LANES)
            im = slice(S5_QS + lc * LANES, S5_QS + (lc + 1) * LANES)
            sr = x_ref[pl.ds(r0, rows), re]
            si = x_ref[pl.ds(r0, rows), im]
            for j in range(n_steps):
                k = 1 << j
                pr = pw_ref[0, 2 * j:2 * j + 1, re]
                pi = pw_ref[0, 2 * j + 1:2 * j + 2, re]
                shr = _shift_rows(sr, k)
                shi = _shift_rows(si, k)
                sr, si = sr + pr * shr - pi * shi, si + pr * shi + pi * shr
            sp_ref[pl.ds(r0, rows), re] = _shift_rows(sr, 1).astype(BF16)
            sp_ref[pl.ds(r0, rows), im] = _shift_rows(si, 1).astype(BF16)
        return carry

    lax.fori_loop(0, u_ref.shape[1] // rows, seq, 0)
    for nn in range(S5_ROW // S5_TILE):
        cols = slice(nn * S5_TILE, (nn + 1) * S5_TILE)
        acc = _dot(sp_ref[...], wx_ref[:, cols])
        for kk in range(nn + 1):
            acc = acc + _dot(u_ref[0, :, kk * S5_TILE:(kk + 1) * S5_TILE], toe_ref[0, nn - kk])
        y_ref[0, :, cols] = acc.astype(y_ref.dtype)


def _s5_tables(lam_re, lam_im, b_re, b_im, c_re, c_im, log_dt, rows):
    lr = lam_re.astype(F32)
    li = lam_im.astype(F32)
    step = jnp.exp(log_dt.astype(F32))[:, None]
    mag = jnp.exp(lr * step)
    ar = mag * jnp.cos(li * step)
    ai = mag * jnp.sin(li * step)
    inv = 1.0 / (lr * lr + li * li)
    cr = ((ar - 1.0) * lr + ai * li) * inv
    ci = (ai * lr - (ar - 1.0) * li) * inv
    bbr = cr[..., None] * b_re - ci[..., None] * b_im
    bbi = cr[..., None] * b_im + ci[..., None] * b_re

    def apow(e):
        e = jnp.asarray(e, F32)[:, None, None]
        m = jnp.exp(lr * step * e)
        return m * jnp.cos(li * step * e), m * jnp.sin(li * step * e)

    sub = S5_SUB
    pr, pi = apow(np.arange(sub + 1))
    m_r = pr[:sub, :, :, None] * bbr - pi[:sub, :, :, None] * bbi
    m_i = pr[:sub, :, :, None] * bbi + pi[:sub, :, :, None] * bbr
    kern = (jnp.einsum('gon,tgni->tgoi', c_re, m_r)
            - jnp.einsum('gon,tgni->tgoi', c_im, m_i))
    eye = jnp.eye(S5_QG, dtype=F32)
    blk = lambda a: a.reshape(a.shape[0], S5_Q, S5_QG, *a.shape[2:])
    kern_bd = jnp.einsum('tqgoi,gh->tqgiho', blk(kern), eye).reshape(sub, S5_Q, LANES, LANES)
    kern_bd = jnp.concatenate([jnp.zeros_like(kern_bd[:1]), kern_bd], axis=0)
    steps_per_tile = S5_TILE // LANES
    d = np.arange(sub // steps_per_tile)[:, None, None]
    tau = steps_per_tile * d + np.arange(steps_per_tile)[None, None, :] - np.arange(steps_per_tile)[None, :, None]
    toe = kern_bd[tau + 1]
    toe = jnp.transpose(toe, (3, 0, 1, 4, 2, 5)).reshape(S5_Q, sub // steps_per_tile, S5_TILE, S5_TILE)
    er, ei = pr[sub - 1 - np.arange(sub)], pi[sub - 1 - np.arange(sub)]
    w_r = er[..., None] * bbr - ei[..., None] * bbi
    w_i = er[..., None] * bbi + ei[..., None] * bbr
    fold_in = lambda w: jnp.transpose(blk(w), (1, 0, 2, 4, 3)).reshape(S5_Q, S5_ROW, S5_STATE)
    wc = jnp.concatenate([fold_in(w_r), fold_in(w_i)], axis=2)
    qr, qi = pr[1:], pi[1:]
    x_r = c_re[None] * qr[:, :, None, :] - c_im[None] * qi[:, :, None, :]
    x_i = -(c_re[None] * qi[:, :, None, :] + c_im[None] * qr[:, :, None, :])
    fold_out = lambda w: jnp.transpose(blk(w), (1, 2, 4, 0, 3)).reshape(S5_Q, S5_QS, sub * S5_GROUP)
    xc = jnp.concatenate([fold_out(x_r), fold_out(x_i)], axis=1)
    n_steps = int(math.log2(rows))
    sr_, si_ = apow(sub * (2.0 ** np.arange(n_steps)))
    pw = jnp.stack([sr_, si_], axis=1).reshape(2 * n_steps, S5_Q, S5_QS)
    pw = jnp.transpose(pw, (1, 0, 2))
    pw = jnp.pad(pw, ((0, 0), (0, 2 * SUBLANES - 2 * n_steps), (0, 0)))
    return toe.astype(BF16), wc.astype(BF16), xc.astype(BF16), pw


def _s5_replicators():
    k = np.arange(2 * S5_STATE)[:, None]
    c = np.arange(2 * S5_QS)[None, :]
    rin = ((k // S5_STATE) == (c // S5_QS)) & ((k % S5_STATE) == (c % S5_STATE))
    r = np.arange(S5_SUB * S5_GROUP)[:, None]
    c = np.arange(S5_ROW)[None, :]
    rout = ((r // S5_GROUP) == (c // LANES)) & ((r % S5_GROUP) == (c % S5_GROUP))
    return jnp.asarray(rin, BF16), jnp.asarray(rout, BF16)


def _s5_scan(u5, tables, bsz, seq):
    toe, wc, xc, pw = tables
    rin, rout = _s5_replicators()
    rows = seq // S5_SUB
    nb = math.gcd(S5_SEQS, bsz)
    row_blk = pl.BlockSpec((1, nb * rows, S5_ROW), lambda q, j: (q, j, 0))
    per_q = lambda a: pl.BlockSpec((1,) + a.shape[1:], lambda q, j: (q,) + (0,) * (a.ndim - 1))
    const = lambda a: pl.BlockSpec(a.shape, lambda q, j: (0,) * a.ndim)
    return pl.pallas_call(
        functools.partial(_s5_kernel, rows=rows),
        grid=(S5_Q, bsz // nb),
        in_specs=[row_blk, per_q(toe), per_q(wc), per_q(xc), const(rin), const(rout), per_q(pw)],
        out_specs=row_blk,
        out_shape=jax.ShapeDtypeStruct(u5.shape, BF16),
        scratch_shapes=[pltpu.VMEM((S5_ROW, 2 * S5_QS), BF16),
                        pltpu.VMEM((2 * S5_QS, S5_ROW), BF16),
                        pltpu.VMEM((nb * rows, 2 * S5_QS), F32),
                        pltpu.VMEM((nb * rows, 2 * S5_QS), BF16)],
        compiler_params=_params("parallel", "arbitrary"),
        name="s5_scan",
    )(u5, toe, wc, xc, rin, rout, pw)


CONV_PAD = SUBLANES
SEQS_PER_STEP = 2


def _causal_conv(xpad_ref, x, w, b):
    rows = x.shape[0]
    xpad_ref[CONV_PAD:CONV_PAD + rows, :] = x
    k = w.shape[0]
    acc = b
    for j in range(k):
        s = CONV_PAD - (k - 1) + j
        acc = acc + w[j:j + 1, :] * xpad_ref[s:s + rows, :]
    xpad_ref[0:CONV_PAD, :] = xpad_ref[rows:rows + CONV_PAD, :]
    return acc


def _ssd_kernel(z_ref, xbc_ref, dt_ref, cw_ref, cb_ref, dtb_ref, a_ref, d_ref, nw_ref, tri_ref,
                o_ref, xpad_ref, state_ref, y_ref):
    @pl.when(pl.program_id(1) == 0)
    def _():
        state_ref[...] = jnp.zeros(state_ref.shape, F32)
        xpad_ref[:, 0:CONV_PAD, :] = jnp.zeros((xpad_ref.shape[0], CONV_PAD, xpad_ref.shape[2]), F32)

    for s in range(z_ref.shape[0]):
        _ssd_chunk(z_ref.at[s], xbc_ref.at[s], dt_ref.at[s], cw_ref, cb_ref, dtb_ref, a_ref, d_ref,
                   nw_ref, tri_ref, o_ref.at[s], xpad_ref.at[s], state_ref.at[s], y_ref.at[s])


def _ssd_chunk(z_ref, xbc_ref, dt_ref, cw_ref, cb_ref, dtb_ref, a_ref, d_ref, nw_ref, tri_ref,
               o_ref, xpad_ref, state_ref, y_ref):
    xbc = _silu(_causal_conv(xpad_ref, xbc_ref[...].astype(F32), cw_ref[...], cb_ref[...]))
    xs_t = xbc[:, :SSD_WIDTH].T
    bs = xbc[:, SSD_WIDTH:SSD_WIDTH + LANES]
    cs_t = xbc[:, SSD_WIDTH + LANES:].T.astype(BF16)
    lane = lax.broadcasted_iota(jnp.int32, bs.shape, 1)
    dt = _softplus(dt_ref[...].astype(F32) + dtb_ref[...])
    acum = _split_dot_left(tri_ref[...], dt * a_ref[...])
    dt_t = dt.T
    acum_t = acum.T
    later = (lax.broadcasted_iota(jnp.int32, (CHUNK, CHUNK), 1)
             >= lax.broadcasted_iota(jnp.int32, (CHUNK, CHUNK), 0))
    heads_per_group = SSD_HEADS // SSD_GROUPS
    b_g, cb_t = [], []
    for g in range(SSD_GROUPS):
        in_group = (lane >= g * SSD_STATE) & (lane < (g + 1) * SSD_STATE)
        b_g.append(jnp.where(in_group, bs, 0.0).astype(BF16))
        cb_t.append(_dot(b_g[g], cs_t))
    x_t, a_row, last, m_t, xdt_t, xw_t = [], [], [], [], [], []
    for hd in range(SSD_HEADS):
        x_t.append(xs_t[hd * SSD_HEAD_DIM:(hd + 1) * SSD_HEAD_DIM, :])
        a_row.append(acum_t[hd:hd + 1, :])
        last.append(acum_t[hd:hd + 1, CHUNK - 1:CHUNK])
        l_t = jnp.exp(jnp.where(later, a_row[hd] - acum[:, hd:hd + 1], -jnp.inf))
        m_t.append((cb_t[hd // heads_per_group] * l_t).astype(BF16))
        xdt = x_t[hd] * dt_t[hd:hd + 1, :]
        xdt_t.append(xdt.astype(BF16))
        xw_t.append((xdt * jnp.exp(last[hd] - a_row[hd])).astype(BF16))
    prev = [state_ref[hd] for hd in range(SSD_HEADS)]
    y_in = [_dot(xdt_t[hd], m_t[hd]) for hd in range(SSD_HEADS)]
    y_off = [_dot(prev[hd].astype(BF16), cs_t) for hd in range(SSD_HEADS)]
    new = [_dot(xw_t[hd], b_g[hd // heads_per_group]) for hd in range(SSD_HEADS)]
    for hd in range(SSD_HEADS):
        state_ref[hd] = prev[hd] * jnp.exp(last[hd]) + new[hd]
        y_ref[hd * SSD_HEAD_DIM:(hd + 1) * SSD_HEAD_DIM, :] = (
            y_in[hd] + y_off[hd] * jnp.exp(a_row[hd]) + d_ref[:, hd:hd + 1] * x_t[hd])
    y = y_ref[...].T * _silu(z_ref[...].astype(F32))
    o_ref[...] = _rms(y, nw_ref[...]).astype(o_ref.dtype)


def _ssd(z, xbc, dt, conv_w, conv_b, dt_bias, a_log, d_skip, norm_w):
    bsz, seq, _ = z.shape
    nc = seq // CHUNK
    ns = math.gcd(SEQS_PER_STEP, bsz)

    def lane_pad(v):
        return jnp.pad(v.astype(F32).reshape(1, -1), ((0, 0), (0, LANES - v.shape[-1])))

    tri = jnp.asarray(np.tril(np.ones((CHUNK, CHUNK), np.float32)), BF16)
    blk = lambda w: pl.BlockSpec((ns, CHUNK, w), lambda b, c: (b, c, 0))
    return pl.pallas_call(
        _ssd_kernel,
        grid=(bsz // ns, nc),
        in_specs=[blk(SSD_WIDTH), blk(SSD_CONV_DIM), blk(LANES),
                  _const_spec((SSD_CONV, SSD_CONV_DIM)), _const_spec((1, SSD_CONV_DIM)),
                  _const_spec((1, LANES)), _const_spec((1, LANES)), _const_spec((1, LANES)),
                  _const_spec((1, SSD_WIDTH)), _const_spec((CHUNK, CHUNK))],
        out_specs=blk(SSD_WIDTH),
        out_shape=jax.ShapeDtypeStruct((bsz, seq, SSD_WIDTH), BF16),
        scratch_shapes=[pltpu.VMEM((ns, CHUNK + CONV_PAD, SSD_CONV_DIM), F32),
                        pltpu.VMEM((ns, SSD_HEADS, SSD_HEAD_DIM, SSD_GROUPS * SSD_STATE), F32),
                        pltpu.VMEM((ns, SSD_WIDTH, CHUNK), F32)],
        compiler_params=_params("parallel", "arbitrary"),
        name="ssd",
    )(z, xbc, dt, conv_w.astype(F32), conv_b.astype(F32).reshape(1, -1),
      lane_pad(dt_bias), lane_pad(-jnp.exp(a_log.astype(F32))), lane_pad(d_skip),
      norm_w.astype(F32).reshape(1, -1), tri)


GN_TILE = 256


def _head_means(x, avg_bf16):
    hi = x.astype(BF16)
    lo = (x - hi.astype(F32)).astype(BF16)
    out = []
    for c in range(0, RET_WIDTH, GN_TILE):
        out.append(_dot(hi[:, c:c + GN_TILE], avg_bf16) + _dot(lo[:, c:c + GN_TILE], avg_bf16))
    return jnp.concatenate(out, axis=1)


def _retention_tables():
    h = np.arange(RET_HEADS, dtype=np.float64)
    log_gamma = np.log1p(-np.exp2(-5.0 - h))
    idx = np.arange(CHUNK, dtype=np.float64)
    diff = idx[:, None] - idx[None, :]
    dmat = np.where(diff >= 0, np.exp(np.maximum(diff, 0.0)[None] * log_gamma[:, None, None]), 0.0)
    k_decay = np.exp((CHUNK - 1.0 - idx)[:, None] * log_gamma)
    q_decay = np.exp((idx + 1.0)[:, None] * log_gamma)
    c_decay = np.exp(CHUNK * log_gamma)[None, :]
    rep = lambda a: np.repeat(a, RET_HEAD_DIM, axis=1).astype(np.float32)
    avg = np.kron(np.eye(GN_TILE // RET_HEAD_DIM),
                  np.full((RET_HEAD_DIM, RET_HEAD_DIM), 1.0 / RET_HEAD_DIM))
    return (dmat.astype(np.float32), rep(k_decay * RET_HEAD_DIM ** -0.5), rep(q_decay),
            rep(c_decay), avg.astype(np.float32))


def _rope(x, cos, sin_signed):
    half = RET_HEAD_DIM // 2
    width = x.shape[1]
    fwd = pltpu.roll(x, half, 1)
    bwd = pltpu.roll(x, width - half, 1)
    lane = lax.broadcasted_iota(jnp.int32, x.shape, 1)
    swapped = jnp.where((lane % RET_HEAD_DIM) < half, bwd, fwd)
    return x * cos + swapped * sin_signed


def _retention_kernel(q_ref, k_ref, v_ref, g_ref, cos_ref, sin_ref, dmat_ref, kdec_ref, qdec_ref,
                      cdec_ref, avg_ref, gn_ref, o_ref, state_ref):
    @pl.when(pl.program_id(1) == 0)
    def _():
        state_ref[...] = jnp.zeros(state_ref.shape, F32)

    for s in range(q_ref.shape[0]):
        _retention_chunk(q_ref.at[s], k_ref.at[s], v_ref.at[s], g_ref.at[s], cos_ref.at[s],
                         sin_ref.at[s], dmat_ref, kdec_ref, qdec_ref, cdec_ref, avg_ref, gn_ref,
                         o_ref.at[s], state_ref.at[s])


def _retention_chunk(q_ref, k_ref, v_ref, g_ref, cos_ref, sin_ref, dmat_ref, kdec_ref, qdec_ref,
                     cdec_ref, avg_ref, gn_ref, o_ref, state_ref):
    reps = RET_WIDTH // LANES
    cos = jnp.tile(cos_ref[...], (1, reps))
    sin = jnp.tile(sin_ref[...], (1, reps))
    q = _rope(q_ref[...].astype(F32), cos, sin)
    k = _rope(k_ref[...].astype(F32), cos, sin)
    v = v_ref[...].astype(F32)
    kd_t = (k * kdec_ref[...]).T.astype(BF16)
    k_tb = (k.T * (RET_HEAD_DIM ** -0.5)).astype(BF16)
    qd = qdec_ref[...]
    cd = cdec_ref[...]
    first = lax.broadcasted_iota(jnp.int32, (CHUNK, LANES), 1) < RET_HEAD_DIM
    r = lax.broadcasted_iota(jnp.int32, (LANES, LANES), 0) < RET_HEAD_DIM
    c = lax.broadcasted_iota(jnp.int32, (LANES, LANES), 1) < RET_HEAD_DIM
    same_head = r == c
    n_pairs = RET_HEADS // 2
    lanes = [slice(p * LANES, (p + 1) * LANES) for p in range(n_pairs)]
    keeps = (first, jnp.logical_not(first))
    prev = [state_ref[p] for p in range(n_pairs)]
    q_h = [jnp.where(keeps[hd % 2], q[:, lanes[hd // 2]], 0.0).astype(BF16) for hd in range(RET_HEADS)]
    v_h = [jnp.where(keeps[hd % 2], v[:, lanes[hd // 2]], 0.0).astype(BF16) for hd in range(RET_HEADS)]
    scores = [(_dot(q_h[hd], k_tb[lanes[hd // 2], :]) * dmat_ref[hd]).astype(BF16)
              for hd in range(RET_HEADS)]
    inner = [_dot(scores[hd], v_h[hd]) for hd in range(RET_HEADS)]
    cross = [_dot(q[:, sl].astype(BF16), prev[p].astype(BF16)) for p, sl in enumerate(lanes)]
    kv = [_dot(kd_t[sl, :], v[:, sl].astype(BF16)) for sl in lanes]
    for p, sl in enumerate(lanes):
        state_ref[p] = prev[p] * cd[:, sl] + jnp.where(same_head, kv[p], 0.0)
    y = jnp.concatenate([cross[p] * qd[:, sl] + inner[2 * p] + inner[2 * p + 1]
                         for p, sl in enumerate(lanes)], axis=1)
    mu = _head_means(y, avg_ref[...])
    yc = y - mu
    var = _head_means(yc * yc, avg_ref[...])
    yn = yc * lax.rsqrt(var + EPS) * gn_ref[...]
    o_ref[...] = (_silu(g_ref[...].astype(F32)) * yn).astype(o_ref.dtype)


def _retention(q, k, v, g, cos2, sin2, gn_w):
    bsz, seq, _ = q.shape
    nc = seq // CHUNK
    ns = math.gcd(SEQS_PER_STEP, bsz)
    dmat, kdec, qdec, cdec, avg = (jnp.asarray(a) for a in _retention_tables())
    avg = avg.astype(BF16)
    blk = lambda w: pl.BlockSpec((ns, CHUNK, w), lambda b, c: (b, c, 0))
    return pl.pallas_call(
        _retention_kernel,
        grid=(bsz // ns, nc),
        in_specs=[blk(RET_WIDTH)] * 4 + [blk(LANES)] * 2 + [
            _const_spec(dmat.shape), _const_spec(kdec.shape), _const_spec(qdec.shape),
            _const_spec(cdec.shape), _const_spec(avg.shape), _const_spec((1, RET_WIDTH))],
        out_specs=blk(RET_WIDTH),
        out_shape=jax.ShapeDtypeStruct((bsz, seq, RET_WIDTH), BF16),
        scratch_shapes=[pltpu.VMEM((ns, RET_HEADS // 2, LANES, LANES), F32)],
        compiler_params=_params("parallel", "arbitrary"),
        name="retention",
    )(q, k, v, g, cos2, sin2, dmat, kdec, qdec, cdec, avg, gn_w.astype(F32).reshape(1, -1))


def _rope_tables(positions):
    half = RET_HEAD_DIM // 2
    inv_freq = ROPE_BASE ** (-jnp.arange(half, dtype=F32) / half)
    ang = positions.astype(F32)[..., None] * inv_freq
    cos = jnp.cos(ang)
    sin = jnp.sin(ang)
    cos2 = jnp.concatenate([cos, cos, cos, cos], axis=-1)
    sin2 = jnp.concatenate([-sin, sin, -sin, sin], axis=-1)
    return cos2, sin2


def _lru_kernel(x_ref, gate_ref, cw_ref, cb_ref, w_ref, bias_ref, lamc_ref, o_ref, xpad_ref, h_ref):
    c = pl.program_id(1)

    @pl.when(c == 0)
    def _():
        h_ref[...] = jnp.zeros(h_ref.shape, F32)
        xpad_ref[0:CONV_PAD, :] = jnp.zeros((CONV_PAD, xpad_ref.shape[1]), F32)

    rows = x_ref.shape[1]
    xc = _causal_conv(xpad_ref, x_ref[0].astype(F32), cw_ref[...], cb_ref[...])
    ri = _sigmoid(_dot(xc.astype(BF16), w_ref[...]) + bias_ref[...])
    log_a = lamc_ref[...] * ri[:, :LRU_WIDTH]
    a_all = jnp.exp(log_a)
    mult = jnp.sqrt(jnp.maximum(1.0 - jnp.exp(2.0 * log_a), 0.0))
    b_all = mult * ri[:, LRU_WIDTH:] * xc
    gate = _gelu(gate_ref[0].astype(F32))
    n_steps = int(math.log2(LRU_SCAN_ROWS))
    row0 = lax.broadcasted_iota(jnp.int32, (LRU_SCAN_ROWS, LANES), 0) == 0
    for j in range(LRU_WIDTH // LANES):
        sl = slice(j * LANES, (j + 1) * LANES)
        carry = h_ref[0:1, sl]
        for r0 in range(0, rows, LRU_SCAN_ROWS):
            a = a_all[r0:r0 + LRU_SCAN_ROWS, sl]
            b = b_all[r0:r0 + LRU_SCAN_ROWS, sl]
            b = b + jnp.where(row0, a * carry, 0.0)
            for s in range(n_steps):
                k = 1 << s
                b = b + a * _shift_rows(b, k)
                if s + 1 < n_steps:
                    a = a * _shift_rows(a, k, 1.0)
            carry = b[LRU_SCAN_ROWS - 1:LRU_SCAN_ROWS, :]
            o_ref[0, r0:r0 + LRU_SCAN_ROWS, sl] = (
                b * gate[r0:r0 + LRU_SCAN_ROWS, sl]).astype(o_ref.dtype)
        h_ref[0:1, sl] = carry


def _block_diag(w):
    nb, n, _ = w.shape
    eye = jnp.eye(nb, dtype=w.dtype)
    return jnp.einsum('bij,bc->bicj', w, eye).reshape(nb * n, nb * n)


def _lru(x, gate, conv_w, conv_b, wa, ba, wx, bx, lam):
    bsz, seq, _ = x.shape
    rows = min(LRU_ROWS, seq)
    w = jnp.concatenate([_block_diag(wa), _block_diag(wx)], axis=1).astype(BF16)
    bias = jnp.concatenate([ba, bx]).astype(F32).reshape(1, -1)
    lamc = (-LRU_C * jax.nn.softplus(-lam.astype(F32))).reshape(1, -1)
    blk = lambda wd: pl.BlockSpec((1, rows, wd), lambda b, c: (b, c, 0))
    return pl.pallas_call(
        _lru_kernel,
        grid=(bsz, seq // rows),
        in_specs=[blk(LRU_WIDTH), blk(LRU_WIDTH),
                  _const_spec((LRU_CONV, LRU_WIDTH)), _const_spec((1, LRU_WIDTH)),
                  _const_spec((LRU_WIDTH, 2 * LRU_WIDTH)), _const_spec((1, 2 * LRU_WIDTH)),
                  _const_spec((1, LRU_WIDTH))],
        out_specs=blk(LRU_WIDTH),
        out_shape=jax.ShapeDtypeStruct((bsz, seq, LRU_WIDTH), BF16),
        scratch_shapes=[pltpu.VMEM((rows + CONV_PAD, LRU_WIDTH), F32),
                        pltpu.VMEM((SUBLANES, LRU_WIDTH), F32)],
        compiler_params=_params("parallel", "arbitrary"),
        name="rglru",
    )(x, gate, conv_w.astype(F32), conv_b.astype(F32).reshape(1, -1), w, bias, lamc)


def _merge_kernel(h_ref, nw_ref, u_ref, ys5_ref, b1_ref, b2_ref, b3_ref, wg_ref, bg_ref, d_ref,
                  wglu_ref, wb_ref, wo_ref, o_ref, yscr_ref):
    folded = yscr_ref.shape[1] // S5_SUB
    for q in range(S5_Q):
        z = ys5_ref[q].astype(F32) + d_ref[q] * u_ref[q].astype(F32)
        for t in range(S5_SUB):
            yscr_ref[q, pl.ds(t, folded, stride=S5_SUB), :] = z[:, t * LANES:(t + 1) * LANES]
    y = _gelu(jnp.concatenate([yscr_ref[q] for q in range(S5_Q)], axis=1))
    b0 = (y * _sigmoid(_dot(y.astype(BF16), wglu_ref[...]))).astype(BF16)
    branches = (b0, b1_ref[...], b2_ref[...], b3_ref[...])
    h = h_ref[...]
    xb = _rms(h, nw_ref[...]).astype(BF16)
    cols = [slice(i * D_MODEL, (i + 1) * D_MODEL) for i in range(N_BRANCH)]
    ths = [jnp.tanh(_dot(xb, wg_ref[:, sl]) + bg_ref[:, sl]) for sl in cols]
    ps = [_dot(br, wb_ref[i]) for i, br in enumerate(branches)]
    merged = None
    for th, p in zip(ths, ps):
        term = p + th * p
        merged = term if merged is None else merged + term
    o_ref[...] = h + _dot(merged.astype(BF16), wo_ref[...])


def _merge(h, norm_w, u, ys5, b1, b2, b3, w_gate_half, b_gate, s5_d, w_glu, w_branch, w_out):
    t = h.shape[0]
    tm = min(ROW_TILE, t)
    row = lambda w: pl.BlockSpec((tm, w), lambda i: (i, 0))
    s5_blk = pl.BlockSpec((S5_Q, tm // S5_SUB, S5_ROW), lambda i: (0, i, 0))
    d_fold = jnp.tile(s5_d.astype(F32).reshape(S5_Q, 1, LANES), (1, 1, S5_SUB))
    return pl.pallas_call(
        _merge_kernel,
        grid=(t // tm,),
        in_specs=[row(D_MODEL), _const_spec((1, D_MODEL)), s5_blk, s5_blk] + [row(512)] * 3 + [
                  _const_spec((D_MODEL, N_BRANCH * D_MODEL)),
                  _const_spec((1, N_BRANCH * D_MODEL)), _const_spec((S5_Q, 1, S5_ROW)),
                  _const_spec((S5_WIDTH, S5_WIDTH)),
                  _const_spec((N_BRANCH, 512, D_MODEL)), _const_spec((D_MODEL, D_MODEL))],
        out_specs=row(D_MODEL),
        out_shape=jax.ShapeDtypeStruct((t, D_MODEL), F32),
        scratch_shapes=[pltpu.VMEM((S5_Q, tm, LANES), F32)],
        compiler_params=_params("parallel"),
        name="merge",
    )(h, norm_w.astype(F32).reshape(1, -1), u, ys5, b1, b2, b3, w_gate_half,
      0.5 * b_gate.astype(F32).reshape(1, -1), d_fold, w_glu.astype(BF16), w_branch.astype(BF16),
      (0.5 * w_out).astype(BF16))


def _kv_kernel(mem_ref, nw_ref, w_ref, k_ref, v_ref):
    mb = _rms(mem_ref[0], nw_ref[...]).astype(BF16)
    for c in range(0, D_MODEL, DOT_COLS):
        k_ref[0, :, c:c + DOT_COLS] = _dot(mb, w_ref[:, c:c + DOT_COLS]).astype(BF16)
        v_ref[0, :, c:c + DOT_COLS] = _dot(
            mb, w_ref[:, D_MODEL + c:D_MODEL + c + DOT_COLS]).astype(BF16)


def _kv(mem, norm_w, wk, wv):
    bsz, n_mem, _ = mem.shape
    w = jnp.concatenate([wk, wv], axis=1).astype(BF16)
    blk = pl.BlockSpec((1, n_mem, D_MODEL), lambda b: (b, 0, 0))
    return pl.pallas_call(
        _kv_kernel,
        grid=(bsz,),
        in_specs=[blk, _const_spec((1, D_MODEL)), _const_spec((D_MODEL, 2 * D_MODEL))],
        out_specs=[blk, blk],
        out_shape=[jax.ShapeDtypeStruct((bsz, n_mem, D_MODEL), BF16)] * 2,
        compiler_params=_params("parallel"),
        name="xattn_kv",
    )(mem, norm_w.astype(F32).reshape(1, -1), w)


def _xattn_kernel(h_ref, nw_ref, wq_ref, k_ref, v_ref, wo_ref, o_ref, att_ref):
    h = h_ref[0]
    xb = _rms(h, nw_ref[...]).astype(BF16)
    heads = [slice(hd * XA_HEAD_DIM, (hd + 1) * XA_HEAD_DIM) for hd in range(XA_HEADS)]
    qs = [_dot(xb, wq_ref[:, sl]).astype(BF16) for sl in heads]
    ss = [_dot_nt(q, k_ref[0, :, sl]) for q, sl in zip(qs, heads)]
    ps = [jnp.exp(s - jnp.max(s, axis=-1, keepdims=True)) for s in ss]
    for p, sl in zip(ps, heads):
        inv = 1.0 / jnp.sum(p, axis=-1, keepdims=True)
        att_ref[:, sl] = (_dot(p.astype(BF16), v_ref[0, :, sl]) * inv).astype(BF16)
    o_ref[0] = h + _dot(att_ref[...], wo_ref[...])


def _xattn(h3, norm_w, wq, k, v, wo):
    bsz, seq, _ = h3.shape
    tq = min(ROW_TILE, seq)
    n_mem = k.shape[1]
    row = pl.BlockSpec((1, tq, D_MODEL), lambda b, i: (b, i, 0))
    kv = pl.BlockSpec((1, n_mem, D_MODEL), lambda b, i: (b, 0, 0))
    return pl.pallas_call(
        _xattn_kernel,
        grid=(bsz, seq // tq),
        in_specs=[row, _const_spec((1, D_MODEL)), _const_spec((D_MODEL, D_MODEL)), kv, kv,
                  _const_spec((D_MODEL, D_MODEL))],
        out_specs=row,
        out_shape=jax.ShapeDtypeStruct((bsz, seq, D_MODEL), F32),
        scratch_shapes=[pltpu.VMEM((tq, D_MODEL), BF16)],
        compiler_params=_params("parallel", "parallel"),
        name="xattn",
    )(h3, norm_w.astype(F32).reshape(1, -1), (wq * XA_HEAD_DIM ** -0.5).astype(BF16), k, v,
      wo.astype(BF16))


FF_COLS = 256


def _swiglu_acc(xb, w1_ref, w3_ref, w2_ref, width, lead=()):
    acc = None
    for c in range(0, width, FF_COLS):
        a = _dot(xb, w1_ref[lead + (slice(None), slice(c, c + FF_COLS))])
        b = _dot(xb, w3_ref[lead + (slice(None), slice(c, c + FF_COLS))])
        g = (_silu(a) * b).astype(BF16)
        term = _dot(g, w2_ref[lead + (slice(c, c + FF_COLS), slice(None))])
        acc = term if acc is None else acc + term
    return acc


def _ffn_kernel(h_ref, nw_ref, w1_ref, w3_ref, w2_ref, o_ref):
    h = h_ref[...]
    xb = _rms(h, nw_ref[...]).astype(BF16)
    o_ref[...] = h + _swiglu_acc(xb, w1_ref, w3_ref, w2_ref, D_FF)


def _ffn(h, norm_w, w1, w3, w2):
    t = h.shape[0]
    tm = min(ROW_TILE, t)
    row = pl.BlockSpec((tm, D_MODEL), lambda i: (i, 0))
    return pl.pallas_call(
        _ffn_kernel,
        grid=(t // tm,),
        in_specs=[row, _const_spec((1, D_MODEL)), _const_spec((D_MODEL, D_FF)),
                  _const_spec((D_MODEL, D_FF)), _const_spec((D_FF, D_MODEL))],
        out_specs=row,
        out_shape=jax.ShapeDtypeStruct((t, D_MODEL), F32),
        compiler_params=_params("parallel"),
        name="ffn",
    )(h, norm_w.astype(F32).reshape(1, -1), w1.astype(BF16), w3.astype(BF16), w2.astype(BF16))


MOE_TILE = 512
TOP_K = 2
DISPATCH_ROWS = 1024
COMBINE_ROWS = 512
DMA_UNROLL = 8


def _sorted_tiles(t):
    return pl.cdiv(TOP_K * t, MOE_TILE) + N_EXPERTS


def _router_kernel(h_ref, nw_ref, wr_ref, xn_ref, member_ref, wsel_ref, esel_ref):
    xn = _rms(h_ref[...], nw_ref[...])
    xn_ref[...] = xn
    x_hi = xn.astype(BF16)
    x_lo = (xn - x_hi.astype(F32)).astype(BF16)
    w = wr_ref[...]
    w_hi = w.astype(BF16)
    w_lo = (w - w_hi.astype(F32)).astype(BF16)
    logits = _dot(x_hi, w_hi) + (_dot(x_lo, w_hi) + _dot(x_hi, w_lo))
    lane = lax.broadcasted_iota(jnp.int32, logits.shape, 1)
    logits = jnp.where(lane < N_EXPERTS, logits, -jnp.inf)
    m1 = jnp.max(logits, axis=-1, keepdims=True)
    i1 = jnp.min(jnp.where(logits == m1, lane, LANES), axis=-1, keepdims=True)
    rest = jnp.where(lane == i1, -jnp.inf, logits)
    m2 = jnp.max(rest, axis=-1, keepdims=True)
    i2 = jnp.min(jnp.where(rest == m2, lane, LANES), axis=-1, keepdims=True)
    e2 = jnp.exp(m2 - m1)
    w1 = 1.0 / (1.0 + e2)
    w2 = e2 / (1.0 + e2)
    member_ref[...] = jnp.where(lane == i1, 1.0, jnp.where(lane == i2, 1.0, 0.0)).astype(BF16)
    wsel_ref[...] = jnp.where(lane == 0, w1, jnp.where(lane == 1, w2, 0.0))
    esel_ref[...] = jnp.where(lane == 0, i1, jnp.where(lane == 1, i2, 0))


def _router(h, norm_w, w_router):
    t = h.shape[0]
    tm = min(ROW_TILE, t)
    wr = jnp.pad(w_router.astype(F32), ((0, 0), (0, LANES - N_EXPERTS)))
    row = lambda w: pl.BlockSpec((tm, w), lambda i: (i, 0))
    return pl.pallas_call(
        _router_kernel,
        grid=(t // tm,),
        in_specs=[row(D_MODEL), _const_spec((1, D_MODEL)), _const_spec((D_MODEL, LANES))],
        out_specs=[row(D_MODEL), row(LANES), row(LANES), row(LANES)],
        out_shape=[jax.ShapeDtypeStruct((t, D_MODEL), F32), jax.ShapeDtypeStruct((t, LANES), BF16),
                   jax.ShapeDtypeStruct((t, LANES), F32), jax.ShapeDtypeStruct((t, LANES), jnp.int32)],
        compiler_params=_params("parallel"),
        name="moe_router",
    )(h, norm_w.astype(F32).reshape(1, -1), wr)


def _positions_kernel(member_ref, esel_ref, pos_ref, meta_ref, cnt_ref, carry_ref, off_ref):
    phase = pl.program_id(0)
    i = pl.program_id(1)
    m = member_ref[...]
    tp = m.shape[0]
    col_sum = _dot(jnp.ones((SUBLANES, tp), BF16), m)

    @pl.when((phase == 0) & (i == 0))
    def _():
        cnt_ref[...] = jnp.zeros(cnt_ref.shape, F32)

    @pl.when(phase == 0)
    def _():
        cnt_ref[...] += col_sum

    @pl.when((phase == 1) & (i == 0))
    def _():
        tiles = jnp.floor((cnt_ref[...] + (MOE_TILE - 1.0)) * (1.0 / MOE_TILE))
        r = lax.broadcasted_iota(jnp.int32, (LANES, LANES), 0)
        c = lax.broadcasted_iota(jnp.int32, (LANES, LANES), 1)
        before = jnp.where(r < c, 1.0, 0.0).astype(BF16)
        first_tile = _dot(tiles.astype(BF16), before)
        off_ref[...] = first_tile * MOE_TILE
        carry_ref[...] = jnp.zeros(carry_ref.shape, F32)
        row = lax.broadcasted_iota(jnp.int32, (SUBLANES, LANES), 0)
        meta_ref[...] = jnp.where(row == 0, first_tile,
                                  jnp.where(row == 1, tiles, cnt_ref[...])).astype(jnp.int32)

    @pl.when(phase == 1)
    def _():
        r = lax.broadcasted_iota(jnp.int32, (tp, tp), 0)
        c = lax.broadcasted_iota(jnp.int32, (tp, tp), 1)
        earlier = jnp.where(r > c, 1.0, 0.0).astype(BF16)
        posm = off_ref[0:1, :] + carry_ref[0:1, :] + _dot(earlier, m)
        lane = lax.broadcasted_iota(jnp.int32, posm.shape, 1)
        e = esel_ref[...]
        p0 = jnp.sum(jnp.where(lane == e[:, 0:1], posm, 0.0), axis=-1, keepdims=True)
        p1 = jnp.sum(jnp.where(lane == e[:, 1:2], posm, 0.0), axis=-1, keepdims=True)
        pos_ref[...] = jnp.where(lane == 0, p0, jnp.where(lane == 1, p1, 0.0)).astype(jnp.int32)
        carry_ref[...] += col_sum


def _positions(member, esel):
    t = member.shape[0]
    tp = min(ROW_TILE, t)
    return pl.pallas_call(
        _positions_kernel,
        grid=(2, t // tp),
        in_specs=[pl.BlockSpec((tp, LANES), lambda p, i: (i, 0)),
                  pl.BlockSpec((tp, LANES), lambda p, i: (i * p, 0))],
        out_specs=[pl.BlockSpec((tp, LANES), lambda p, i: (i * p, 0)),
                   pl.BlockSpec((SUBLANES, LANES), lambda p, i: (0, 0))],
        out_shape=[jax.ShapeDtypeStruct((t, LANES), jnp.int32),
                   jax.ShapeDtypeStruct((SUBLANES, LANES), jnp.int32)],
        scratch_shapes=[pltpu.VMEM((SUBLANES, LANES), F32)] * 3,
        compiler_params=_params("arbitrary", "arbitrary"),
        name="moe_positions",
    )(member, esel)


def _row_copy(src_ref, src_row, dst_ref, dst_row, sem):
    return pltpu.make_async_copy(src_ref.at[pl.ds(src_row, 1)], dst_ref.at[pl.ds(dst_row, 1)], sem)


def _dispatch_kernel(pad_lo_ref, pad_hi_ref, pos_ref, xn_ref, xs_ref, zero_ref, sem):
    rows = xn_ref.shape[0]

    @pl.when(pl.program_id(0) == 0)
    def _():
        zero_ref[...] = jnp.zeros(zero_ref.shape, F32)
        for e in range(N_EXPERTS):
            lo, hi = pad_lo_ref[e], pad_hi_ref[e]

            def fill(r, carry):
                _row_copy(zero_ref, 0, xs_ref, r, sem).start()
                return carry

            def filled(r, carry):
                _row_copy(zero_ref, 0, xs_ref, r, sem).wait()
                return carry

            lax.fori_loop(lo, hi, fill, 0)
            lax.fori_loop(lo, hi, filled, 0)

        def tail_copy(j):
            return pltpu.make_async_copy(
                zero_ref, xs_ref.at[pl.ds(pl.multiple_of(j * SUBLANES, SUBLANES), SUBLANES)], sem)

        def tail_fill(j, carry):
            tail_copy(j).start()
            return carry

        def tail_filled(j, carry):
            tail_copy(j).wait()
            return carry

        first = lax.shift_right_logical(pad_hi_ref[N_EXPERTS - 1], int(math.log2(SUBLANES)))
        lax.fori_loop(first, xs_ref.shape[0] // SUBLANES, tail_fill, 0)
        lax.fori_loop(first, xs_ref.shape[0] // SUBLANES, tail_filled, 0)

    def copies(r):
        return [_row_copy(xn_ref, r, xs_ref, pos_ref[0, k, r], sem) for k in range(TOP_K)]

    def issue(r, carry):
        for k, cp in enumerate(copies(r)):
            cp.start(priority=k)
        return carry

    def drain(r, carry):
        for cp in copies(r):
            cp.wait()
        return carry

    lax.fori_loop(0, rows, issue, 0, unroll=DMA_UNROLL)
    lax.fori_loop(0, rows, drain, 0, unroll=DMA_UNROLL)


def _slot_major(pos, rows):
    t = pos.shape[0]
    return jnp.transpose(pos[:, :TOP_K].reshape(t // rows, rows, TOP_K), (0, 2, 1))


def _dispatch(xn, pos, pad_lo, pad_hi, n_sorted):
    t = xn.shape[0]
    rows = min(DISPATCH_ROWS, t)
    return pl.pallas_call(
        _dispatch_kernel,
        grid_spec=pltpu.PrefetchScalarGridSpec(
            num_scalar_prefetch=2,
            grid=(t // rows,),
            in_specs=[pl.BlockSpec((1, TOP_K, rows), lambda i, lo, hi: (i, 0, 0),
                                   memory_space=pltpu.SMEM),
                      pl.BlockSpec((rows, D_MODEL), lambda i, lo, hi: (i, 0))],
            out_specs=pl.BlockSpec(memory_space=pl.ANY),
            scratch_shapes=[pltpu.VMEM((SUBLANES, D_MODEL), F32), pltpu.SemaphoreType.DMA(())]),
        out_shape=jax.ShapeDtypeStruct((n_sorted, D_MODEL), F32),
        compiler_params=_params("arbitrary"),
        name="moe_dispatch",
    )(pad_lo, pad_hi, _slot_major(pos, rows), xn)


MOE_FF_SPLIT = 2
MOE_FF_BLOCK = D_FF_EXPERT // MOE_FF_SPLIT


def _experts_kernel(tile_expert_ref, n_used_ref, xs_ref, w1_ref, w3_ref, w2_ref, y_ref):
    del tile_expert_ref
    used = pl.program_id(0) < n_used_ref[0]
    f = pl.program_id(1)

    def part():
        xb = xs_ref[...].astype(BF16)
        return _swiglu_acc(xb, w1_ref, w3_ref, w2_ref, MOE_FF_BLOCK, lead=(0,))

    @pl.when(used & (f == 0))
    def _():
        y_ref[...] = part()

    @pl.when(used & (f > 0))
    def _():
        y_ref[...] += part()

    @pl.when(jnp.logical_not(used) & (f == 0))
    def _():
        y_ref[...] = jnp.zeros(y_ref.shape, F32)


def _experts(xs, tile_expert, n_used, w1, w3, w2):
    n_tiles = xs.shape[0] // MOE_TILE
    row = pl.BlockSpec((MOE_TILE, D_MODEL), lambda i, f, te, nu: (i, 0))
    row_in = pl.BlockSpec((MOE_TILE, D_MODEL), lambda i, f, te, nu: (jnp.minimum(i, nu[0] - 1), 0))
    ff = lambda i, f, nu: jnp.where(i < nu[0], f, MOE_FF_SPLIT - 1)
    w_up = pl.BlockSpec((1, D_MODEL, MOE_FF_BLOCK), lambda i, f, te, nu: (te[i], 0, ff(i, f, nu)))
    w_down = pl.BlockSpec((1, MOE_FF_BLOCK, D_MODEL), lambda i, f, te, nu: (te[i], ff(i, f, nu), 0))
    return pl.pallas_call(
        _experts_kernel,
        grid_spec=pltpu.PrefetchScalarGridSpec(
            num_scalar_prefetch=2,
            grid=(n_tiles, MOE_FF_SPLIT),
            in_specs=[row_in, w_up, w_up, w_down],
            out_specs=row),
        out_shape=jax.ShapeDtypeStruct(xs.shape, F32),
        compiler_params=_params("arbitrary", "arbitrary"),
        name="moe_experts",
    )(tile_expert, n_used, xs, w1.astype(BF16), w3.astype(BF16), w2.astype(BF16))


def _combine_kernel(pos_ref, next_pos_ref, h_ref, wsel_ref, fw_ref, y_ref, o_ref, ybuf_ref, sems,
                    *, final_norm):
    rows = h_ref.shape[0]
    i = pl.program_id(0)
    last = pl.num_programs(0) - 1

    def copies(p_ref, buf, r):
        return [_row_copy(y_ref, p_ref[0, k, r], ybuf_ref.at[buf, k], r, sems.at[buf])
                for k in range(TOP_K)]

    def start(p_ref, buf):
        def issue(r, carry):
            for k, cp in enumerate(copies(p_ref, buf, r)):
                cp.start(priority=k)
            return carry
        lax.fori_loop(0, rows, issue, 0, unroll=DMA_UNROLL)

    def wait(p_ref, buf):
        def drain(r, carry):
            for cp in copies(p_ref, buf, r):
                cp.wait()
            return carry
        lax.fori_loop(0, rows, drain, 0, unroll=DMA_UNROLL)

    cur = lax.rem(i, 2)

    @pl.when(i == 0)
    def _():
        start(pos_ref, 0)

    @pl.when(i < last)
    def _():
        start(next_pos_ref, 1 - cur)

    wait(pos_ref, cur)
    w = wsel_ref[...]
    out = h_ref[...] + w[:, 0:1] * ybuf_ref[cur, 0] + w[:, 1:2] * ybuf_ref[cur, 1]
    if final_norm:
        out = _rms(out, fw_ref[...])
    o_ref[...] = out


def _combine(h, wsel, pos, y_sorted, final_w, final_norm):
    t = h.shape[0]
    rows = min(COMBINE_ROWS, t)
    steps = t // rows
    slots = _slot_major(pos, rows)
    return pl.pallas_call(
        functools.partial(_combine_kernel, final_norm=final_norm),
        grid=(steps,),
        in_specs=[pl.BlockSpec((1, TOP_K, rows), lambda i: (i, 0, 0), memory_space=pltpu.SMEM),
                  pl.BlockSpec((1, TOP_K, rows), lambda i: (jnp.minimum(i + 1, steps - 1), 0, 0),
                               memory_space=pltpu.SMEM),
                  pl.BlockSpec((rows, D_MODEL), lambda i: (i, 0)),
                  pl.BlockSpec((rows, LANES), lambda i: (i, 0)),
                  _const_spec((1, D_MODEL)),
                  pl.BlockSpec(memory_space=pl.ANY)],
        out_specs=pl.BlockSpec((rows, D_MODEL), lambda i: (i, 0)),
        out_shape=jax.ShapeDtypeStruct((t, D_MODEL), F32),
        scratch_shapes=[pltpu.VMEM((2, TOP_K, rows, D_MODEL), F32), pltpu.SemaphoreType.DMA((2,))],
        compiler_params=_params("arbitrary"),
        name="moe_combine",
    )(slots, slots, h, wsel, final_w.astype(F32).reshape(1, -1), y_sorted)


def _moe(h, norm_w, w_router, w1, w3, w2, final_w, final_norm):
    t = h.shape[0]
    n_tiles = _sorted_tiles(t)
    xn, member, wsel, esel = _router(h, norm_w, w_router)
    pos, meta = _positions(member, esel)
    first_tile = meta[0, :N_EXPERTS]
    last_tile = first_tile + meta[1, :N_EXPERTS]
    tile = jnp.arange(n_tiles, dtype=jnp.int32)
    tile_expert = jnp.minimum(jnp.sum(tile[:, None] >= last_tile[None, :], axis=1), N_EXPERTS - 1)
    pad_lo = first_tile * MOE_TILE + meta[2, :N_EXPERTS]
    xs = _dispatch(xn, pos, pad_lo, last_tile * MOE_TILE, n_tiles * MOE_TILE)
    ys = _experts(xs, tile_expert.astype(jnp.int32), last_tile[N_EXPERTS - 1:], w1, w3, w2)
    return _combine(h, wsel, pos, ys, final_w, final_norm)


def _final_norm_kernel(h_ref, w_ref, o_ref):
    o_ref[...] = _rms(h_ref[...], w_ref[...])


def _final_norm(h, w):
    t = h.shape[0]
    tm = min(ROW_TILE, t)
    row = pl.BlockSpec((tm, D_MODEL), lambda i: (i, 0))
    return pl.pallas_call(
        _final_norm_kernel, grid=(t // tm,), in_specs=[row, _const_spec((1, D_MODEL))],
        out_specs=row, out_shape=jax.ShapeDtypeStruct((t, D_MODEL), F32),
        compiler_params=_params("parallel"), name="final_norm",
    )(h, w.astype(F32).reshape(1, -1))


def _mixing_block(h, bsz, seq, cos2, sin2, norm_w, w_in, b_gate,
                  s5_lam_re, s5_lam_im, s5_b_re, s5_b_im, s5_c_re, s5_c_im, s5_d, s5_log_dt, s5_w_glu,
                  ssd_conv_w, ssd_conv_b, ssd_dt_bias, ssd_a_log, ssd_d, ssd_norm,
                  ret_norm,
                  lru_conv_w, lru_conv_b, lru_wa, lru_ba, lru_wx, lru_bx, lru_lam,
                  w_branch, w_out):
    w_cat, w_gate_half = _pack_w_in(w_in)
    (u_s5, z_ssd, xbc_ssd, dt_ssd, q_ret, k_ret, v_ret, g_ret, x_lru,
     gate_lru) = _inproj(h, norm_w, w_cat)
    seq3 = lambda a: a.reshape(bsz, seq, a.shape[-1])
    tables = _s5_tables(s5_lam_re, s5_lam_im, s5_b_re, s5_b_im, s5_c_re, s5_c_im, s5_log_dt,
                        seq // S5_SUB)
    y_s5 = _s5_scan(u_s5, tables, bsz, seq)
    y_ssd = _ssd(seq3(z_ssd), seq3(xbc_ssd), seq3(dt_ssd), ssd_conv_w, ssd_conv_b, ssd_dt_bias,
                 ssd_a_log, ssd_d, ssd_norm)
    y_ret = _retention(seq3(q_ret), seq3(k_ret), seq3(v_ret), seq3(g_ret), cos2, sin2, ret_norm)
    y_lru = _lru(seq3(x_lru), seq3(gate_lru), lru_conv_w, lru_conv_b, lru_wa, lru_ba, lru_wx,
                 lru_bx, lru_lam)
    flat = lambda a: a.reshape(bsz * seq, a.shape[-1])
    return _merge(h, norm_w, u_s5, y_s5, flat(y_ssd), flat(y_ret), flat(y_lru), w_gate_half,
                  b_gate, s5_d, s5_w_glu, w_branch, w_out)


def kernel(x, mem, positions, norm_mix, w_in, b_gate, s5_lam_re, s5_lam_im, s5_b_re, s5_b_im, s5_c_re, s5_c_im, s5_d, s5_log_dt, s5_w_glu, ssd_conv_w, ssd_conv_b, ssd_dt_bias, ssd_a_log, ssd_d, ssd_norm, ret_norm, lru_conv_w, lru_conv_b, lru_wa, lru_ba, lru_wx, lru_bx, lru_lam, w_branch, w_out, norm_xa, norm_mem, xa_wq, xa_wk, xa_wv, xa_wo, norm_ffn, ffn_w1, ffn_w3, ffn_w2, moe_router, moe_w1, moe_w3, moe_w2, norm_final):
    bsz, seq, _ = x.shape
    depth = norm_mix.shape[0]
    cos2, sin2 = _rope_tables(positions)
    h = x.reshape(bsz * seq, D_MODEL)
    for i in range(depth):
        h = _mixing_block(h, bsz, seq, cos2, sin2, norm_mix[i], w_in[i], b_gate[i],
                          s5_lam_re[i], s5_lam_im[i], s5_b_re[i], s5_b_im[i], s5_c_re[i], s5_c_im[i],
                          s5_d[i], s5_log_dt[i], s5_w_glu[i],
                          ssd_conv_w[i], ssd_conv_b[i], ssd_dt_bias[i], ssd_a_log[i], ssd_d[i],
                          ssd_norm[i], ret_norm[i],
                          lru_conv_w[i], lru_conv_b[i], lru_wa[i], lru_ba[i], lru_wx[i], lru_bx[i],
                          lru_lam[i], w_branch[i], w_out[i])
        k, v = _kv(mem, norm_mem[i], xa_wk[i], xa_wv[i])
        h = _xattn(h.reshape(bsz, seq, D_MODEL), norm_xa[i], xa_wq[i], k, v, xa_wo[i])
        h = h.reshape(bsz * seq, D_MODEL)
        last = i == depth - 1
        if i % 2 == 0:
            h = _ffn(h, norm_ffn[i], ffn_w1[i // 2], ffn_w3[i // 2], ffn_w2[i // 2])
            if last:
                h = _final_norm(h, norm_final)
        else:
            h = _moe(h, norm_ffn[i], moe_router[i // 2], moe_w1[i // 2], moe_w3[i // 2],
                     moe_w2[i // 2], norm_final, last)
    return h.reshape(bsz, seq, D_MODEL)
```

```python
import functools
import math

import numpy as np
import jax
import jax.numpy as jnp
from jax import lax
from jax.experimental import pallas as pl
from jax.experimental.pallas import tpu as pltpu

F32 = jnp.float32
BF16 = jnp.bfloat16

D_MODEL = 1024
N_MEM = 256
EPS = 1e-6
CHUNK = 128
N_BRANCH = 4
S5_WIDTH = 512
S5_GROUP = 16
S5_GROUPS = 32
S5_STATE = 64
S5_SUB = 16
SSD_HEADS = 8
SSD_HEAD_DIM = 64
SSD_WIDTH = 512
SSD_GROUPS = 2
SSD_STATE = 64
SSD_CONV = 4
SSD_CONV_DIM = SSD_WIDTH + 2 * SSD_GROUPS * SSD_STATE
RET_HEADS = 8
RET_HEAD_DIM = 64
RET_WIDTH = 512
ROPE_BASE = 10000.0
LRU_WIDTH = 512
LRU_BLOCKS = 8
LRU_BLOCK = 64
LRU_CONV = 4
LRU_C = 8.0
XA_HEADS = 4
XA_HEAD_DIM = 256
D_FF = 2816
N_EXPERTS = 8
D_FF_EXPERT = 3584
SECTION_WIDTHS = (S5_WIDTH, SSD_WIDTH, SSD_CONV_DIM, SSD_HEADS,
                  RET_WIDTH, RET_WIDTH, RET_WIDTH, RET_WIDTH,
                  LRU_WIDTH, LRU_WIDTH, N_BRANCH * D_MODEL)

LANES = 128
SUBLANES = 8
VMEM_LIMIT = 56 * 1024 * 1024
ROW_TILE = 512
LRU_ROWS = 256
LRU_SCAN_ROWS = 8


def _params(*sem):
    return pltpu.CompilerParams(dimension_semantics=sem, vmem_limit_bytes=VMEM_LIMIT)


def _const_spec(shape):
    nd = len(shape)
    return pl.BlockSpec(shape, lambda *_: (0,) * nd)


def _rms(x, w):
    return x * lax.rsqrt(jnp.mean(x * x, axis=-1, keepdims=True) + EPS) * w


def _sigmoid(x):
    return 0.5 + 0.5 * jnp.tanh(0.5 * x)


def _silu(x):
    return x * _sigmoid(x)


def _gelu(x):
    return jax.nn.gelu(x)


def _softplus(x):
    return jnp.maximum(x, 0.0) + jnp.log(1.0 + jnp.exp(-jnp.abs(x)))


def _dot(a, b):
    return jnp.dot(a, b, preferred_element_type=F32)


def _dot_nt(a, b):
    return lax.dot_general(a, b, (((1,), (1,)), ((), ())), preferred_element_type=F32)


def _split3(x):
    hi = x.astype(BF16)
    r1 = x - hi.astype(F32)
    mid = r1.astype(BF16)
    lo = (r1 - mid.astype(F32)).astype(BF16)
    return hi, mid, lo


def _split_dot(x, m_bf16):
    hi, mid, lo = _split3(x)
    return _dot(hi, m_bf16) + _dot(mid, m_bf16) + _dot(lo, m_bf16)


def _split_dot_left(m_bf16, x):
    hi, mid, lo = _split3(x)
    return _dot(m_bf16, hi) + _dot(m_bf16, mid) + _dot(m_bf16, lo)


def _shift_rows(x, k, fill=0.0):
    rows = x.shape[0]
    if k % SUBLANES == 0:
        return jnp.concatenate([jnp.full((k, x.shape[1]), fill, x.dtype), x[:rows - k]], axis=0)
    rolled = pltpu.roll(x, k, 0)
    row = lax.broadcasted_iota(jnp.int32, x.shape, 0)
    return jnp.where(row >= k, rolled, fill)


IN_WIDTHS = (S5_WIDTH, SSD_WIDTH, SSD_CONV_DIM, LANES,
             RET_WIDTH, RET_WIDTH, RET_WIDTH, RET_WIDTH,
             LRU_WIDTH, LRU_WIDTH)
IN_TOTAL_PADDED = sum(IN_WIDTHS)
DOT_COLS = 512


def _inproj_kernel(h_ref, nw_ref, w_ref, u_ref, *refs):
    out_refs, uscr_ref = refs[:-1], refs[-1]
    xb = _rms(h_ref[...], nw_ref[...]).astype(BF16)
    u = _dot(xb, w_ref[:, :S5_WIDTH])
    folded = u.shape[0] // S5_SUB
    for q in range(S5_Q):
        uscr_ref[q] = u[:, q * LANES:(q + 1) * LANES]
        for t in range(S5_SUB):
            u_ref[q, :, t * LANES:(t + 1) * LANES] = uscr_ref[
                q, pl.ds(t, folded, stride=S5_SUB), :].astype(u_ref.dtype)
    off = S5_WIDTH
    for o_ref, width in zip(out_refs, IN_WIDTHS[1:]):
        for c in range(0, width, DOT_COLS):
            n = min(DOT_COLS, width - c)
            o_ref[:, c:c + n] = _dot(xb, w_ref[:, off + c:off + c + n]).astype(o_ref.dtype)
        off += width


def _inproj(h, norm_w, w_cat):
    t = h.shape[0]
    tm = min(ROW_TILE, t)
    out_shape = ([jax.ShapeDtypeStruct((S5_Q, t // S5_SUB, S5_ROW), BF16)]
                 + [jax.ShapeDtypeStruct((t, w), BF16) for w in IN_WIDTHS[1:]])
    out_specs = ([pl.BlockSpec((S5_Q, tm // S5_SUB, S5_ROW), lambda i: (0, i, 0))]
                 + [pl.BlockSpec((tm, w), lambda i: (i, 0)) for w in IN_WIDTHS[1:]])
    return pl.pallas_call(
        _inproj_kernel,
        grid=(t // tm,),
        in_specs=[pl.BlockSpec((tm, D_MODEL), lambda i: (i, 0)),
                  _const_spec((1, D_MODEL)),
                  _const_spec((D_MODEL, IN_TOTAL_PADDED))],
        out_specs=out_specs,
        out_shape=out_shape,
        scratch_shapes=[pltpu.VMEM((S5_Q, tm, LANES), F32)],
        compiler_params=_params("parallel"),
        name="inproj",
    )(h, norm_w.reshape(1, D_MODEL), w_cat)


def _pack_w_in(w_in):
    pieces = []
    off = 0
    for width in SECTION_WIDTHS[:-1]:
        sec = w_in[:, off:off + width]
        if width == SSD_HEADS:
            sec = jnp.pad(sec, ((0, 0), (0, LANES - width)))
        pieces.append(sec)
        off += width
    w_gate = (0.5 * w_in[:, off:]).astype(BF16)
    return jnp.concatenate(pieces, axis=1).astype(BF16), w_gate


S5_Q = S5_WIDTH // LANES
S5_QG = S5_GROUPS // S5_Q
S5_QS = S5_QG * S5_STATE
S5_ROW = S5_SUB * LANES
S5_TILE = 256
S5_SEQS = 8


def _s5_expand_tables(wc_ref, xc_ref, rin_ref, rout_ref, win_ref, wx_ref):
    group_shift = int(math.log2(S5_GROUP))
    state_shift = int(math.log2(S5_STATE))
    for c in range(0, 2 * S5_QS, DOT_COLS):
        r = lax.broadcasted_iota(jnp.int32, (S5_ROW, DOT_COLS), 0)
        col = lax.broadcasted_iota(jnp.int32, (S5_ROW, DOT_COLS), 1) + c
        own = ((r >> group_shift) & (S5_QG - 1)) == ((col & (S5_QS - 1)) >> state_shift)
        rep = _dot(wc_ref[0], rin_ref[:, c:c + DOT_COLS])
        win_ref[:, c:c + DOT_COLS] = jnp.where(own, rep, 0.0).astype(BF16)
    for c in range(0, S5_ROW, DOT_COLS):
        r = lax.broadcasted_iota(jnp.int32, (2 * S5_QS, DOT_COLS), 0)
        col = lax.broadcasted_iota(jnp.int32, (2 * S5_QS, DOT_COLS), 1) + c
        own = ((r & (S5_QS - 1)) >> state_shift) == ((col >> group_shift) & (S5_QG - 1))
        rep = _dot(xc_ref[0], rout_ref[:, c:c + DOT_COLS])
        wx_ref[:, c:c + DOT_COLS] = jnp.where(own, rep, 0.0).astype(BF16)


def _s5_kernel(u_ref, toe_ref, wc_ref, xc_ref, rin_ref, rout_ref, pw_ref, y_ref,
               win_ref, wx_ref, x_ref, sp_ref, *, rows):
    @pl.when(pl.program_id(1) == 0)
    def _():
        _s5_expand_tables(wc_ref, xc_ref, rin_ref, rout_ref, win_ref, wx_ref)

    n_steps = int(math.log2(rows))
    for c in range(0, 2 * S5_QS, DOT_COLS):
        x_ref[:, c:c + DOT_COLS] = _dot(u_ref[0], win_ref[:, c:c + DOT_COLS])

    def seq(b, carry):
        r0 = pl.multiple_of(b * rows, rows)
        for lc in range(S5_QS // LANES):
            re = slice(lc * LANES, (lc + 1) * LANES)
            im = slice(S5_QS + lc * LANES, S5_QS + (lc + 1) * LANES)
            sr = x_ref[pl.ds(r0, rows), re]
            si = x_ref[pl.ds(r0, rows), im]
            for j in range(n_steps):
                k = 1 << j
                pr = pw_ref[0, 2 * j:2 * j + 1, re]
                pi = pw_ref[0, 2 * j + 1:2 * j + 2, re]
                shr = _shift_rows(sr, k)
                shi = _shift_rows(si, k)
                sr, si = sr + pr * shr - pi * shi, si + pr * shi + pi * shr
            sp_ref[pl.ds(r0, rows), re] = _shift_rows(sr, 1).astype(BF16)
            sp_ref[pl.ds(r0, rows), im] = _shift_rows(si, 1).astype(BF16)
        return carry

    lax.fori_loop(0, u_ref.shape[1] // rows, seq, 0)
    for nn in range(S5_ROW // S5_TILE):
        cols = slice(nn * S5_TILE, (nn + 1) * S5_TILE)
        acc = _dot(sp_ref[...], wx_ref[:, cols])
        for kk in range(nn + 1):
            acc = acc + _dot(u_ref[0, :, kk * S5_TILE:(kk + 1) * S5_TILE], toe_ref[0, nn - kk])
        y_ref[0, :, cols] = acc.astype(y_ref.dtype)


def _s5_tables(lam_re, lam_im, b_re, b_im, c_re, c_im, log_dt, rows):
    lr = lam_re.astype(F32)
    li = lam_im.astype(F32)
    step = jnp.exp(log_dt.astype(F32))[:, None]
    mag = jnp.exp(lr * step)
    ar = mag * jnp.cos(li * step)
    ai = mag * jnp.sin(li * step)
    inv = 1.0 / (lr * lr + li * li)
    cr = ((ar - 1.0) * lr + ai * li) * inv
    ci = (ai * lr - (ar - 1.0) * li) * inv
    bbr = cr[..., None] * b_re - ci[..., None] * b_im
    bbi = cr[..., None] * b_im + ci[..., None] * b_re

    def apow(e):
        e = jnp.asarray(e, F32)[:, None, None]
        m = jnp.exp(lr * step * e)
        return m * jnp.cos(li * step * e), m * jnp.sin(li * step * e)

    sub = S5_SUB
    pr, pi = apow(np.arange(sub + 1))
    m_r = pr[:sub, :, :, None] * bbr - pi[:sub, :, :, None] * bbi
    m_i = pr[:sub, :, :, None] * bbi + pi[:sub, :, :, None] * bbr
    kern = (jnp.einsum('gon,tgni->tgoi', c_re, m_r)
            - jnp.einsum('gon,tgni->tgoi', c_im, m_i))
    eye = jnp.eye(S5_QG, dtype=F32)
    blk = lambda a: a.reshape(a.shape[0], S5_Q, S5_QG, *a.shape[2:])
    kern_bd = jnp.einsum('tqgoi,gh->tqgiho', blk(kern), eye).reshape(sub, S5_Q, LANES, LANES)
    kern_bd = jnp.concatenate([jnp.zeros_like(kern_bd[:1]), kern_bd], axis=0)
    steps_per_tile = S5_TILE // LANES
    d = np.arange(sub // steps_per_tile)[:, None, None]
    tau = steps_per_tile * d + np.arange(steps_per_tile)[None, None, :] - np.arange(steps_per_tile)[None, :, None]
    toe = kern_bd[tau + 1]
    toe = jnp.transpose(toe, (3, 0, 1, 4, 2, 5)).reshape(S5_Q, sub // steps_per_tile, S5_TILE, S5_TILE)
    er, ei = pr[sub - 1 - np.arange(sub)], pi[sub - 1 - np.arange(sub)]
    w_r = er[..., None] * bbr - ei[..., None] * bbi
    w_i = er[..., None] * bbi + ei[..., None] * bbr
    fold_in = lambda w: jnp.transpose(blk(w), (1, 0, 2, 4, 3)).reshape(S5_Q, S5_ROW, S5_STATE)
    wc = jnp.concatenate([fold_in(w_r), fold_in(w_i)], axis=2)
    qr, qi = pr[1:], pi[1:]
    x_r = c_re[None] * qr[:, :, None, :] - c_im[None] * qi[:, :, None, :]
    x_i = -(c_re[None] * qi[:, :, None, :] + c_im[None] * qr[:, :, None, :])
    fold_out = lambda w: jnp.transpose(blk(w), (1, 2, 4, 0, 3)).reshape(S5_Q, S5_QS, sub * S5_GROUP)
    xc = jnp.concatenate([fold_out(x_r), fold_out(x_i)], axis=1)
    n_steps = int(math.log2(rows))
    sr_, si_ = apow(sub * (2.0 ** np.arange(n_steps)))
    pw = jnp.stack([sr_, si_], axis=1).reshape(2 * n_steps, S5_Q, S5_QS)
    pw = jnp.transpose(pw, (1, 0, 2))
    pw = jnp.pad(pw, ((0, 0), (0, 2 * SUBLANES - 2 * n_steps), (0, 0)))
    return toe.astype(BF16), wc.astype(BF16), xc.astype(BF16), pw


def _s5_replicators():
    k = np.arange(2 * S5_STATE)[:, None]
    c = np.arange(2 * S5_QS)[None, :]
    rin = ((k // S5_STATE) == (c // S5_QS)) & ((k % S5_STATE) == (c % S5_STATE))
    r = np.arange(S5_SUB * S5_GROUP)[:, None]
    c = np.arange(S5_ROW)[None, :]
    rout = ((r // S5_GROUP) == (c // LANES)) & ((r % S5_GROUP) == (c % S5_GROUP))
    return jnp.asarray(rin, BF16), jnp.asarray(rout, BF16)


def _s5_scan(u5, tables, bsz, seq):
    toe, wc, xc, pw = tables
    rin, rout = _s5_replicators()
    rows = seq // S5_SUB
    nb = math.gcd(S5_SEQS, bsz)
    row_blk = pl.BlockSpec((1, nb * rows, S5_ROW), lambda q, j: (q, j, 0))
    per_q = lambda a: pl.BlockSpec((1,) + a.shape[1:], lambda q, j: (q,) + (0,) * (a.ndim - 1))
    const = lambda a: pl.BlockSpec(a.shape, lambda q, j: (0,) * a.ndim)
    return pl.pallas_call(
        functools.partial(_s5_kernel, rows=rows),
        grid=(S5_Q, bsz // nb),
        in_specs=[row_blk, per_q(toe), per_q(wc), per_q(xc), const(rin), const(rout), per_q(pw)],
        out_specs=row_blk,
        out_shape=jax.ShapeDtypeStruct(u5.shape, BF16),
        scratch_shapes=[pltpu.VMEM((S5_ROW, 2 * S5_QS), BF16),
                        pltpu.VMEM((2 * S5_QS, S5_ROW), BF16),
                        pltpu.VMEM((nb * rows, 2 * S5_QS), F32),
                        pltpu.VMEM((nb * rows, 2 * S5_QS), BF16)],
        compiler_params=_params("parallel", "arbitrary"),
        name="s5_scan",
    )(u5, toe, wc, xc, rin, rout, pw)


CONV_PAD = SUBLANES
SEQS_PER_STEP = 8


def _causal_conv(xpad_ref, x, w, b):
    rows = x.shape[0]
    xpad_ref[CONV_PAD:CONV_PAD + rows, :] = x
    k = w.shape[0]
    acc = b
    for j in range(k):
        s = CONV_PAD - (k - 1) + j
        acc = acc + w[j:j + 1, :] * xpad_ref[s:s + rows, :]
    xpad_ref[0:CONV_PAD, :] = xpad_ref[rows:rows + CONV_PAD, :]
    return acc


def _ssd_kernel(z_ref, xbc_ref, dt_ref, cw_ref, cb_ref, dtb_ref, a_ref, d_ref, nw_ref, tri_ref,
                o_ref, xpad_ref, state_ref, y_ref):
    @pl.when(pl.program_id(1) == 0)
    def _():
        state_ref[...] = jnp.zeros(state_ref.shape, F32)
        xpad_ref[:, 0:CONV_PAD, :] = jnp.zeros((xpad_ref.shape[0], CONV_PAD, xpad_ref.shape[2]), F32)

    for s in range(z_ref.shape[0]):
        _ssd_chunk(z_ref.at[s], xbc_ref.at[s], dt_ref.at[s], cw_ref, cb_ref, dtb_ref, a_ref, d_ref,
                   nw_ref, tri_ref, o_ref.at[s], xpad_ref.at[s], state_ref.at[s], y_ref.at[s])


def _ssd_chunk(z_ref, xbc_ref, dt_ref, cw_ref, cb_ref, dtb_ref, a_ref, d_ref, nw_ref, tri_ref,
               o_ref, xpad_ref, state_ref, y_ref):
    xbc = _silu(_causal_conv(xpad_ref, xbc_ref[...].astype(F32), cw_ref[...], cb_ref[...]))
    xs_t = xbc[:, :SSD_WIDTH].T
    bs = xbc[:, SSD_WIDTH:SSD_WIDTH + LANES]
    cs_t = xbc[:, SSD_WIDTH + LANES:].T.astype(BF16)
    lane = lax.broadcasted_iota(jnp.int32, bs.shape, 1)
    dt = _softplus(dt_ref[...].astype(F32) + dtb_ref[...])
    acum = _split_dot_left(tri_ref[...], dt * a_ref[...])
    dt_t = dt.T
    acum_t = acum.T
    later = (lax.broadcasted_iota(jnp.int32, (CHUNK, CHUNK), 1)
             >= lax.broadcasted_iota(jnp.int32, (CHUNK, CHUNK), 0))
    heads_per_group = SSD_HEADS // SSD_GROUPS
    b_g, cb_t = [], []
    for g in range(SSD_GROUPS):
        in_group = (lane >= g * SSD_STATE) & (lane < (g + 1) * SSD_STATE)
        b_g.append(jnp.where(in_group, bs, 0.0).astype(BF16))
        cb_t.append(_dot(b_g[g], cs_t))
    x_t, a_row, last, m_t, xdt_t, xw_t = [], [], [], [], [], []
    for hd in range(SSD_HEADS):
        x_t.append(xs_t[hd * SSD_HEAD_DIM:(hd + 1) * SSD_HEAD_DIM, :])
        a_row.append(acum_t[hd:hd + 1, :])
        last.append(acum_t[hd:hd + 1, CHUNK - 1:CHUNK])
        l_t = jnp.exp(jnp.where(later, a_row[hd] - acum[:, hd:hd + 1], -jnp.inf))
        m_t.append((cb_t[hd // heads_per_group] * l_t).astype(BF16))
        xdt = x_t[hd] * dt_t[hd:hd + 1, :]
        xdt_t.append(xdt.astype(BF16))
        xw_t.append((xdt * jnp.exp(last[hd] - a_row[hd])).astype(BF16))
    prev = [state_ref[hd] for hd in range(SSD_HEADS)]
    y_in = [_dot(xdt_t[hd], m_t[hd]) for hd in range(SSD_HEADS)]
    y_off = [_dot(prev[hd].astype(BF16), cs_t) for hd in range(SSD_HEADS)]
    new = [_dot(xw_t[hd], b_g[hd // heads_per_group]) for hd in range(SSD_HEADS)]
    for hd in range(SSD_HEADS):
        state_ref[hd] = prev[hd] * jnp.exp(last[hd]) + new[hd]
        y_ref[hd * SSD_HEAD_DIM:(hd + 1) * SSD_HEAD_DIM, :] = (
            y_in[hd] + y_off[hd] * jnp.exp(a_row[hd]) + d_ref[:, hd:hd + 1] * x_t[hd])
    y = y_ref[...].T * _silu(z_ref[...].astype(F32))
    o_ref[...] = _rms(y, nw_ref[...]).astype(o_ref.dtype)


def _ssd(z, xbc, dt, conv_w, conv_b, dt_bias, a_log, d_skip, norm_w):
    bsz, seq, _ = z.shape
    nc = seq // CHUNK
    ns = math.gcd(SEQS_PER_STEP, bsz)

    def lane_pad(v):
        return jnp.pad(v.astype(F32).reshape(1, -1), ((0, 0), (0, LANES - v.shape[-1])))

    tri = jnp.asarray(np.tril(np.ones((CHUNK, CHUNK), np.float32)), BF16)
    blk = lambda w: pl.BlockSpec((ns, CHUNK, w), lambda b, c: (b, c, 0))
    return pl.pallas_call(
        _ssd_kernel,
        grid=(bsz // ns, nc),
        in_specs=[blk(SSD_WIDTH), blk(SSD_CONV_DIM), blk(LANES),
                  _const_spec((SSD_CONV, SSD_CONV_DIM)), _const_spec((1, SSD_CONV_DIM)),
                  _const_spec((1, LANES)), _const_spec((1, LANES)), _const_spec((1, LANES)),
                  _const_spec((1, SSD_WIDTH)), _const_spec((CHUNK, CHUNK))],
        out_specs=blk(SSD_WIDTH),
        out_shape=jax.ShapeDtypeStruct((bsz, seq, SSD_WIDTH), BF16),
        scratch_shapes=[pltpu.VMEM((ns, CHUNK + CONV_PAD, SSD_CONV_DIM), F32),
                        pltpu.VMEM((ns, SSD_HEADS, SSD_HEAD_DIM, SSD_GROUPS * SSD_STATE), F32),
                        pltpu.VMEM((ns, SSD_WIDTH, CHUNK), F32)],
        compiler_params=_params("parallel", "arbitrary"),
        name="ssd",
    )(z, xbc, dt, conv_w.astype(F32), conv_b.astype(F32).reshape(1, -1),
      lane_pad(dt_bias), lane_pad(-jnp.exp(a_log.astype(F32))), lane_pad(d_skip),
      norm_w.astype(F32).reshape(1, -1), tri)


GN_TILE = 256


def _head_means(x, avg_bf16):
    hi = x.astype(BF16)
    lo = (x - hi.astype(F32)).astype(BF16)
    out = []
    for c in range(0, RET_WIDTH, GN_TILE):
        out.append(_dot(hi[:, c:c + GN_TILE], avg_bf16) + _dot(lo[:, c:c + GN_TILE], avg_bf16))
    return jnp.concatenate(out, axis=1)


def _retention_tables():
    h = np.arange(RET_HEADS, dtype=np.float64)
    log_gamma = np.log1p(-np.exp2(-5.0 - h))
    idx = np.arange(CHUNK, dtype=np.float64)
    diff = idx[:, None] - idx[None, :]
    dmat = np.where(diff >= 0, np.exp(np.maximum(diff, 0.0)[None] * log_gamma[:, None, None]), 0.0)
    k_decay = np.exp((CHUNK - 1.0 - idx)[:, None] * log_gamma)
    q_decay = np.exp((idx + 1.0)[:, None] * log_gamma)
    c_decay = np.exp(CHUNK * log_gamma)[None, :]
    rep = lambda a: np.repeat(a, RET_HEAD_DIM, axis=1).astype(np.float32)
    avg = np.kron(np.eye(GN_TILE // RET_HEAD_DIM),
                  np.full((RET_HEAD_DIM, RET_HEAD_DIM), 1.0 / RET_HEAD_DIM))
    return (dmat.astype(np.float32), rep(k_decay * RET_HEAD_DIM ** -0.5), rep(q_decay),
            rep(c_decay), avg.astype(np.float32))


def _rope(x, cos, sin_signed):
    half = RET_HEAD_DIM // 2
    width = x.shape[1]
    fwd = pltpu.roll(x, half, 1)
    bwd = pltpu.roll(x, width - half, 1)
    lane = lax.broadcasted_iota(jnp.int32, x.shape, 1)
    swapped = jnp.where((lane % RET_HEAD_DIM) < half, bwd, fwd)
    return x * cos + swapped * sin_signed


def _retention_kernel(q_ref, k_ref, v_ref, g_ref, cos_ref, sin_ref, dmat_ref, kdec_ref, qdec_ref,
                      cdec_ref, avg_ref, gn_ref, o_ref, state_ref):
    @pl.when(pl.program_id(1) == 0)
    def _():
        state_ref[...] = jnp.zeros(state_ref.shape, F32)

    for s in range(q_ref.shape[0]):
        _retention_chunk(q_ref.at[s], k_ref.at[s], v_ref.at[s], g_ref.at[s], cos_ref.at[s],
                         sin_ref.at[s], dmat_ref, kdec_ref, qdec_ref, cdec_ref, avg_ref, gn_ref,
                         o_ref.at[s], state_ref.at[s])


def _retention_chunk(q_ref, k_ref, v_ref, g_ref, cos_ref, sin_ref, dmat_ref, kdec_ref, qdec_ref,
                     cdec_ref, avg_ref, gn_ref, o_ref, state_ref):
    reps = RET_WIDTH // LANES
    cos = jnp.tile(cos_ref[...], (1, reps))
    sin = jnp.tile(sin_ref[...], (1, reps))
    q = _rope(q_ref[...].astype(F32), cos, sin)
    k = _rope(k_ref[...].astype(F32), cos, sin)
    v = v_ref[...].astype(F32)
    kd_t = (k * kdec_ref[...]).T.astype(BF16)
    k_tb = (k.T * (RET_HEAD_DIM ** -0.5)).astype(BF16)
    qd = qdec_ref[...]
    cd = cdec_ref[...]
    first = lax.broadcasted_iota(jnp.int32, (CHUNK, LANES), 1) < RET_HEAD_DIM
    r = lax.broadcasted_iota(jnp.int32, (LANES, LANES), 0) < RET_HEAD_DIM
    c = lax.broadcasted_iota(jnp.int32, (LANES, LANES), 1) < RET_HEAD_DIM
    same_head = r == c
    n_pairs = RET_HEADS // 2
    lanes = [slice(p * LANES, (p + 1) * LANES) for p in range(n_pairs)]
    keeps = (first, jnp.logical_not(first))
    prev = [state_ref[p] for p in range(n_pairs)]
    q_h = [jnp.where(keeps[hd % 2], q[:, lanes[hd // 2]], 0.0).astype(BF16) for hd in range(RET_HEADS)]
    v_h = [jnp.where(keeps[hd % 2], v[:, lanes[hd // 2]], 0.0).astype(BF16) for hd in range(RET_HEADS)]
    scores = [(_dot(q_h[hd], k_tb[lanes[hd // 2], :]) * dmat_ref[hd]).astype(BF16)
              for hd in range(RET_HEADS)]
    inner = [_dot(scores[hd], v_h[hd]) for hd in range(RET_HEADS)]
    cross = [_dot(q[:, sl].astype(BF16), prev[p].astype(BF16)) for p, sl in enumerate(lanes)]
    kv = [_dot(kd_t[sl, :], v[:, sl].astype(BF16)) for sl in lanes]
    for p, sl in enumerate(lanes):
        state_ref[p] = prev[p] * cd[:, sl] + jnp.where(same_head, kv[p], 0.0)
    y = jnp.concatenate([cross[p] * qd[:, sl] + inner[2 * p] + inner[2 * p + 1]
                         for p, sl in enumerate(lanes)], axis=1)
    mu = _head_means(y, avg_ref[...])
    yc = y - mu
    var = _head_means(yc * yc, avg_ref[...])
    yn = yc * lax.rsqrt(var + EPS) * gn_ref[...]
    o_ref[...] = (_silu(g_ref[...].astype(F32)) * yn).astype(o_ref.dtype)


def _retention(q, k, v, g, cos2, sin2, gn_w):
    bsz, seq, _ = q.shape
    nc = seq // CHUNK
    ns = math.gcd(SEQS_PER_STEP, bsz)
    dmat, kdec, qdec, cdec, avg = (jnp.asarray(a) for a in _retention_tables())
    avg = avg.astype(BF16)
    blk = lambda w: pl.BlockSpec((ns, CHUNK, w), lambda b, c: (b, c, 0))
    return pl.pallas_call(
        _retention_kernel,
        grid=(bsz // ns, nc),
        in_specs=[blk(RET_WIDTH)] * 4 + [blk(LANES)] * 2 + [
            _const_spec(dmat.shape), _const_spec(kdec.shape), _const_spec(qdec.shape),
            _const_spec(cdec.shape), _const_spec(avg.shape), _const_spec((1, RET_WIDTH))],
        out_specs=blk(RET_WIDTH),
        out_shape=jax.ShapeDtypeStruct((bsz, seq, RET_WIDTH), BF16),
        scratch_shapes=[pltpu.VMEM((ns, RET_HEADS // 2, LANES, LANES), F32)],
        compiler_params=_params("parallel", "arbitrary"),
        name="retention",
    )(q, k, v, g, cos2, sin2, dmat, kdec, qdec, cdec, avg, gn_w.astype(F32).reshape(1, -1))


def _rope_tables(positions):
    half = RET_HEAD_DIM // 2
    inv_freq = ROPE_BASE ** (-jnp.arange(half, dtype=F32) / half)
    ang = positions.astype(F32)[..., None] * inv_freq
    cos = jnp.cos(ang)
    sin = jnp.sin(ang)
    cos2 = jnp.concatenate([cos, cos, cos, cos], axis=-1)
    sin2 = jnp.concatenate([-sin, sin, -sin, sin], axis=-1)
    return cos2, sin2


def _lru_kernel(x_ref, gate_ref, cw_ref, cb_ref, w_ref, bias_ref, lamc_ref, o_ref, xpad_ref, h_ref):
    c = pl.program_id(1)

    @pl.when(c == 0)
    def _():
        h_ref[...] = jnp.zeros(h_ref.shape, F32)
        xpad_ref[0:CONV_PAD, :] = jnp.zeros((CONV_PAD, xpad_ref.shape[1]), F32)

    rows = x_ref.shape[1]
    xc = _causal_conv(xpad_ref, x_ref[0].astype(F32), cw_ref[...], cb_ref[...])
    ri = _sigmoid(_dot(xc.astype(BF16), w_ref[...]) + bias_ref[...])
    log_a = lamc_ref[...] * ri[:, :LRU_WIDTH]
    a_all = jnp.exp(log_a)
    mult = jnp.sqrt(jnp.maximum(1.0 - jnp.exp(2.0 * log_a), 0.0))
    b_all = mult * ri[:, LRU_WIDTH:] * xc
    gate = _gelu(gate_ref[0].astype(F32))
    n_steps = int(math.log2(LRU_SCAN_ROWS))
    row0 = lax.broadcasted_iota(jnp.int32, (LRU_SCAN_ROWS, LANES), 0) == 0
    for j in range(LRU_WIDTH // LANES):
        sl = slice(j * LANES, (j + 1) * LANES)
        carry = h_ref[0:1, sl]
        for r0 in range(0, rows, LRU_SCAN_ROWS):
            a = a_all[r0:r0 + LRU_SCAN_ROWS, sl]
            b = b_all[r0:r0 + LRU_SCAN_ROWS, sl]
            b = b + jnp.where(row0, a * carry, 0.0)
            for s in range(n_steps):
                k = 1 << s
                b = b + a * _shift_rows(b, k)
                if s + 1 < n_steps:
                    a = a * _shift_rows(a, k, 1.0)
            carry = b[LRU_SCAN_ROWS - 1:LRU_SCAN_ROWS, :]
            o_ref[0, r0:r0 + LRU_SCAN_ROWS, sl] = (
                b * gate[r0:r0 + LRU_SCAN_ROWS, sl]).astype(o_ref.dtype)
        h_ref[0:1, sl] = carry


def _block_diag(w):
    nb, n, _ = w.shape
    eye = jnp.eye(nb, dtype=w.dtype)
    return jnp.einsum('bij,bc->bicj', w, eye).reshape(nb * n, nb * n)


def _lru(x, gate, conv_w, conv_b, wa, ba, wx, bx, lam):
    bsz, seq, _ = x.shape
    rows = min(LRU_ROWS, seq)
    w = jnp.concatenate([_block_diag(wa), _block_diag(wx)], axis=1).astype(BF16)
    bias = jnp.concatenate([ba, bx]).astype(F32).reshape(1, -1)
    lamc = (-LRU_C * jax.nn.softplus(-lam.astype(F32))).reshape(1, -1)
    blk = lambda wd: pl.BlockSpec((1, rows, wd), lambda b, c: (b, c, 0))
    return pl.pallas_call(
        _lru_kernel,
        grid=(bsz, seq // rows),
        in_specs=[blk(LRU_WIDTH), blk(LRU_WIDTH),
                  _const_spec((LRU_CONV, LRU_WIDTH)), _const_spec((1, LRU_WIDTH)),
                  _const_spec((LRU_WIDTH, 2 * LRU_WIDTH)), _const_spec((1, 2 * LRU_WIDTH)),
                  _const_spec((1, LRU_WIDTH))],
        out_specs=blk(LRU_WIDTH),
        out_shape=jax.ShapeDtypeStruct((bsz, seq, LRU_WIDTH), BF16),
        scratch_shapes=[pltpu.VMEM((rows + CONV_PAD, LRU_WIDTH), F32),
                        pltpu.VMEM((SUBLANES, LRU_WIDTH), F32)],
        compiler_params=_params("parallel", "arbitrary"),
        name="rglru",
    )(x, gate, conv_w.astype(F32), conv_b.astype(F32).reshape(1, -1), w, bias, lamc)


def _merge_kernel(h_ref, nw_ref, u_ref, ys5_ref, b1_ref, b2_ref, b3_ref, wg_ref, bg_ref, d_ref,
                  wglu_ref, wb_ref, wo_ref, o_ref, yscr_ref):
    folded = yscr_ref.shape[1] // S5_SUB
    for q in range(S5_Q):
        z = ys5_ref[q].astype(F32) + d_ref[q] * u_ref[q].astype(F32)
        for t in range(S5_SUB):
            yscr_ref[q, pl.ds(t, folded, stride=S5_SUB), :] = z[:, t * LANES:(t + 1) * LANES]
    y = _gelu(jnp.concatenate([yscr_ref[q] for q in range(S5_Q)], axis=1))
    b0 = (y * _sigmoid(_dot(y.astype(BF16), wglu_ref[...]))).astype(BF16)
    branches = (b0, b1_ref[...], b2_ref[...], b3_ref[...])
    h = h_ref[...]
    xb = _rms(h, nw_ref[...]).astype(BF16)
    cols = [slice(i * D_MODEL, (i + 1) * D_MODEL) for i in range(N_BRANCH)]
    ths = [jnp.tanh(_dot(xb, wg_ref[:, sl]) + bg_ref[:, sl]) for sl in cols]
    ps = [_dot(br, wb_ref[i]) for i, br in enumerate(branches)]
    merged = None
    for th, p in zip(ths, ps):
        term = p + th * p
        merged = term if merged is None else merged + term
    o_ref[...] = h + _dot(merged.astype(BF16), wo_ref[...])


def _merge(h, norm_w, u, ys5, b1, b2, b3, w_gate_half, b_gate, s5_d, w_glu, w_branch, w_out):
    t = h.shape[0]
    tm = min(ROW_TILE, t)
    row = lambda w: pl.BlockSpec((tm, w), lambda i: (i, 0))
    s5_blk = pl.BlockSpec((S5_Q, tm // S5_SUB, S5_ROW), lambda i: (0, i, 0))
    d_fold = jnp.tile(s5_d.astype(F32).reshape(S5_Q, 1, LANES), (1, 1, S5_SUB))
    return pl.pallas_call(
        _merge_kernel,
        grid=(t // tm,),
        in_specs=[row(D_MODEL), _const_spec((1, D_MODEL)), s5_blk, s5_blk] + [row(512)] * 3 + [
                  _const_spec((D_MODEL, N_BRANCH * D_MODEL)),
                  _const_spec((1, N_BRANCH * D_MODEL)), _const_spec((S5_Q, 1, S5_ROW)),
                  _const_spec((S5_WIDTH, S5_WIDTH)),
                  _const_spec((N_BRANCH, 512, D_MODEL)), _const_spec((D_MODEL, D_MODEL))],
        out_specs=row(D_MODEL),
        out_shape=jax.ShapeDtypeStruct((t, D_MODEL), F32),
        scratch_shapes=[pltpu.VMEM((S5_Q, tm, LANES), F32)],
        compiler_params=_params("parallel"),
        name="merge",
    )(h, norm_w.astype(F32).reshape(1, -1), u, ys5, b1, b2, b3, w_gate_half,
      0.5 * b_gate.astype(F32).reshape(1, -1), d_fold, w_glu.astype(BF16), w_branch.astype(BF16),
      (0.5 * w_out).astype(BF16))


def _kv_kernel(mem_ref, nw_ref, w_ref, k_ref, v_ref):
    mb = _rms(mem_ref[0], nw_ref[...]).astype(BF16)
    for c in range(0, D_MODEL, DOT_COLS):
        k_ref[0, :, c:c + DOT_COLS] = _dot(mb, w_ref[:, c:c + DOT_COLS]).astype(BF16)
        v_ref[0, :, c:c + DOT_COLS] = _dot(
            mb, w_ref[:, D_MODEL + c:D_MODEL + c + DOT_COLS]).astype(BF16)


def _kv(mem, norm_w, wk, wv):
    bsz, n_mem, _ = mem.shape
    w = jnp.concatenate([wk, wv], axis=1).astype(BF16)
    blk = pl.BlockSpec((1, n_mem, D_MODEL), lambda b: (b, 0, 0))
    return pl.pallas_call(
        _kv_kernel,
        grid=(bsz,),
        in_specs=[blk, _const_spec((1, D_MODEL)), _const_spec((D_MODEL, 2 * D_MODEL))],
        out_specs=[blk, blk],
        out_shape=[jax.ShapeDtypeStruct((bsz, n_mem, D_MODEL), BF16)] * 2,
        compiler_params=_params("parallel"),
        name="xattn_kv",
    )(mem, norm_w.astype(F32).reshape(1, -1), w)


def _xattn_kernel(h_ref, nw_ref, wq_ref, k_ref, v_ref, wo_ref, o_ref, att_ref):
    h = h_ref[0]
    xb = _rms(h, nw_ref[...]).astype(BF16)
    heads = [slice(hd * XA_HEAD_DIM, (hd + 1) * XA_HEAD_DIM) for hd in range(XA_HEADS)]
    qs = [_dot(xb, wq_ref[:, sl]).astype(BF16) for sl in heads]
    ss = [_dot_nt(q, k_ref[0, :, sl]) for q, sl in zip(qs, heads)]
    ps = [jnp.exp(s - jnp.max(s, axis=-1, keepdims=True)) for s in ss]
    for p, sl in zip(ps, heads):
        inv = 1.0 / jnp.sum(p, axis=-1, keepdims=True)
        att_ref[:, sl] = (_dot(p.astype(BF16), v_ref[0, :, sl]) * inv).astype(BF16)
    o_ref[0] = h + _dot(att_ref[...], wo_ref[...])


def _xattn(h3, norm_w, wq, k, v, wo):
    bsz, seq, _ = h3.shape
    tq = min(ROW_TILE, seq)
    n_mem = k.shape[1]
    row = pl.BlockSpec((1, tq, D_MODEL), lambda b, i: (b, i, 0))
    kv = pl.BlockSpec((1, n_mem, D_MODEL), lambda b, i: (b, 0, 0))
    return pl.pallas_call(
        _xattn_kernel,
        grid=(bsz, seq // tq),
        in_specs=[row, _const_spec((1, D_MODEL)), _const_spec((D_MODEL, D_MODEL)), kv, kv,
                  _const_spec((D_MODEL, D_MODEL))],
        out_specs=row,
        out_shape=jax.ShapeDtypeStruct((bsz, seq, D_MODEL), F32),
        scratch_shapes=[pltpu.VMEM((tq, D_MODEL), BF16)],
        compiler_params=_params("parallel", "parallel"),
        name="xattn",
    )(h3, norm_w.astype(F32).reshape(1, -1), (wq * XA_HEAD_DIM ** -0.5).astype(BF16), k, v,
      wo.astype(BF16))


FF_COLS = 256


def _swiglu_acc(xb, w1_ref, w3_ref, w2_ref, width, lead=()):
    acc = None
    for c in range(0, width, FF_COLS):
        a = _dot(xb, w1_ref[lead + (slice(None), slice(c, c + FF_COLS))])
        b = _dot(xb, w3_ref[lead + (slice(None), slice(c, c + FF_COLS))])
        g = (_silu(a) * b).astype(BF16)
        term = _dot(g, w2_ref[lead + (slice(c, c + FF_COLS), slice(None))])
        acc = term if acc is None else acc + term
    return acc


def _ffn_kernel(h_ref, nw_ref, w1_ref, w3_ref, w2_ref, o_ref):
    h = h_ref[...]
    xb = _rms(h, nw_ref[...]).astype(BF16)
    o_ref[...] = h + _swiglu_acc(xb, w1_ref, w3_ref, w2_ref, D_FF)


def _ffn(h, norm_w, w1, w3, w2):
    t = h.shape[0]
    tm = min(ROW_TILE, t)
    row = pl.BlockSpec((tm, D_MODEL), lambda i: (i, 0))
    return pl.pallas_call(
        _ffn_kernel,
        grid=(t // tm,),
        in_specs=[row, _const_spec((1, D_MODEL)), _const_spec((D_MODEL, D_FF)),
                  _const_spec((D_MODEL, D_FF)), _const_spec((D_FF, D_MODEL))],
        out_specs=row,
        out_shape=jax.ShapeDtypeStruct((t, D_MODEL), F32),
        compiler_params=_params("parallel"),
        name="ffn",
    )(h, norm_w.astype(F32).reshape(1, -1), w1.astype(BF16), w3.astype(BF16), w2.astype(BF16))


MOE_TILE = 512
TOP_K = 2
DISPATCH_ROWS = 1024
COMBINE_ROWS = 512
DMA_UNROLL = 8


def _sorted_tiles(t):
    return pl.cdiv(TOP_K * t, MOE_TILE) + N_EXPERTS


def _router_kernel(h_ref, nw_ref, wr_ref, xn_ref, member_ref, wsel_ref, esel_ref):
    xn = _rms(h_ref[...], nw_ref[...])
    xn_ref[...] = xn
    x_hi = xn.astype(BF16)
    x_lo = (xn - x_hi.astype(F32)).astype(BF16)
    w = wr_ref[...]
    w_hi = w.astype(BF16)
    w_lo = (w - w_hi.astype(F32)).astype(BF16)
    logits = _dot(x_hi, w_hi) + (_dot(x_lo, w_hi) + _dot(x_hi, w_lo))
    lane = lax.broadcasted_iota(jnp.int32, logits.shape, 1)
    logits = jnp.where(lane < N_EXPERTS, logits, -jnp.inf)
    m1 = jnp.max(logits, axis=-1, keepdims=True)
    i1 = jnp.min(jnp.where(logits == m1, lane, LANES), axis=-1, keepdims=True)
    rest = jnp.where(lane == i1, -jnp.inf, logits)
    m2 = jnp.max(rest, axis=-1, keepdims=True)
    i2 = jnp.min(jnp.where(rest == m2, lane, LANES), axis=-1, keepdims=True)
    e2 = jnp.exp(m2 - m1)
    w1 = 1.0 / (1.0 + e2)
    w2 = e2 / (1.0 + e2)
    member_ref[...] = jnp.where(lane == i1, 1.0, jnp.where(lane == i2, 1.0, 0.0)).astype(BF16)
    wsel_ref[...] = jnp.where(lane == 0, w1, jnp.where(lane == 1, w2, 0.0))
    esel_ref[...] = jnp.where(lane == 0, i1, jnp.where(lane == 1, i2, 0))


def _router(h, norm_w, w_router):
    t = h.shape[0]
    tm = min(ROW_TILE, t)
    wr = jnp.pad(w_router.astype(F32), ((0, 0), (0, LANES - N_EXPERTS)))
    row = lambda w: pl.BlockSpec((tm, w), lambda i: (i, 0))
    return pl.pallas_call(
        _router_kernel,
        grid=(t // tm,),
        in_specs=[row(D_MODEL), _const_spec((1, D_MODEL)), _const_spec((D_MODEL, LANES))],
        out_specs=[row(D_MODEL), row(LANES), row(LANES), row(LANES)],
        out_shape=[jax.ShapeDtypeStruct((t, D_MODEL), F32), jax.ShapeDtypeStruct((t, LANES), BF16),
                   jax.ShapeDtypeStruct((t, LANES), F32), jax.ShapeDtypeStruct((t, LANES), jnp.int32)],
        compiler_params=_params("parallel"),
        name="moe_router",
    )(h, norm_w.astype(F32).reshape(1, -1), wr)


def _positions_kernel(member_ref, esel_ref, pos_ref, meta_ref, cnt_ref, carry_ref, off_ref):
    phase = pl.program_id(0)
    i = pl.program_id(1)
    m = member_ref[...]
    tp = m.shape[0]
    col_sum = _dot(jnp.ones((SUBLANES, tp), BF16), m)

    @pl.when((phase == 0) & (i == 0))
    def _():
        cnt_ref[...] = jnp.zeros(cnt_ref.shape, F32)

    @pl.when(phase == 0)
    def _():
        cnt_ref[...] += col_sum

    @pl.when((phase == 1) & (i == 0))
    def _():
        tiles = jnp.floor((cnt_ref[...] + (MOE_TILE - 1.0)) * (1.0 / MOE_TILE))
        r = lax.broadcasted_iota(jnp.int32, (LANES, LANES), 0)
        c = lax.broadcasted_iota(jnp.int32, (LANES, LANES), 1)
        before = jnp.where(r < c, 1.0, 0.0).astype(BF16)
        first_tile = _dot(tiles.astype(BF16), before)
        off_ref[...] = first_tile * MOE_TILE
        carry_ref[...] = jnp.zeros(carry_ref.shape, F32)
        row = lax.broadcasted_iota(jnp.int32, (SUBLANES, LANES), 0)
        meta_ref[...] = jnp.where(row == 0, first_tile,
                                  jnp.where(row == 1, tiles, cnt_ref[...])).astype(jnp.int32)

    @pl.when(phase == 1)
    def _():
        r = lax.broadcasted_iota(jnp.int32, (tp, tp), 0)
        c = lax.broadcasted_iota(jnp.int32, (tp, tp), 1)
        earlier = jnp.where(r > c, 1.0, 0.0).astype(BF16)
        posm = off_ref[0:1, :] + carry_ref[0:1, :] + _dot(earlier, m)
        lane = lax.broadcasted_iota(jnp.int32, posm.shape, 1)
        e = esel_ref[...]
        p0 = jnp.sum(jnp.where(lane == e[:, 0:1], posm, 0.0), axis=-1, keepdims=True)
        p1 = jnp.sum(jnp.where(lane == e[:, 1:2], posm, 0.0), axis=-1, keepdims=True)
        pos_ref[...] = jnp.where(lane == 0, p0, jnp.where(lane == 1, p1, 0.0)).astype(jnp.int32)
        carry_ref[...] += col_sum


def _positions(member, esel):
    t = member.shape[0]
    tp = min(ROW_TILE, t)
    return pl.pallas_call(
        _positions_kernel,
        grid=(2, t // tp),
        in_specs=[pl.BlockSpec((tp, LANES), lambda p, i: (i, 0)),
                  pl.BlockSpec((tp, LANES), lambda p, i: (i * p, 0))],
        out_specs=[pl.BlockSpec((tp, LANES), lambda p, i: (i * p, 0)),
                   pl.BlockSpec((SUBLANES, LANES), lambda p, i: (0, 0))],
        out_shape=[jax.ShapeDtypeStruct((t, LANES), jnp.int32),
                   jax.ShapeDtypeStruct((SUBLANES, LANES), jnp.int32)],
        scratch_shapes=[pltpu.VMEM((SUBLANES, LANES), F32)] * 3,
        compiler_params=_params("arbitrary", "arbitrary"),
        name="moe_positions",
    )(member, esel)


def _row_copy(src_ref, src_row, dst_ref, dst_row, sem):
    return pltpu.make_async_copy(src_ref.at[pl.ds(src_row, 1)], dst_ref.at[pl.ds(dst_row, 1)], sem)


def _dispatch_kernel(pad_lo_ref, pad_hi_ref, pos_ref, xn_ref, xs_ref, zero_ref, sem):
    rows = xn_ref.shape[0]

    @pl.when(pl.program_id(0) == 0)
    def _():
        zero_ref[...] = jnp.zeros(zero_ref.shape, F32)
        for e in range(N_EXPERTS):
            lo, hi = pad_lo_ref[e], pad_hi_ref[e]

            def fill(r, carry):
                _row_copy(zero_ref, 0, xs_ref, r, sem).start()
                return carry

            def filled(r, carry):
                _row_copy(zero_ref, 0, xs_ref, r, sem).wait()
                return carry

            lax.fori_loop(lo, hi, fill, 0)
            lax.fori_loop(lo, hi, filled, 0)

        def tail_copy(j):
            return pltpu.make_async_copy(
                zero_ref, xs_ref.at[pl.ds(pl.multiple_of(j * SUBLANES, SUBLANES), SUBLANES)], sem)

        def tail_fill(j, carry):
            tail_copy(j).start()
            return carry

        def tail_filled(j, carry):
            tail_copy(j).wait()
            return carry

        first = lax.shift_right_logical(pad_hi_ref[N_EXPERTS - 1], int(math.log2(SUBLANES)))
        lax.fori_loop(first, xs_ref.shape[0] // SUBLANES, tail_fill, 0)
        lax.fori_loop(first, xs_ref.shape[0] // SUBLANES, tail_filled, 0)

    def copies(r):
        return [_row_copy(xn_ref, r, xs_ref, pos_ref[0, k, r], sem) for k in range(TOP_K)]

    def issue(r, carry):
        for k, cp in enumerate(copies(r)):
            cp.start(priority=k)
        return carry

    def drain(r, carry):
        for cp in copies(r):
            cp.wait()
        return carry

    lax.fori_loop(0, rows, issue, 0, unroll=DMA_UNROLL)
    lax.fori_loop(0, rows, drain, 0, unroll=DMA_UNROLL)


def _slot_major(pos, rows):
    t = pos.shape[0]
    return jnp.transpose(pos[:, :TOP_K].reshape(t // rows, rows, TOP_K), (0, 2, 1))


def _dispatch(xn, pos, pad_lo, pad_hi, n_sorted):
    t = xn.shape[0]
    rows = min(DISPATCH_ROWS, t)
    return pl.pallas_call(
        _dispatch_kernel,
        grid_spec=pltpu.PrefetchScalarGridSpec(
            num_scalar_prefetch=2,
            grid=(t // rows,),
            in_specs=[pl.BlockSpec((1, TOP_K, rows), lambda i, lo, hi: (i, 0, 0),
                                   memory_space=pltpu.SMEM),
                      pl.BlockSpec((rows, D_MODEL), lambda i, lo, hi: (i, 0))],
            out_specs=pl.BlockSpec(memory_space=pl.ANY),
            scratch_shapes=[pltpu.VMEM((SUBLANES, D_MODEL), F32), pltpu.SemaphoreType.DMA(())]),
        out_shape=jax.ShapeDtypeStruct((n_sorted, D_MODEL), F32),
        compiler_params=_params("arbitrary"),
        name="moe_dispatch",
    )(pad_lo, pad_hi, _slot_major(pos, rows), xn)


MOE_FF_SPLIT = 2
MOE_FF_BLOCK = D_FF_EXPERT // MOE_FF_SPLIT


def _experts_kernel(tile_expert_ref, n_used_ref, xs_ref, w1_ref, w3_ref, w2_ref, y_ref):
    del tile_expert_ref
    used = pl.program_id(0) < n_used_ref[0]
    f = pl.program_id(1)

    def part():
        xb = xs_ref[...].astype(BF16)
        return _swiglu_acc(xb, w1_ref, w3_ref, w2_ref, MOE_FF_BLOCK, lead=(0,))

    @pl.when(used & (f == 0))
    def _():
        y_ref[...] = part()

    @pl.when(used & (f > 0))
    def _():
        y_ref[...] += part()

    @pl.when(jnp.logical_not(used) & (f == 0))
    def _():
        y_ref[...] = jnp.zeros(y_ref.shape, F32)


def _experts(xs, tile_expert, n_used, w1, w3, w2):
    n_tiles = xs.shape[0] // MOE_TILE
    row = pl.BlockSpec((MOE_TILE, D_MODEL), lambda i, f, te, nu: (i, 0))
    row_in = pl.BlockSpec((MOE_TILE, D_MODEL), lambda i, f, te, nu: (jnp.minimum(i, nu[0] - 1), 0))
    ff = lambda i, f, nu: jnp.where(i < nu[0], f, MOE_FF_SPLIT - 1)
    w_up = pl.BlockSpec((1, D_MODEL, MOE_FF_BLOCK), lambda i, f, te, nu: (te[i], 0, ff(i, f, nu)))
    w_down = pl.BlockSpec((1, MOE_FF_BLOCK, D_MODEL), lambda i, f, te, nu: (te[i], ff(i, f, nu), 0))
    return pl.pallas_call(
        _experts_kernel,
        grid_spec=pltpu.PrefetchScalarGridSpec(
            num_scalar_prefetch=2,
            grid=(n_tiles, MOE_FF_SPLIT),
            in_specs=[row_in, w_up, w_up, w_down],
            out_specs=row),
        out_shape=jax.ShapeDtypeStruct(xs.shape, F32),
        compiler_params=_params("arbitrary", "arbitrary"),
        name="moe_experts",
    )(tile_expert, n_used, xs, w1.astype(BF16), w3.astype(BF16), w2.astype(BF16))


def _combine_kernel(pos_ref, next_pos_ref, h_ref, wsel_ref, fw_ref, y_ref, o_ref, ybuf_ref, sems,
                    *, final_norm):
    rows = h_ref.shape[0]
    i = pl.program_id(0)
    last = pl.num_programs(0) - 1

    def copies(p_ref, buf, r):
        return [_row_copy(y_ref, p_ref[0, k, r], ybuf_ref.at[buf, k], r, sems.at[buf])
                for k in range(TOP_K)]

    def start(p_ref, buf):
        def issue(r, carry):
            for k, cp in enumerate(copies(p_ref, buf, r)):
                cp.start(priority=k)
            return carry
        lax.fori_loop(0, rows, issue, 0, unroll=DMA_UNROLL)

    def wait(p_ref, buf):
        def drain(r, carry):
            for cp in copies(p_ref, buf, r):
                cp.wait()
            return carry
        lax.fori_loop(0, rows, drain, 0, unroll=DMA_UNROLL)

    cur = lax.rem(i, 2)

    @pl.when(i == 0)
    def _():
        start(pos_ref, 0)

    @pl.when(i < last)
    def _():
        start(next_pos_ref, 1 - cur)

    wait(pos_ref, cur)
    w = wsel_ref[...]
    out = h_ref[...] + w[:, 0:1] * ybuf_ref[cur, 0] + w[:, 1:2] * ybuf_ref[cur, 1]
    if final_norm:
        out = _rms(out, fw_ref[...])
    o_ref[...] = out


def _combine(h, wsel, pos, y_sorted, final_w, final_norm):
    t = h.shape[0]
    rows = min(COMBINE_ROWS, t)
    steps = t // rows
    slots = _slot_major(pos, rows)
    return pl.pallas_call(
        functools.partial(_combine_kernel, final_norm=final_norm),
        grid=(steps,),
        in_specs=[pl.BlockSpec((1, TOP_K, rows), lambda i: (i, 0, 0), memory_space=pltpu.SMEM),
                  pl.BlockSpec((1, TOP_K, rows), lambda i: (jnp.minimum(i + 1, steps - 1), 0, 0),
                               memory_space=pltpu.SMEM),
                  pl.BlockSpec((rows, D_MODEL), lambda i: (i, 0)),
                  pl.BlockSpec((rows, LANES), lambda i: (i, 0)),
                  _const_spec((1, D_MODEL)),
                  pl.BlockSpec(memory_space=pl.ANY)],
        out_specs=pl.BlockSpec((rows, D_MODEL), lambda i: (i, 0)),
        out_shape=jax.ShapeDtypeStruct((t, D_MODEL), F32),
        scratch_shapes=[pltpu.VMEM((2, TOP_K, rows, D_MODEL), F32), pltpu.SemaphoreType.DMA((2,))],
        compiler_params=_params("arbitrary"),
        name="moe_combine",
    )(slots, slots, h, wsel, final_w.astype(F32).reshape(1, -1), y_sorted)


def _moe(h, norm_w, w_router, w1, w3, w2, final_w, final_norm):
    t = h.shape[0]
    n_tiles = _sorted_tiles(t)
    xn, member, wsel, esel = _router(h, norm_w, w_router)
    pos, meta = _positions(member, esel)
    first_tile = meta[0, :N_EXPERTS]
    last_tile = first_tile + meta[1, :N_EXPERTS]
    tile = jnp.arange(n_tiles, dtype=jnp.int32)
    tile_expert = jnp.minimum(jnp.sum(tile[:, None] >= last_tile[None, :], axis=1), N_EXPERTS - 1)
    pad_lo = first_tile * MOE_TILE + meta[2, :N_EXPERTS]
    xs = _dispatch(xn, pos, pad_lo, last_tile * MOE_TILE, n_tiles * MOE_TILE)
    ys = _experts(xs, tile_expert.astype(jnp.int32), last_tile[N_EXPERTS - 1:], w1, w3, w2)
    return _combine(h, wsel, pos, ys, final_w, final_norm)


def _final_norm_kernel(h_ref, w_ref, o_ref):
    o_ref[...] = _rms(h_ref[...], w_ref[...])


def _final_norm(h, w):
    t = h.shape[0]
    tm = min(ROW_TILE, t)
    row = pl.BlockSpec((tm, D_MODEL), lambda i: (i, 0))
    return pl.pallas_call(
        _final_norm_kernel, grid=(t // tm,), in_specs=[row, _const_spec((1, D_MODEL))],
        out_specs=row, out_shape=jax.ShapeDtypeStruct((t, D_MODEL), F32),
        compiler_params=_params("parallel"), name="final_norm",
    )(h, w.astype(F32).reshape(1, -1))


def _mixing_block(h, bsz, seq, cos2, sin2, norm_w, w_in, b_gate,
                  s5_lam_re, s5_lam_im, s5_b_re, s5_b_im, s5_c_re, s5_c_im, s5_d, s5_log_dt, s5_w_glu,
                  ssd_conv_w, ssd_conv_b, ssd_dt_bias, ssd_a_log, ssd_d, ssd_norm,
                  ret_norm,
                  lru_conv_w, lru_conv_b, lru_wa, lru_ba, lru_wx, lru_bx, lru_lam,
                  w_branch, w_out):
    w_cat, w_gate_half = _pack_w_in(w_in)
    (u_s5, z_ssd, xbc_ssd, dt_ssd, q_ret, k_ret, v_ret, g_ret, x_lru,
     gate_lru) = _inproj(h, norm_w, w_cat)
    seq3 = lambda a: a.reshape(bsz, seq, a.shape[-1])
    tables = _s5_tables(s5_lam_re, s5_lam_im, s5_b_re, s5_b_im, s5_c_re, s5_c_im, s5_log_dt,
                        seq // S5_SUB)
    y_s5 = _s5_scan(u_s5, tables, bsz, seq)
    y_ssd = _ssd(seq3(z_ssd), seq3(xbc_ssd), seq3(dt_ssd), ssd_conv_w, ssd_conv_b, ssd_dt_bias,
                 ssd_a_log, ssd_d, ssd_norm)
    y_ret = _retention(seq3(q_ret), seq3(k_ret), seq3(v_ret), seq3(g_ret), cos2, sin2, ret_norm)
    y_lru = _lru(seq3(x_lru), seq3(gate_lru), lru_conv_w, lru_conv_b, lru_wa, lru_ba, lru_wx,
                 lru_bx, lru_lam)
    flat = lambda a: a.reshape(bsz * seq, a.shape[-1])
    return _merge(h, norm_w, u_s5, y_s5, flat(y_ssd), flat(y_ret), flat(y_lru), w_gate_half,
                  b_gate, s5_d, s5_w_glu, w_branch, w_out)


def kernel(x, mem, positions, norm_mix, w_in, b_gate, s5_lam_re, s5_lam_im, s5_b_re, s5_b_im, s5_c_re, s5_c_im, s5_d, s5_log_dt, s5_w_glu, ssd_conv_w, ssd_conv_b, ssd_dt_bias, ssd_a_log, ssd_d, ssd_norm, ret_norm, lru_conv_w, lru_conv_b, lru_wa, lru_ba, lru_wx, lru_bx, lru_lam, w_branch, w_out, norm_xa, norm_mem, xa_wq, xa_wk, xa_wv, xa_wo, norm_ffn, ffn_w1, ffn_w3, ffn_w2, moe_router, moe_w1, moe_w3, moe_w2, norm_final):
    bsz, seq, _ = x.shape
    depth = norm_mix.shape[0]
    cos2, sin2 = _rope_tables(positions)
    h = x.reshape(bsz * seq, D_MODEL)
    for i in range(depth):
        h = _mixing_block(h, bsz, seq, cos2, sin2, norm_mix[i], w_in[i], b_gate[i],
                          s5_lam_re[i], s5_lam_im[i], s5_b_re[i], s5_b_im[i], s5_c_re[i], s5_c_im[i],
                          s5_d[i], s5_log_dt[i], s5_w_glu[i],
                          ssd_conv_w[i], ssd_conv_b[i], ssd_dt_bias[i], ssd_a_log[i], ssd_d[i],
                          ssd_norm[i], ret_norm[i],
                          lru_conv_w[i], lru_conv_b[i], lru_wa[i], lru_ba[i], lru_wx[i], lru_bx[i],
                          lru_lam[i], w_branch[i], w_out[i])
        k, v = _kv(mem, norm_mem[i], xa_wk[i], xa_wv[i])
        h = _xattn(h.reshape(bsz, seq, D_MODEL), norm_xa[i], xa_wq[i], k, v, xa_wo[i])
        h = h.reshape(bsz * seq, D_MODEL)
        last = i == depth - 1
        if i % 2 == 0:
            h = _ffn(h, norm_ffn[i], ffn_w1[i // 2], ffn_w3[i // 2], ffn_w2[i // 2])
            if last:
                h = _final_norm(h, norm_final)
        else:
            h = _moe(h, norm_ffn[i], moe_router[i // 2], moe_w1[i // 2], moe_w3[i // 2],
                     moe_w2[i // 2], norm_final, last)
    return h.reshape(bsz, seq, D_MODEL)
```
